```python
import math
import jax, jax.numpy as jnp
from jax import lax
import numpy as np

D_MODEL = 4096
BATCH = 2
SEQ = 4096
DEPTH = 2

CTX_LEN = 256
GRID_W = 64

NA_HEADS = 8
NA_HEAD_DIM = 128
NA_WIDTH = NA_HEADS * NA_HEAD_DIM
NA_WIN_R = 8
NA_WIN_C = 16

HY_WIDTH = 1024
HY_ORDER = 2
HY_BANDS = 16
HY_POS_DIM = 1 + 2 * HY_BANDS
HY_FILTER_HIDDEN = 64
HY_DECAY_TARGET = 1e-2
HY_FAST_DECAY = 0.3
HY_SLOW_DECAY = 1.5

GLA_HEADS = 4
GLA_DK = 128
GLA_DV = 256
GLA_QK_WIDTH = GLA_HEADS * GLA_DK
GLA_V_WIDTH = GLA_HEADS * GLA_DV
GLA_GATE_RANK = 16
GLA_TAU = 16.0
GLA_CHUNK = 64

S5_WIDTH = 1024
S5_GROUP = 16
S5_GROUPS = S5_WIDTH // S5_GROUP
S5_STATE = 64

N_BRANCH = 4
BRANCH_WIDTH = 1024

N_EXPERTS = 16
EXPERT_FF = 1024
EC_CAPACITY = 2

ROPE_BASE = 10000.0
RMS_EPS = 1e-6
NEG_INF = -1e30
F32 = jnp.float32

IN_SPLITS = (3 * NA_WIDTH, (HY_ORDER + 1) * HY_WIDTH, GLA_QK_WIDTH, GLA_QK_WIDTH, GLA_V_WIDTH, GLA_V_WIDTH, 2 * GLA_GATE_RANK, S5_WIDTH)
IN_COLS = sum(IN_SPLITS)

kernel_name = 'hybrid_na_hyena_gla_s5_ecmoe_dit'


def _ident(t):
    return t


def _rev(t):
    return t[:, ::-1]


def rms_norm(x, g):
    xf = x.astype(F32)
    y = xf * lax.rsqrt(jnp.mean(xf * xf, axis=-1, keepdims=True) + RMS_EPS)
    return (y * g.astype(F32)).astype(x.dtype)


def modulation(cvec, w, b):
    m = jax.nn.silu(cvec) @ w + b
    parts = jnp.split(m, 6, axis=-1)
    if cvec.ndim == 2:
        parts = [p[:, None, :] for p in parts]
    return parts


def split_in(z):
    points = [int(p) for p in np.cumsum(np.array(IN_SPLITS))[:-1]]
    return jnp.split(z, points, axis=-1)


def rope_1d(x, pos):
    half = x.shape[-1] // 2
    inv = ROPE_BASE ** (-jnp.arange(half, dtype=F32) / half)
    ang = pos[:, None] * inv[None, :]
    cos = jnp.cos(ang)[None, :, None, :]
    sin = jnp.sin(ang)[None, :, None, :]
    xf = x.astype(F32)
    x1, x2 = xf[..., :half], xf[..., half:]
    return jnp.concatenate([x1 * cos - x2 * sin, x2 * cos + x1 * sin], axis=-1).astype(x.dtype)


def axial_rope(x):
    t = jnp.arange(x.shape[1])
    rows = (t // GRID_W).astype(F32)
    cols = (t % GRID_W).astype(F32)
    half = x.shape[-1] // 2
    return jnp.concatenate([rope_1d(x[..., :half], rows), rope_1d(x[..., half:], cols)], axis=-1)


def dense_attention(q, k, v):
    scale = q.shape[-1] ** -0.5
    s = jnp.einsum('bqhd,bkhd->bhqk', q, k, preferred_element_type=F32) * scale
    p = jax.nn.softmax(s, axis=-1).astype(v.dtype)
    o = jnp.einsum('bhqk,bkhd->bqhd', p, v)
    return o.reshape(q.shape[0], q.shape[1], -1)


def neighbourhood_attention(q, k, v, kc, vc, rpb):
    B, L, H, dh = q.shape
    rows = L // GRID_W
    wr = min(NA_WIN_R, rows)
    n_win = wr * GRID_W
    scale = dh ** -0.5
    r = jnp.arange(rows)
    row_idx = jnp.clip(r - wr // 2, 0, rows - wr)[:, None] + jnp.arange(wr)[None, :]
    j = jnp.arange(GRID_W)
    col_start = jnp.clip(j - NA_WIN_C // 2, 0, GRID_W - NA_WIN_C)
    col_ok = (j[None, :] >= col_start[:, None]) & (j[None, :] < col_start[:, None] + NA_WIN_C)
    row_off = row_idx - r[:, None] + (NA_WIN_R - 1)
    col_off = jnp.clip(j[None, :] - j[:, None] + (NA_WIN_C - 1), 0, 2 * NA_WIN_C - 2)
    bias = rpb.astype(F32)[:, row_off[:, :, None, None], col_off[None, None, :, :]]
    bias = bias.transpose(1, 0, 3, 2, 4)
    qg = q.reshape(B, rows, GRID_W, H, dh)
    kg = jnp.take(k.reshape(B, rows, GRID_W, H, dh), row_idx, axis=1)
    vg = jnp.take(v.reshape(B, rows, GRID_W, H, dh), row_idx, axis=1).reshape(B, rows, n_win, H, dh)
    s_win = jnp.einsum('brqhd,brwkhd->brhqwk', qg, kg, preferred_element_type=F32) * scale
    s_win = jnp.where(col_ok[:, None, :], s_win + bias[None], NEG_INF).reshape(B, rows, H, GRID_W, n_win)
    s_ctx = jnp.einsum('brqhd,bkhd->brhqk', qg, kc, preferred_element_type=F32) * scale
    p = jax.nn.softmax(jnp.concatenate([s_win, s_ctx], axis=-1), axis=-1).astype(v.dtype)
    o = jnp.einsum('brhqk,brkhd->brqhd', p[..., :n_win], vg) + jnp.einsum('brhqk,bkhd->brqhd', p[..., n_win:], vc)
    return o.reshape(B, L, H * dh)


def na_mixer(zl, zc, rpb, need_ctx):
    B, L, _ = zl.shape
    Lc = zc.shape[1]
    ql, kl, vl = [t.reshape(B, L, NA_HEADS, NA_HEAD_DIM) for t in jnp.split(zl, 3, axis=-1)]
    qc, kc, vc = [t.reshape(B, Lc, NA_HEADS, NA_HEAD_DIM) for t in jnp.split(zc, 3, axis=-1)]
    out_l = neighbourhood_attention(ql, kl, vl, kc, vc, rpb)
    out_c = dense_attention(qc, kc, vc) if need_ctx else None
    return out_l, out_c


def short_conv3(u, w, b):
    up = jnp.pad(u, ((0, 0), (1, 1), (0, 0)))
    return up[:, :-2] * w[0] + up[:, 1:-1] * w[1] + up[:, 2:] * w[2] + b


def hyena_filters(L, w1, b1, w2, b2, w3, freq):
    t = jnp.arange(L, dtype=F32)
    t01 = t / max(L - 1, 1)
    bands = jnp.linspace(1e-4, HY_BANDS - 1, HY_BANDS, dtype=F32)
    ang = (2.0 * math.pi / L) * t[:, None] * bands[None, :]
    z = jnp.concatenate([t01[:, None], jnp.cos(ang), -jnp.sin(ang)], axis=-1)
    f = freq.astype(F32)
    h = jnp.sin(f * (z @ w1.astype(F32) + b1.astype(F32)))
    h = jnp.sin(f * (h @ w2.astype(F32) + b2.astype(F32)))
    h = (h @ w3.astype(F32)).reshape(L, 2, HY_ORDER, HY_WIDTH)
    deltas = jnp.abs(jnp.linspace(math.log(HY_DECAY_TARGET) / HY_SLOW_DECAY, math.log(HY_DECAY_TARGET) / HY_FAST_DECAY, HY_WIDTH, dtype=F32))
    h = h * jnp.exp(-t01[:, None] * deltas[None, :])[:, None, None, :]
    kern = jnp.concatenate([h[:, 0], jnp.zeros((1, HY_ORDER, HY_WIDTH), F32), h[:0:-1, 1]], axis=0)
    return kern / jnp.sum(jnp.abs(kern), axis=0, keepdims=True)


def fft_long_conv(u, kern, bias):
    L = u.shape[1]
    n = 2 * L
    spec = jnp.fft.rfft(u, n=n, axis=1) * jnp.fft.rfft(kern, n=n, axis=0)[None]
    return jnp.fft.irfft(spec, n=n, axis=1)[:, :L] + u * bias


def hyena_mixer(z, conv_w, conv_b, w1, b1, w2, b2, w3, freq, bias):
    L = z.shape[1]
    zc = short_conv3(z.astype(F32), conv_w.astype(F32), conv_b.astype(F32))
    parts = jnp.split(zc, HY_ORDER + 1, axis=-1)
    kern = hyena_filters(L, w1, b1, w2, b2, w3, freq)
    y = parts[0]
    for o in range(HY_ORDER):
        y = parts[o + 1] * fft_long_conv(y, kern[:, o], bias[o].astype(F32))
    return y


def gla_chunked(q, k, v, log_a, s0):
    B, L, H, dk = q.shape
    dv = v.shape[-1]
    n = L // GLA_CHUNK

    def blocks(t):
        return t.reshape(B, n, GLA_CHUNK, H, t.shape[-1]).transpose(1, 0, 3, 2, 4)

    qb, kb, vb, ab = blocks(q), blocks(k), blocks(v), blocks(log_a)
    bcum = jnp.cumsum(ab, axis=3)
    blast = bcum[:, :, :, -1:, :]
    q_in = qb * jnp.exp(bcum)
    k_in = kb * jnp.exp(-bcum)
    k_st = kb * jnp.exp(blast - bcum)
    lower = jnp.tril(jnp.ones((GLA_CHUNK, GLA_CHUNK), bool))
    att = jnp.where(lower, jnp.einsum('nbhtd,nbhsd->nbhts', q_in, k_in), 0.0)
    o_intra = jnp.einsum('nbhts,nbhsv->nbhtv', att, vb)
    kv = jnp.einsum('nbhsd,nbhsv->nbhdv', k_st, vb)
    decay = jnp.exp(blast[:, :, :, 0, :])

    def step(state, inp):
        d, kv_n = inp
        return d[..., None] * state + kv_n, state

    s_fin, s_prev = lax.scan(step, s0, (decay, kv))
    o = o_intra + jnp.einsum('nbhtd,nbhdv->nbhtv', q_in, s_prev)
    return o.transpose(1, 0, 3, 2, 4).reshape(B, L, H, dv), s_fin


def gla_prep(zq, zk, zv, za, w2, b2, rotary):
    B, L, _ = zq.shape
    q = zq.reshape(B, L, GLA_HEADS, GLA_DK)
    k = zk.reshape(B, L, GLA_HEADS, GLA_DK)
    if rotary:
        q, k = axial_rope(q), axial_rope(k)
    q = q.astype(F32) * GLA_DK ** -0.5
    k = k.astype(F32)
    v = zv.astype(F32).reshape(B, L, GLA_HEADS, GLA_DV)
    a = za.astype(F32).reshape(B, L, 2, GLA_GATE_RANK)
    log_a = jax.nn.log_sigmoid(jnp.einsum('bldr,drk->bldk', a, w2.astype(F32)) + b2.astype(F32)) / GLA_TAU
    return q, k, v, log_a.reshape(B, L, 2, GLA_HEADS, GLA_DK)


def gla_out(o, g, norm_g):
    B, L = o.shape[:2]
    o = o * lax.rsqrt(jnp.mean(o * o, axis=-1, keepdims=True) + RMS_EPS) * norm_g.astype(F32)
    return o.reshape(B, L, GLA_V_WIDTH) * jax.nn.silu(g.astype(F32))


def gla_mixer(lat, ctxp, w2, b2, norm_g, need_ctx):
    zq_l, zk_l, zv_l, zg_l, za_l = lat
    zq_c, zk_c, zv_c, zg_c, za_c = ctxp
    ql, kl, vl, al = gla_prep(zq_l, zk_l, zv_l, za_l, w2, b2, True)
    qc, kc, vc, ac = gla_prep(zq_c, zk_c, zv_c, za_c, w2, b2, False)
    s0 = jnp.zeros((qc.shape[0], GLA_HEADS, GLA_DK, GLA_DV), F32)
    oc_f, sc_f = gla_chunked(qc, kc, vc, ac[:, :, 0], s0)
    oc_b, sc_b = gla_chunked(_rev(qc), _rev(kc), _rev(vc), _rev(ac[:, :, 1]), s0)
    ol_f, _ = gla_chunked(ql, kl, vl, al[:, :, 0], sc_f)
    ol_b, _ = gla_chunked(_rev(ql), _rev(kl), _rev(vl), _rev(al[:, :, 1]), sc_b)
    out_l = gla_out(ol_f + _rev(ol_b), zg_l, norm_g)
    out_c = gla_out(oc_f + _rev(oc_b), zg_c, norm_g) if need_ctx else None
    return out_l, out_c


def s5_discretise(a_re, a_im, log_dt, b_re, b_im):
    A = lax.complex(a_re.astype(F32), a_im.astype(F32))
    dt = jnp.exp(log_dt.astype(F32))[:, None]
    a_bar = jnp.exp(A * dt)
    b_bar = ((a_bar - 1.0) / A)[..., None] * lax.complex(b_re.astype(F32), b_im.astype(F32))
    return a_bar, b_bar


def s5_scan(a_bar, b_bar, u, x0):
    bu = jnp.einsum('gpi,blgi->blgp', b_bar, u.astype(jnp.complex64))
    bu = bu.at[:, 0].add(a_bar * x0)
    a = jnp.broadcast_to(a_bar, bu.shape)

    def combine(left, right):
        return left[0] * right[0], right[0] * left[1] + right[1]

    _, xs = lax.associative_scan(combine, (a, bu), axis=1)
    return xs


def s5_glu(y, w, b):
    y = jax.nn.gelu(y)
    return y * jax.nn.sigmoid(y @ w.astype(F32) + b.astype(F32))


def s5_mixer(uc, ul, a_re, a_im, log_dt, b_re, b_im, c_re, c_im, d_skip, glu_w, glu_b, need_ctx):
    B, L, _ = ul.shape
    Lc = uc.shape[1]
    ulg = ul.astype(F32).reshape(B, L, S5_GROUPS, S5_GROUP)
    ucg = uc.astype(F32).reshape(B, Lc, S5_GROUPS, S5_GROUP)
    d = d_skip.astype(F32).reshape(S5_GROUPS, S5_GROUP)
    yl = d * ulg
    yc = d * ucg if need_ctx else None
    x0 = jnp.zeros((B, S5_GROUPS, S5_STATE), jnp.complex64)
    for direction in range(2):
        a_bar, b_bar = s5_discretise(a_re[direction], a_im[direction], log_dt[direction], b_re[direction], b_im[direction])
        c_mat = lax.complex(c_re[direction].astype(F32), c_im[direction].astype(F32))
        orient = _ident if direction == 0 else _rev
        xs_c = s5_scan(a_bar, b_bar, orient(ucg), x0)
        xs_l = s5_scan(a_bar, b_bar, orient(ulg), xs_c[:, -1])
        yl = yl + orient(jnp.einsum('gip,blgp->blgi', c_mat, xs_l).real)
        if need_ctx:
            yc = yc + orient(jnp.einsum('gip,blgp->blgi', c_mat, xs_c).real)
    out_l = s5_glu(yl.reshape(B, L, S5_WIDTH), glu_w, glu_b)
    out_c = s5_glu(yc.reshape(B, Lc, S5_WIDTH), glu_w, glu_b) if need_ctx else None
    return out_l, out_c


def merge_branches(h, branches, w_branch, w_gate, b_gate, w_out):
    merged = jnp.zeros(h.shape, h.dtype)
    for i, br in enumerate(branches):
        gate = jax.nn.sigmoid(h @ w_gate[i] + b_gate[i])
        merged = merged + gate * (br.astype(h.dtype) @ w_branch[i])
    return merged @ w_out


def moe_ec(h, router_w, w_gu, w_down):
    B, N, D = h.shape
    cap = max(1, EC_CAPACITY * N // N_EXPERTS)
    aff = jax.nn.softmax((h @ router_w).astype(F32), axis=-1)
    gate, idx = lax.top_k(jnp.swapaxes(aff, 1, 2), cap)
    xs = jax.vmap(lambda hb, ib: hb[ib])(h, idx)
    gu = jnp.einsum('becd,edf->becf', xs, w_gu)
    g_, u_ = jnp.split(gu, 2, axis=-1)
    y = jnp.einsum('becf,efd->becd', jax.nn.silu(g_) * u_, w_down) * gate[..., None].astype(h.dtype)
    return jax.vmap(lambda ib, yb: jnp.zeros((N, D), yb.dtype).at[ib.reshape(-1)].add(yb.reshape(-1, D)))(idx, y)


def setup_inputs(seed: int = 0) -> dict:
    key = jax.random.key(seed)
    ks = iter(jax.random.split(key, 48))

    def nrm(shape, scale=1.0):
        return jax.random.normal(next(ks), shape, jnp.float32) * scale

    def gain(shape):
        return 1.0 + nrm(shape, 0.02)

    D = D_MODEL
    G, P, I = S5_GROUPS, S5_STATE, S5_GROUP
    hid = HY_FILTER_HIDDEN
    return {
        'x': nrm((BATCH, SEQ, D)),
        'c': nrm((BATCH, D)),
        'ctx': nrm((BATCH, CTX_LEN, D)),
        'c_ctx': nrm((D,)),
        'ada_w': nrm((DEPTH, D, 6 * D), 0.5 * D ** -0.5),
        'ada_b': nrm((DEPTH, 6 * D), 0.02),
        'mix_pre_g': gain((DEPTH, D)),
        'mix_post_g': gain((DEPTH, D)),
        'ffn_pre_g': gain((DEPTH, D)),
        'ffn_post_g': gain((DEPTH, D)),
        'w_in': nrm((DEPTH, D, IN_COLS), D ** -0.5),
        'na_rpb': nrm((DEPTH, NA_HEADS, 2 * NA_WIN_R - 1, 2 * NA_WIN_C - 1), 0.02),
        'hy_conv_w': nrm((DEPTH, 3, (HY_ORDER + 1) * HY_WIDTH), 3 ** -0.5),
        'hy_conv_b': nrm((DEPTH, (HY_ORDER + 1) * HY_WIDTH), 0.02),
        'hy_w1': nrm((DEPTH, HY_POS_DIM, hid), HY_POS_DIM ** -0.5),
        'hy_b1': nrm((DEPTH, hid), 0.1),
        'hy_w2': nrm((DEPTH, hid, hid), hid ** -0.5),
        'hy_b2': nrm((DEPTH, hid), 0.1),
        'hy_w3': nrm((DEPTH, hid, 2 * HY_ORDER * HY_WIDTH), hid ** -0.5),
        'hy_freq': gain((DEPTH, hid)),
        'hy_bias': nrm((DEPTH, HY_ORDER, HY_WIDTH), 0.1),
        'gla_w2': nrm((DEPTH, 2, GLA_GATE_RANK, GLA_QK_WIDTH), GLA_GATE_RANK ** -0.5),
        'gla_b2': nrm((DEPTH, 2, GLA_QK_WIDTH), 0.1),
        'gla_norm_g': gain((DEPTH, GLA_DV)),
        's5_a_re': -0.5 + nrm((DEPTH, 2, G, P), 0.01),
        's5_a_im': math.pi * jnp.arange(P, dtype=jnp.float32) + nrm((DEPTH, 2, G, P), 0.01),
        's5_log_dt': jax.random.uniform(next(ks), (DEPTH, 2, G), jnp.float32, math.log(1e-3), math.log(1e-1)),
        's5_b_re': nrm((DEPTH, 2, G, P, I), (2 * I) ** -0.5),
        's5_b_im': nrm((DEPTH, 2, G, P, I), (2 * I) ** -0.5),
        's5_c_re': nrm((DEPTH, 2, G, I, P), P ** -0.5),
        's5_c_im': nrm((DEPTH, 2, G, I, P), P ** -0.5),
        's5_d': nrm((DEPTH, S5_WIDTH)),
        's5_glu_w': nrm((DEPTH, S5_WIDTH, S5_WIDTH), S5_WIDTH ** -0.5),
        's5_glu_b': nrm((DEPTH, S5_WIDTH), 0.02),
        'w_branch': nrm((DEPTH, N_BRANCH, BRANCH_WIDTH, D), BRANCH_WIDTH ** -0.5),
        'w_gate': nrm((DEPTH, N_BRANCH, D, D), D ** -0.5),
        'b_gate': nrm((DEPTH, N_BRANCH, D), 0.02),
        'w_out': nrm((DEPTH, D, D), D ** -0.5),
        'router_w': nrm((DEPTH, D, N_EXPERTS), D ** -0.5),
        'ex_w_gu': nrm((DEPTH, N_EXPERTS, D, 2 * EXPERT_FF), D ** -0.5),
        'ex_w_down': nrm((DEPTH, N_EXPERTS, EXPERT_FF, D), EXPERT_FF ** -0.5),
    }


def reference(x, c, ctx, c_ctx, ada_w, ada_b, mix_pre_g, mix_post_g, ffn_pre_g, ffn_post_g, w_in, na_rpb,
              hy_conv_w, hy_conv_b, hy_w1, hy_b1, hy_w2, hy_b2, hy_w3, hy_freq, hy_bias,
              gla_w2, gla_b2, gla_norm_g, s5_a_re, s5_a_im, s5_log_dt, s5_b_re, s5_b_im, s5_c_re, s5_c_im,
              s5_d, s5_glu_w, s5_glu_b, w_branch, w_gate, b_gate, w_out, router_w, ex_w_gu, ex_w_down):
    xl, xc = x, ctx
    for l in range(DEPTH):
        need_ctx = l < DEPTH - 1
        ml = modulation(c, ada_w[l], ada_b[l])
        mc = modulation(c_ctx, ada_w[l], ada_b[l])

        hl = rms_norm(xl, mix_pre_g[l]) * (1.0 + ml[1]) + ml[0]
        hc = rms_norm(xc, mix_pre_g[l]) * (1.0 + mc[1]) + mc[0]
        zl = split_in(hl @ w_in[l])
        zc = split_in(hc @ w_in[l])
        na_l, na_c = na_mixer(zl[0], zc[0], na_rpb[l], need_ctx)
        hy_l = hyena_mixer(zl[1], hy_conv_w[l], hy_conv_b[l], hy_w1[l], hy_b1[l], hy_w2[l], hy_b2[l], hy_w3[l], hy_freq[l], hy_bias[l])
        gla_l, gla_c = gla_mixer(zl[2:7], zc[2:7], gla_w2[l], gla_b2[l], gla_norm_g[l], need_ctx)
        s5_l, s5_c = s5_mixer(zc[7], zl[7], s5_a_re[l], s5_a_im[l], s5_log_dt[l], s5_b_re[l], s5_b_im[l],
                              s5_c_re[l], s5_c_im[l], s5_d[l], s5_glu_w[l], s5_glu_b[l], need_ctx)
        yl = merge_branches(hl, (na_l, hy_l, gla_l, s5_l), w_branch[l], w_gate[l], b_gate[l], w_out[l])
        xl = xl + ml[2] * rms_norm(yl, mix_post_g[l])
        if need_ctx:
            hy_c = hyena_mixer(zc[1], hy_conv_w[l], hy_conv_b[l], hy_w1[l], hy_b1[l], hy_w2[l], hy_b2[l], hy_w3[l], hy_freq[l], hy_bias[l])
            yc = merge_branches(hc, (na_c, hy_c, gla_c, s5_c), w_branch[l], w_gate[l], b_gate[l], w_out[l])
            xc = xc + mc[2] * rms_norm(yc, mix_post_g[l])

        hl = rms_norm(xl, ffn_pre_g[l]) * (1.0 + ml[4]) + ml[3]
        xl = xl + ml[5] * rms_norm(moe_ec(hl, router_w[l], ex_w_gu[l], ex_w_down[l]), ffn_post_g[l])
        if need_ctx:
            hc = rms_norm(xc, ffn_pre_g[l]) * (1.0 + mc[4]) + mc[3]
            xc = xc + mc[5] * rms_norm(moe_ec(hc, router_w[l], ex_w_gu[l], ex_w_down[l]), ffn_post_g[l])
    return xl
```

```python
import functools
import math

import jax
import jax.numpy as jnp
import numpy as np
from jax import lax
from jax.experimental import pallas as pl
from jax.experimental.pallas import tpu as pltpu

D_MODEL = 4096
BATCH = 2
SEQ = 4096
DEPTH = 2
CTX_LEN = 256
GRID_W = 64

NA_HEADS = 8
NA_HEAD_DIM = 128
NA_WIDTH = NA_HEADS * NA_HEAD_DIM
NA_WIN_R = 8
NA_WIN_C = 16

HY_WIDTH = 1024
HY_ORDER = 2
HY_BANDS = 16
HY_POS_DIM = 1 + 2 * HY_BANDS
HY_FILTER_HIDDEN = 64
HY_DECAY_TARGET = 1e-2
HY_FAST_DECAY = 0.3
HY_SLOW_DECAY = 1.5

GLA_HEADS = 4
GLA_DK = 128
GLA_DV = 256
GLA_QK_WIDTH = GLA_HEADS * GLA_DK
GLA_V_WIDTH = GLA_HEADS * GLA_DV
GLA_GATE_RANK = 16
GLA_TAU = 16.0
GLA_CHUNK = 64

S5_WIDTH = 1024
S5_GROUP = 16
S5_GROUPS = S5_WIDTH // S5_GROUP
S5_STATE = 64

N_BRANCH = 4
BRANCH_WIDTH = 1024
N_EXPERTS = 16
EXPERT_FF = 1024
EC_CAPACITY = 2

ROPE_BASE = 10000.0
RMS_EPS = 1e-6
NEG_INF = -1e30
F32 = jnp.float32
BF16 = jnp.bfloat16

IN_SPLITS = (3 * NA_WIDTH, (HY_ORDER + 1) * HY_WIDTH, GLA_QK_WIDTH, GLA_QK_WIDTH, GLA_V_WIDTH, GLA_V_WIDTH,
             2 * GLA_GATE_RANK, S5_WIDTH)
IN_COLS = sum(IN_SPLITS)

V7X_LANES = 128
V7X_SUBLANES = 8
V7X_VMEM_LIMIT_BYTES = 56 * 1024 * 1024


def _mm_kernel(x_ref, w_ref, o_ref):
    o_ref[...] = jnp.dot(x_ref[...].astype(BF16), w_ref[...].astype(BF16), preferred_element_type=F32)


def _mm_tiles(M, K, N):
    tm = M if M <= 512 else 512
    tn = 512 if K > 1024 else 1024
    tn = min(tn, N)
    return tm, tn


def mm(x, w):
    M, K = x.shape
    N = w.shape[1]
    pad = (-M) % V7X_SUBLANES
    if pad:
        x = jnp.pad(x, ((0, pad), (0, 0)))
    Mp = M + pad
    tm, tn = _mm_tiles(Mp, K, N)
    assert Mp % tm == 0
    out = pl.pallas_call(
        _mm_kernel,
        grid=(pl.cdiv(N, tn), Mp // tm),
        in_specs=[pl.BlockSpec((tm, K), lambda j, i: (i, 0)),
                  pl.BlockSpec((K, tn), lambda j, i: (0, j))],
        out_specs=pl.BlockSpec((tm, tn), lambda j, i: (i, j)),
        out_shape=jax.ShapeDtypeStruct((Mp, N), F32),
        compiler_params=pltpu.CompilerParams(dimension_semantics=("arbitrary", "arbitrary"),
                                             vmem_limit_bytes=V7X_VMEM_LIMIT_BYTES),
        name="mm",
    )(x, w)
    return out[:M] if pad else out


def mm_nd(x, w):
    lead = x.shape[:-1]
    return mm(x.reshape(-1, x.shape[-1]), w).reshape(*lead, w.shape[1])


def _bmm_kernel(x_ref, w_ref, o_ref):
    o_ref[...] = jnp.dot(x_ref[...].astype(BF16), w_ref[...].astype(BF16), preferred_element_type=F32)


def expert_mm(x, w):
    B, E, C, K = x.shape
    N = w.shape[2]
    tn = 512 if K > 1024 else 1024
    return pl.pallas_call(
        _bmm_kernel,
        grid=(E, N // tn, B),
        in_specs=[pl.BlockSpec((None, None, C, K), lambda e, j, b: (b, e, 0, 0)),
                  pl.BlockSpec((None, K, tn), lambda e, j, b: (e, 0, j))],
        out_specs=pl.BlockSpec((None, None, C, tn), lambda e, j, b: (b, e, 0, j)),
        out_shape=jax.ShapeDtypeStruct((B, E, C, N), F32),
        compiler_params=pltpu.CompilerParams(dimension_semantics=("arbitrary", "arbitrary", "arbitrary"),
                                             vmem_limit_bytes=V7X_VMEM_LIMIT_BYTES),
        name="expert_mm",
    )(x, w)


def _ident(t):
    return t


def _rev(t):
    return t[:, ::-1]


def rms_norm(x, g):
    xf = x.astype(F32)
    y = xf * lax.rsqrt(jnp.mean(xf * xf, axis=-1, keepdims=True) + RMS_EPS)
    return (y * g.astype(F32)).astype(x.dtype)


def split_in(z):
    points = [int(p) for p in np.cumsum(np.array(IN_SPLITS))[:-1]]
    return jnp.split(z, points, axis=-1)


def rope_1d(x, pos):
    half = x.shape[-1] // 2
    inv = ROPE_BASE ** (-jnp.arange(half, dtype=F32) / half)
    ang = pos[:, None] * inv[None, :]
    cos = jnp.cos(ang)[None, :, None, :]
    sin = jnp.sin(ang)[None, :, None, :]
    xf = x.astype(F32)
    x1, x2 = xf[..., :half], xf[..., half:]
    return jnp.concatenate([x1 * cos - x2 * sin, x2 * cos + x1 * sin], axis=-1).astype(x.dtype)


def axial_rope(x):
    t = jnp.arange(x.shape[1])
    rows = (t // GRID_W).astype(F32)
    cols = (t % GRID_W).astype(F32)
    half = x.shape[-1] // 2
    return jnp.concatenate([rope_1d(x[..., :half], rows), rope_1d(x[..., half:], cols)], axis=-1)


def dense_attention(q, k, v):
    scale = q.shape[-1] ** -0.5
    s = jnp.einsum('bqhd,bkhd->bhqk', q, k, preferred_element_type=F32) * scale
    p = jax.nn.softmax(s, axis=-1).astype(v.dtype)
    o = jnp.einsum('bhqk,bkhd->bqhd', p, v)
    return o.reshape(q.shape[0], q.shape[1], -1)


def neighbourhood_attention(q, k, v, kc, vc, rpb):
    B, L, H, dh = q.shape
    rows = L // GRID_W
    wr = min(NA_WIN_R, rows)
    n_win = wr * GRID_W
    scale = dh ** -0.5
    r = jnp.arange(rows)
    row_idx = jnp.clip(r - wr // 2, 0, rows - wr)[:, None] + jnp.arange(wr)[None, :]
    j = jnp.arange(GRID_W)
    col_start = jnp.clip(j - NA_WIN_C // 2, 0, GRID_W - NA_WIN_C)
    col_ok = (j[None, :] >= col_start[:, None]) & (j[None, :] < col_start[:, None] + NA_WIN_C)
    row_off = row_idx - r[:, None] + (NA_WIN_R - 1)
    col_off = jnp.clip(j[None, :] - j[:, None] + (NA_WIN_C - 1), 0, 2 * NA_WIN_C - 2)
    bias = rpb.astype(F32)[:, row_off[:, :, None, None], col_off[None, None, :, :]]
    bias = bias.transpose(1, 0, 3, 2, 4)
    qg = q.reshape(B, rows, GRID_W, H, dh)
    kg = jnp.take(k.reshape(B, rows, GRID_W, H, dh), row_idx, axis=1)
    vg = jnp.take(v.reshape(B, rows, GRID_W, H, dh), row_idx, axis=1).reshape(B, rows, n_win, H, dh)
    s_win = jnp.einsum('brqhd,brwkhd->brhqwk', qg, kg, preferred_element_type=F32) * scale
    s_win = jnp.where(col_ok[:, None, :], s_win + bias[None], NEG_INF).reshape(B, rows, H, GRID_W, n_win)
    s_ctx = jnp.einsum('brqhd,bkhd->brhqk', qg, kc, preferred_element_type=F32) * scale
    p = jax.nn.softmax(jnp.concatenate([s_win, s_ctx], axis=-1), axis=-1).astype(v.dtype)
    o = jnp.einsum('brhqk,brkhd->brqhd', p[..., :n_win], vg) + jnp.einsum('brhqk,bkhd->brqhd', p[..., n_win:], vc)
    return o.reshape(B, L, H * dh)


def na_mixer(zl, zc, rpb, need_ctx):
    B, L, _ = zl.shape
    Lc = zc.shape[1]
    ql, kl, vl = [t.reshape(B, L, NA_HEADS, NA_HEAD_DIM) for t in jnp.split(zl, 3, axis=-1)]
    qc, kc, vc = [t.reshape(B, Lc, NA_HEADS, NA_HEAD_DIM) for t in jnp.split(zc, 3, axis=-1)]
    out_l = neighbourhood_attention(ql, kl, vl, kc, vc, rpb)
    out_c = dense_attention(qc, kc, vc) if need_ctx else None
    return out_l, out_c


def short_conv3(u, w, b):
    up = jnp.pad(u, ((0, 0), (1, 1), (0, 0)))
    return up[:, :-2] * w[0] + up[:, 1:-1] * w[1] + up[:, 2:] * w[2] + b


def hyena_filters(L, w1, b1, w2, b2, w3, freq):
    t = jnp.arange(L, dtype=F32)
    t01 = t / max(L - 1, 1)
    bands = jnp.linspace(1e-4, HY_BANDS - 1, HY_BANDS, dtype=F32)
    ang = (2.0 * math.pi / L) * t[:, None] * bands[None, :]
    z = jnp.concatenate([t01[:, None], jnp.cos(ang), -jnp.sin(ang)], axis=-1)
    f = freq.astype(F32)
    h = jnp.sin(f * (z @ w1.astype(F32) + b1.astype(F32)))
    h = jnp.sin(f * (h @ w2.astype(F32) + b2.astype(F32)))
    h = (h @ w3.astype(F32)).reshape(L, 2, HY_ORDER, HY_WIDTH)
    deltas = jnp.abs(jnp.linspace(math.log(HY_DECAY_TARGET) / HY_SLOW_DECAY, math.log(HY_DECAY_TARGET) / HY_FAST_DECAY,
                                  HY_WIDTH, dtype=F32))
    h = h * jnp.exp(-t01[:, None] * deltas[None, :])[:, None, None, :]
    kern = jnp.concatenate([h[:, 0], jnp.zeros((1, HY_ORDER, HY_WIDTH), F32), h[:0:-1, 1]], axis=0)
    return kern / jnp.sum(jnp.abs(kern), axis=0, keepdims=True)


def fft_long_conv(u, kern, bias):
    L = u.shape[1]
    n = 2 * L
    spec = jnp.fft.rfft(u, n=n, axis=1) * jnp.fft.rfft(kern, n=n, axis=0)[None]
    return jnp.fft.irfft(spec, n=n, axis=1)[:, :L] + u * bias


def hyena_mixer(z, conv_w, conv_b, w1, b1, w2, b2, w3, freq, bias):
    L = z.shape[1]
    zc = short_conv3(z.astype(F32), conv_w.astype(F32), conv_b.astype(F32))
    parts = jnp.split(zc, HY_ORDER + 1, axis=-1)
    kern = hyena_filters(L, w1, b1, w2, b2, w3, freq)
    y = parts[0]
    for o in range(HY_ORDER):
        y = parts[o + 1] * fft_long_conv(y, kern[:, o], bias[o].astype(F32))
    return y


def gla_chunked(q, k, v, log_a, s0):
    B, L, H, dk = q.shape
    dv = v.shape[-1]
    n = L // GLA_CHUNK

    def blocks(t):
        return t.reshape(B, n, GLA_CHUNK, H, t.shape[-1]).transpose(1, 0, 3, 2, 4)

    qb, kb, vb, ab = blocks(q), blocks(k), blocks(v), blocks(log_a)
    bcum = jnp.cumsum(ab, axis=3)
    blast = bcum[:, :, :, -1:, :]
    q_in = qb * jnp.exp(bcum)
    k_in = kb * jnp.exp(-bcum)
    k_st = kb * jnp.exp(blast - bcum)
    lower = jnp.tril(jnp.ones((GLA_CHUNK, GLA_CHUNK), bool))
    att = jnp.where(lower, jnp.einsum('nbhtd,nbhsd->nbhts', q_in, k_in), 0.0)
    o_intra = jnp.einsum('nbhts,nbhsv->nbhtv', att, vb)
    kv = jnp.einsum('nbhsd,nbhsv->nbhdv', k_st, vb)
    decay = jnp.exp(blast[:, :, :, 0, :])

    def step(state, inp):
        d, kv_n = inp
        return d[..., None] * state + kv_n, state

    s_fin, s_prev = lax.scan(step, s0, (decay, kv))
    o = o_intra + jnp.einsum('nbhtd,nbhdv->nbhtv', q_in, s_prev)
    return o.transpose(1, 0, 3, 2, 4).reshape(B, L, H, dv), s_fin


def gla_prep(zq, zk, zv, za, w2, b2, rotary):
    B, L, _ = zq.shape
    q = zq.reshape(B, L, GLA_HEADS, GLA_DK)
    k = zk.reshape(B, L, GLA_HEADS, GLA_DK)
    if rotary:
        q, k = axial_rope(q), axial_rope(k)
    q = q.astype(F32) * GLA_DK ** -0.5
    k = k.astype(F32)
    v = zv.astype(F32).reshape(B, L, GLA_HEADS, GLA_DV)
    a = za.astype(F32).reshape(B, L, 2, GLA_GATE_RANK)
    log_a = jax.nn.log_sigmoid(jnp.einsum('bldr,drk->bldk', a, w2.astype(F32)) + b2.astype(F32)) / GLA_TAU
    return q, k, v, log_a.reshape(B, L, 2, GLA_HEADS, GLA_DK)


def gla_out(o, g, norm_g):
    B, L = o.shape[:2]
    o = o * lax.rsqrt(jnp.mean(o * o, axis=-1, keepdims=True) + RMS_EPS) * norm_g.astype(F32)
    return o.reshape(B, L, GLA_V_WIDTH) * jax.nn.silu(g.astype(F32))


def gla_mixer(lat, ctxp, w2, b2, norm_g, need_ctx):
    zq_l, zk_l, zv_l, zg_l, za_l = lat
    zq_c, zk_c, zv_c, zg_c, za_c = ctxp
    ql, kl, vl, al = gla_prep(zq_l, zk_l, zv_l, za_l, w2, b2, True)
    qc, kc, vc, ac = gla_prep(zq_c, zk_c, zv_c, za_c, w2, b2, False)
    s0 = jnp.zeros((qc.shape[0], GLA_HEADS, GLA_DK, GLA_DV), F32)
    oc_f, sc_f = gla_chunked(qc, kc, vc, ac[:, :, 0], s0)
    oc_b, sc_b = gla_chunked(_rev(qc), _rev(kc), _rev(vc), _rev(ac[:, :, 1]), s0)
    ol_f, _ = gla_chunked(ql, kl, vl, al[:, :, 0], sc_f)
    ol_b, _ = gla_chunked(_rev(ql), _rev(kl), _rev(vl), _rev(al[:, :, 1]), sc_b)
    out_l = gla_out(ol_f + _rev(ol_b), zg_l, norm_g)
    out_c = gla_out(oc_f + _rev(oc_b), zg_c, norm_g) if need_ctx else None
    return out_l, out_c


def s5_discretise(a_re, a_im, log_dt, b_re, b_im):
    A = lax.complex(a_re.astype(F32), a_im.astype(F32))
    dt = jnp.exp(log_dt.astype(F32))[:, None]
    a_bar = jnp.exp(A * dt)
    b_bar = ((a_bar - 1.0) / A)[..., None] * lax.complex(b_re.astype(F32), b_im.astype(F32))
    return a_bar, b_bar


def s5_scan(a_bar, b_bar, u, x0):
    bu = jnp.einsum('gpi,blgi->blgp', b_bar, u.astype(jnp.complex64))
    bu = bu.at[:, 0].add(a_bar * x0)
    a = jnp.broadcast_to(a_bar, bu.shape)

    def combine(left, right):
        return left[0] * right[0], right[0] * left[1] + right[1]

    _, xs = lax.associative_scan(combine, (a, bu), axis=1)
    return xs


def s5_glu(y, w, b):
    y = jax.nn.gelu(y)
    return y * jax.nn.sigmoid(mm_nd(y, w) + b.astype(F32))


def s5_mixer(uc, ul, a_re, a_im, log_dt, b_re, b_im, c_re, c_im, d_skip, glu_w, glu_b, need_ctx):
    B, L, _ = ul.shape
    Lc = uc.shape[1]
    ulg = ul.astype(F32).reshape(B, L, S5_GROUPS, S5_GROUP)
    ucg = uc.astype(F32).reshape(B, Lc, S5_GROUPS, S5_GROUP)
    d = d_skip.astype(F32).reshape(S5_GROUPS, S5_GROUP)
    yl = d * ulg
    yc = d * ucg if need_ctx else None
    x0 = jnp.zeros((B, S5_GROUPS, S5_STATE), jnp.complex64)
    for direction in range(2):
        a_bar, b_bar = s5_discretise(a_re[direction], a_im[direction], log_dt[direction], b_re[direction],
                                     b_im[direction])
        c_mat = lax.complex(c_re[direction].astype(F32), c_im[direction].astype(F32))
        orient = _ident if direction == 0 else _rev
        xs_c = s5_scan(a_bar, b_bar, orient(ucg), x0)
        xs_l = s5_scan(a_bar, b_bar, orient(ulg), xs_c[:, -1])
        yl = yl + orient(jnp.einsum('gip,blgp->blgi', c_mat, xs_l).real)
        if need_ctx:
            yc = yc + orient(jnp.einsum('gip,blgp->blgi', c_mat, xs_c).real)
    out_l = s5_glu(yl.reshape(B, L, S5_WIDTH), glu_w, glu_b)
    out_c = s5_glu(yc.reshape(B, Lc, S5_WIDTH), glu_w, glu_b) if need_ctx else None
    return out_l, out_c


def merge_branches(h, branches, w_branch, w_gate, b_gate, w_out):
    merged = jnp.zeros(h.shape, h.dtype)
    for i, br in enumerate(branches):
        gate = jax.nn.sigmoid(mm_nd(h, w_gate[i]) + b_gate[i])
        merged = merged + gate * mm_nd(br.astype(h.dtype), w_branch[i])
    return mm_nd(merged, w_out)


def moe_ec(h, router_w, w_gu, w_down):
    B, N, D = h.shape
    cap = max(1, EC_CAPACITY * N // N_EXPERTS)
    logits = jnp.einsum('bnd,de->bne', h, router_w, precision=lax.Precision.HIGHEST)
    aff = jax.nn.softmax(logits.astype(F32), axis=-1)
    gate, idx = lax.top_k(jnp.swapaxes(aff, 1, 2), cap)
    xs = jax.vmap(lambda hb, ib: hb[ib])(h, idx)
    gu = expert_mm(xs, w_gu)
    g_, u_ = jnp.split(gu, 2, axis=-1)
    y = expert_mm(jax.nn.silu(g_) * u_, w_down) * gate[..., None].astype(h.dtype)
    return jax.vmap(lambda ib, yb: jnp.zeros((N, D), yb.dtype).at[ib.reshape(-1)].add(yb.reshape(-1, D)))(idx, y)


def kernel(x, c, ctx, c_ctx, ada_w, ada_b, mix_pre_g, mix_post_g, ffn_pre_g, ffn_post_g, w_in, na_rpb,
           hy_conv_w, hy_conv_b, hy_w1, hy_b1, hy_w2, hy_b2, hy_w3, hy_freq, hy_bias,
           gla_w2, gla_b2, gla_norm_g, s5_a_re, s5_a_im, s5_log_dt, s5_b_re, s5_b_im, s5_c_re, s5_c_im,
           s5_d, s5_glu_w, s5_glu_b, w_branch, w_gate, b_gate, w_out, router_w, ex_w_gu, ex_w_down):
    xl, xc = x, ctx
    B = x.shape[0]
    for l in range(DEPTH):
        need_ctx = l < DEPTH - 1
        cvecs = jnp.concatenate([c, c_ctx[None, :]], axis=0)
        mod = mm(jax.nn.silu(cvecs), ada_w[l]) + ada_b[l]
        ml = [p[:, None, :] for p in jnp.split(mod[:B], 6, axis=-1)]
        mc = jnp.split(mod[B], 6, axis=-1)

        hl = rms_norm(xl, mix_pre_g[l]) * (1.0 + ml[1]) + ml[0]
        hc = rms_norm(xc, mix_pre_g[l]) * (1.0 + mc[1]) + mc[0]
        zl = split_in(mm_nd(hl, w_in[l]))
        zc = split_in(mm_nd(hc, w_in[l]))
        na_l, na_c = na_mixer(zl[0], zc[0], na_rpb[l], need_ctx)
        hy_args = (hy_conv_w[l], hy_conv_b[l], hy_w1[l], hy_b1[l], hy_w2[l], hy_b2[l], hy_w3[l], hy_freq[l], hy_bias[l])
        hy_l = hyena_mixer(zl[1], *hy_args)
        gla_l, gla_c = gla_mixer(zl[2:7], zc[2:7], gla_w2[l], gla_b2[l], gla_norm_g[l], need_ctx)
        s5_l, s5_c = s5_mixer(zc[7], zl[7], s5_a_re[l], s5_a_im[l], s5_log_dt[l], s5_b_re[l], s5_b_im[l],
                              s5_c_re[l], s5_c_im[l], s5_d[l], s5_glu_w[l], s5_glu_b[l], need_ctx)
        yl = merge_branches(hl, (na_l, hy_l, gla_l, s5_l), w_branch[l], w_gate[l], b_gate[l], w_out[l])
        xl = xl + ml[2] * rms_norm(yl, mix_post_g[l])
        if need_ctx:
            hy_c = hyena_mixer(zc[1], *hy_args)
            yc = merge_branches(hc, (na_c, hy_c, gla_c, s5_c), w_branch[l], w_gate[l], b_gate[l], w_out[l])
            xc = xc + mc[2] * rms_norm(yc, mix_post_g[l])

        hl = rms_norm(xl, ffn_pre_g[l]) * (1.0 + ml[4]) + ml[3]
        xl = xl + ml[5] * rms_norm(moe_ec(hl, router_w[l], ex_w_gu[l], ex_w_down[l]), ffn_post_g[l])
        if need_ctx:
            hc = rms_norm(xc, ffn_pre_g[l]) * (1.0 + mc[4]) + mc[3]
            xc = xc + mc[5] * rms_norm(moe_ec(hc, router_w[l], ex_w_gu[l], ex_w_down[l]), ffn_post_g[l])
    return xl
```

```python
import functools
import math

import jax
import jax.numpy as jnp
import numpy as np
from jax import lax
from jax.experimental import pallas as pl
from jax.experimental.pallas import tpu as pltpu

D_MODEL = 4096
BATCH = 2
SEQ = 4096
DEPTH = 2
CTX_LEN = 256
GRID_W = 64

NA_HEADS = 8
NA_HEAD_DIM = 128
NA_WIDTH = NA_HEADS * NA_HEAD_DIM
NA_WIN_R = 8
NA_WIN_C = 16

HY_WIDTH = 1024
HY_ORDER = 2
HY_BANDS = 16
HY_POS_DIM = 1 + 2 * HY_BANDS
HY_FILTER_HIDDEN = 64
HY_DECAY_TARGET = 1e-2
HY_FAST_DECAY = 0.3
HY_SLOW_DECAY = 1.5

GLA_HEADS = 4
GLA_DK = 128
GLA_DV = 256
GLA_QK_WIDTH = GLA_HEADS * GLA_DK
GLA_V_WIDTH = GLA_HEADS * GLA_DV
GLA_GATE_RANK = 16
GLA_TAU = 16.0
GLA_CHUNK = 64

S5_WIDTH = 1024
S5_GROUP = 16
S5_GROUPS = S5_WIDTH // S5_GROUP
S5_STATE = 64

N_BRANCH = 4
BRANCH_WIDTH = 1024
N_EXPERTS = 16
EXPERT_FF = 1024
EC_CAPACITY = 2

ROPE_BASE = 10000.0
RMS_EPS = 1e-6
NEG_INF = -1e30
F32 = jnp.float32
BF16 = jnp.bfloat16

IN_SPLITS = (3 * NA_WIDTH, (HY_ORDER + 1) * HY_WIDTH, GLA_QK_WIDTH, GLA_QK_WIDTH, GLA_V_WIDTH, GLA_V_WIDTH,
             2 * GLA_GATE_RANK, S5_WIDTH)
IN_COLS = sum(IN_SPLITS)
IN_OFF = tuple(int(v) for v in np.concatenate([[0], np.cumsum(IN_SPLITS)]))
IN_MAIN = IN_OFF[6]

V7X_LANES = 128
V7X_SUBLANES = 8
V7X_VMEM_LIMIT_BYTES = 56 * 1024 * 1024


def _mm_kernel(x_ref, w_ref, o_ref):
    o_ref[...] = jnp.dot(x_ref[...].astype(BF16), w_ref[...].astype(BF16), preferred_element_type=F32)


def _mm_tiles(M, K, N):
    tm = M if M <= 512 else 512
    tn = 512 if K > 1024 else 1024
    tn = min(tn, N)
    return tm, tn


def mm(x, w, ncols=None):
    M, K = x.shape
    N = w.shape[1] if ncols is None else ncols
    pad = (-M) % V7X_SUBLANES
    if pad:
        x = jnp.pad(x, ((0, pad), (0, 0)))
    Mp = M + pad
    tm, tn = _mm_tiles(Mp, K, N)
    assert Mp % tm == 0 and (N == w.shape[1] or N % tn == 0)
    out = pl.pallas_call(
        _mm_kernel,
        grid=(pl.cdiv(N, tn), Mp // tm),
        in_specs=[pl.BlockSpec((tm, K), lambda j, i: (i, 0)),
                  pl.BlockSpec((K, tn), lambda j, i: (0, j))],
        out_specs=pl.BlockSpec((tm, tn), lambda j, i: (i, j)),
        out_shape=jax.ShapeDtypeStruct((Mp, N), F32),
        compiler_params=pltpu.CompilerParams(dimension_semantics=("arbitrary", "arbitrary"),
                                             vmem_limit_bytes=V7X_VMEM_LIMIT_BYTES),
        name="mm",
    )(x, w)
    return out[:M] if pad else out


def mm_nd(x, w):
    lead = x.shape[:-1]
    return mm(x.reshape(-1, x.shape[-1]), w).reshape(*lead, w.shape[1])


def _bmm_kernel(x_ref, w_ref, o_ref):
    o_ref[...] = jnp.dot(x_ref[...].astype(BF16), w_ref[...].astype(BF16), preferred_element_type=F32)


def expert_mm(x, w):
    B, E, C, K = x.shape
    N = w.shape[2]
    tn = 512 if K > 1024 else 1024
    return pl.pallas_call(
        _bmm_kernel,
        grid=(E, N // tn, B),
        in_specs=[pl.BlockSpec((None, None, C, K), lambda e, j, b: (b, e, 0, 0)),
                  pl.BlockSpec((None, K, tn), lambda e, j, b: (e, 0, j))],
        out_specs=pl.BlockSpec((None, None, C, tn), lambda e, j, b: (b, e, 0, j)),
        out_shape=jax.ShapeDtypeStruct((B, E, C, N), F32),
        compiler_params=pltpu.CompilerParams(dimension_semantics=("arbitrary", "arbitrary", "arbitrary"),
                                             vmem_limit_bytes=V7X_VMEM_LIMIT_BYTES),
        name="expert_mm",
    )(x, w)


def _ident(t):
    return t


def _rev(t):
    return t[:, ::-1]


def rms_norm(x, g):
    xf = x.astype(F32)
    y = xf * lax.rsqrt(jnp.mean(xf * xf, axis=-1, keepdims=True) + RMS_EPS)
    return (y * g.astype(F32)).astype(x.dtype)


def rope_1d(x, pos):
    half = x.shape[-1] // 2
    inv = ROPE_BASE ** (-jnp.arange(half, dtype=F32) / half)
    ang = pos[:, None] * inv[None, :]
    cos = jnp.cos(ang)[None, :, None, :]
    sin = jnp.sin(ang)[None, :, None, :]
    xf = x.astype(F32)
    x1, x2 = xf[..., :half], xf[..., half:]
    return jnp.concatenate([x1 * cos - x2 * sin, x2 * cos + x1 * sin], axis=-1).astype(x.dtype)


def axial_rope(x):
    t = jnp.arange(x.shape[1])
    rows = (t // GRID_W).astype(F32)
    cols = (t % GRID_W).astype(F32)
    half = x.shape[-1] // 2
    return jnp.concatenate([rope_1d(x[..., :half], rows), rope_1d(x[..., half:], cols)], axis=-1)


def dense_attention(q, k, v):
    scale = q.shape[-1] ** -0.5
    s = jnp.einsum('bqhd,bkhd->bhqk', q, k, preferred_element_type=F32) * scale
    p = jax.nn.softmax(s, axis=-1).astype(v.dtype)
    o = jnp.einsum('bhqk,bkhd->bqhd', p, v)
    return o.reshape(q.shape[0], q.shape[1], -1)


NA_TILE_ROWS = 8


def na_bias_table(rpb, rows):
    nt = rows // NA_TILE_ROWS
    cases = jnp.array([0, min(1, nt - 1), nt - 1])
    rq = jnp.arange(NA_TILE_ROWS)
    rk = jnp.arange(2 * NA_TILE_ROWS)
    c = jnp.arange(GRID_W)
    r = cases[:, None] * NA_TILE_ROWS + rq[None, :]
    ks = jnp.clip(cases * NA_TILE_ROWS - NA_WIN_R // 2, 0, rows - 2 * NA_TILE_ROWS)
    krow = ks[:, None] + rk[None, :]
    ws = jnp.clip(r - NA_WIN_R // 2, 0, rows - NA_WIN_R)
    row_ok = (krow[:, None, :] >= ws[:, :, None]) & (krow[:, None, :] < ws[:, :, None] + NA_WIN_R)
    row_off = jnp.clip(krow[:, None, :] - r[:, :, None] + (NA_WIN_R - 1), 0, 2 * NA_WIN_R - 2)
    col_start = jnp.clip(c - NA_WIN_C // 2, 0, GRID_W - NA_WIN_C)
    col_ok = (c[None, :] >= col_start[:, None]) & (c[None, :] < col_start[:, None] + NA_WIN_C)
    col_off = jnp.clip(c[None, :] - c[:, None] + (NA_WIN_C - 1), 0, 2 * NA_WIN_C - 2)
    b = rpb.astype(F32)[:, row_off[:, :, None, :, None], col_off[None, None, :, None, :]]
    ok = row_ok[:, :, None, :, None] & col_ok[None, None, :, None, :]
    b = jnp.where(ok[None], b, NEG_INF).transpose(1, 0, 2, 3, 4, 5)
    return b.reshape(3, rpb.shape[0], NA_TILE_ROWS * GRID_W, 2 * NA_TILE_ROWS * GRID_W).astype(BF16)


def _na_kernel(q_ref, k_ref, v_ref, kc_ref, vc_ref, b_ref, o_ref, *, scale):
    q = q_ref[...].astype(BF16)
    nk = k_ref.shape[0]
    k = jnp.concatenate([k_ref[...].astype(BF16), kc_ref[...].astype(BF16)], axis=0)
    v = jnp.concatenate([v_ref[...].astype(BF16), vc_ref[...].astype(BF16)], axis=0)
    s = lax.dot_general(q, k, (((1,), (1,)), ((), ())), preferred_element_type=F32) * scale
    bias = jnp.concatenate([b_ref[...].astype(F32), jnp.zeros((s.shape[0], s.shape[1] - nk), F32)], axis=1)
    s = s + bias
    m = jnp.max(s, axis=-1, keepdims=True)
    p = jnp.exp(s - m)
    den = jnp.sum(p, axis=-1, keepdims=True)
    o = jnp.dot(p.astype(BF16), v, preferred_element_type=F32)
    o_ref[...] = o / den


def na_latent(zl, zc, bias_tab, B, L, Lc):
    rows = L // GRID_W
    nt = rows // NA_TILE_ROWS
    tq = NA_TILE_ROWS * GRID_W
    tk = 2 * tq
    H, dh = NA_HEADS, NA_HEAD_DIM

    def case(j):
        return jnp.where(j == 0, 0, jnp.where(j == nt - 1, 2, 1))

    def kstart(b, j):
        ks = jnp.clip(j * NA_TILE_ROWS - NA_WIN_R // 2, 0, rows - 2 * NA_TILE_ROWS)
        return pl.multiple_of(b * L + ks * GRID_W, (NA_WIN_R // 2) * GRID_W)

    def kv_spec(col0):
        return pl.BlockSpec((pl.Element(tk), pl.Element(dh)),
                            lambda b, j, h: (kstart(b, j), pl.multiple_of((col0 + h) * dh, dh)))

    return pl.pallas_call(
        functools.partial(_na_kernel, scale=dh ** -0.5),
        grid=(B, nt, H),
        in_specs=[
            pl.BlockSpec((tq, dh), lambda b, j, h: (b * nt + j, h)),
            kv_spec(H),
            kv_spec(2 * H),
            pl.BlockSpec((Lc, dh), lambda b, j, h: (b, H + h)),
            pl.BlockSpec((Lc, dh), lambda b, j, h: (b, 2 * H + h)),
            pl.BlockSpec((None, None, tq, tk), lambda b, j, h: (case(j), h, 0, 0)),
        ],
        out_specs=pl.BlockSpec((tq, dh), lambda b, j, h: (b * nt + j, h)),
        out_shape=jax.ShapeDtypeStruct((B * L, H * dh), F32),
        compiler_params=pltpu.CompilerParams(dimension_semantics=("arbitrary",) * 3),
        name="na_latent",
    )(zl, zl, zl, zc, zc, bias_tab)


def na_context(zc, B, Lc):
    qc, kc, vc = [t.reshape(B, Lc, NA_HEADS, NA_HEAD_DIM) for t in jnp.split(zc[:, :3 * NA_WIDTH], 3, axis=-1)]
    return dense_attention(qc, kc, vc).reshape(B * Lc, NA_WIDTH)


def short_conv3(u, w, b):
    up = jnp.pad(u, ((0, 0), (1, 1), (0, 0)))
    return up[:, :-2] * w[0] + up[:, 1:-1] * w[1] + up[:, 2:] * w[2] + b


def hyena_filters(L, w1, b1, w2, b2, w3, freq):
    t = jnp.arange(L, dtype=F32)
    t01 = t / max(L - 1, 1)
    bands = jnp.linspace(1e-4, HY_BANDS - 1, HY_BANDS, dtype=F32)
    ang = (2.0 * math.pi / L) * t[:, None] * bands[None, :]
    z = jnp.concatenate([t01[:, None], jnp.cos(ang), -jnp.sin(ang)], axis=-1)
    f = freq.astype(F32)
    h = jnp.sin(f * (z @ w1.astype(F32) + b1.astype(F32)))
    h = jnp.sin(f * (h @ w2.astype(F32) + b2.astype(F32)))
    h = (h @ w3.astype(F32)).reshape(L, 2, HY_ORDER, HY_WIDTH)
    deltas = jnp.abs(jnp.linspace(math.log(HY_DECAY_TARGET) / HY_SLOW_DECAY, math.log(HY_DECAY_TARGET) / HY_FAST_DECAY,
                                  HY_WIDTH, dtype=F32))
    h = h * jnp.exp(-t01[:, None] * deltas[None, :])[:, None, None, :]
    kern = jnp.concatenate([h[:, 0], jnp.zeros((1, HY_ORDER, HY_WIDTH), F32), h[:0:-1, 1]], axis=0)
    return kern / jnp.sum(jnp.abs(kern), axis=0, keepdims=True)


def fft_long_conv(u, kern, bias):
    L = u.shape[1]
    n = 2 * L
    spec = jnp.fft.rfft(u, n=n, axis=1) * jnp.fft.rfft(kern, n=n, axis=0)[None]
    return jnp.fft.irfft(spec, n=n, axis=1)[:, :L] + u * bias


def hyena_mixer(z, conv_w, conv_b, w1, b1, w2, b2, w3, freq, bias):
    L = z.shape[1]
    zc = short_conv3(z.astype(F32), conv_w.astype(F32), conv_b.astype(F32))
    parts = jnp.split(zc, HY_ORDER + 1, axis=-1)
    kern = hyena_filters(L, w1, b1, w2, b2, w3, freq)
    y = parts[0]
    for o in range(HY_ORDER):
        y = parts[o + 1] * fft_long_conv(y, kern[:, o], bias[o].astype(F32))
    return y


def gla_chunked(q, k, v, log_a, s0):
    B, L, H, dk = q.shape
    dv = v.shape[-1]
    n = L // GLA_CHUNK

    def blocks(t):
        return t.reshape(B, n, GLA_CHUNK, H, t.shape[-1]).transpose(1, 0, 3, 2, 4)

    qb, kb, vb, ab = blocks(q), blocks(k), blocks(v), blocks(log_a)
    bcum = jnp.cumsum(ab, axis=3)
    blast = bcum[:, :, :, -1:, :]
    q_in = qb * jnp.exp(bcum)
    k_in = kb * jnp.exp(-bcum)
    k_st = kb * jnp.exp(blast - bcum)
    lower = jnp.tril(jnp.ones((GLA_CHUNK, GLA_CHUNK), bool))
    att = jnp.where(lower, jnp.einsum('nbhtd,nbhsd->nbhts', q_in, k_in), 0.0)
    o_intra = jnp.einsum('nbhts,nbhsv->nbhtv', att, vb)
    kv = jnp.einsum('nbhsd,nbhsv->nbhdv', k_st, vb)
    decay = jnp.exp(blast[:, :, :, 0, :])

    def step(state, inp):
        d, kv_n = inp
        return d[..., None] * state + kv_n, state

    s_fin, s_prev = lax.scan(step, s0, (decay, kv))
    o = o_intra + jnp.einsum('nbhtd,nbhdv->nbhtv', q_in, s_prev)
    return o.transpose(1, 0, 3, 2, 4).reshape(B, L, H, dv), s_fin


def gla_prep(zq, zk, zv, za, w2, b2, rotary):
    B, L, _ = zq.shape
    q = zq.reshape(B, L, GLA_HEADS, GLA_DK)
    k = zk.reshape(B, L, GLA_HEADS, GLA_DK)
    if rotary:
        q, k = axial_rope(q), axial_rope(k)
    q = q.astype(F32) * GLA_DK ** -0.5
    k = k.astype(F32)
    v = zv.astype(F32).reshape(B, L, GLA_HEADS, GLA_DV)
    a = za.astype(F32).reshape(B, L, 2, GLA_GATE_RANK)
    log_a = jax.nn.log_sigmoid(jnp.einsum('bldr,drk->bldk', a, w2.astype(F32)) + b2.astype(F32)) / GLA_TAU
    return q, k, v, log_a.reshape(B, L, 2, GLA_HEADS, GLA_DK)


def gla_out(o, g, norm_g):
    B, L = o.shape[:2]
    o = o * lax.rsqrt(jnp.mean(o * o, axis=-1, keepdims=True) + RMS_EPS) * norm_g.astype(F32)
    return o.reshape(B, L, GLA_V_WIDTH) * jax.nn.silu(g.astype(F32))


def gla_mixer(lat, ctxp, w2, b2, norm_g, need_ctx):
    zq_l, zk_l, zv_l, zg_l, za_l = lat
    zq_c, zk_c, zv_c, zg_c, za_c = ctxp
    ql, kl, vl, al = gla_prep(zq_l, zk_l, zv_l, za_l, w2, b2, True)
    qc, kc, vc, ac = gla_prep(zq_c, zk_c, zv_c, za_c, w2, b2, False)
    s0 = jnp.zeros((qc.shape[0], GLA_HEADS, GLA_DK, GLA_DV), F32)
    oc_f, sc_f = gla_chunked(qc, kc, vc, ac[:, :, 0], s0)
    oc_b, sc_b = gla_chunked(_rev(qc), _rev(kc), _rev(vc), _rev(ac[:, :, 1]), s0)
    ol_f, _ = gla_chunked(ql, kl, vl, al[:, :, 0], sc_f)
    ol_b, _ = gla_chunked(_rev(ql), _rev(kl), _rev(vl), _rev(al[:, :, 1]), sc_b)
    out_l = gla_out(ol_f + _rev(ol_b), zg_l, norm_g)
    out_c = gla_out(oc_f + _rev(oc_b), zg_c, norm_g) if need_ctx else None
    return out_l, out_c


def s5_discretise(a_re, a_im, log_dt, b_re, b_im):
    A = lax.complex(a_re.astype(F32), a_im.astype(F32))
    dt = jnp.exp(log_dt.astype(F32))[:, None]
    a_bar = jnp.exp(A * dt)
    b_bar = ((a_bar - 1.0) / A)[..., None] * lax.complex(b_re.astype(F32), b_im.astype(F32))
    return a_bar, b_bar


S5_T = 16
S5_GQ = V7X_LANES // S5_GROUP
S5_NQ = S5_GROUPS // S5_GQ
S5_NS = 2 * S5_GQ * S5_STATE


def s5_operators(a_re, a_im, log_dt, b_re, b_im, c_re, c_im):
    T, GQ, NQ, P, I = S5_T, S5_GQ, S5_NQ, S5_STATE, S5_GROUP
    eye = jnp.eye(GQ, dtype=F32)
    w_parts, v_parts, at_parts = [], [], []
    ktot = 0.0
    for d in range(2):
        a_bar, b_bar = s5_discretise(a_re[d], a_im[d], log_dt[d], b_re[d], b_im[d])
        c_mat = lax.complex(c_re[d].astype(F32), c_im[d].astype(F32))
        e = jnp.arange(T + 1, dtype=F32)
        apow = a_bar[None] ** e[:, None, None].astype(jnp.complex64)
        ex = (T - 1 - jnp.arange(T)) if d == 0 else jnp.arange(T)
        w = apow[ex][:, :, :, None] * b_bar[None]
        w = jnp.stack([w.real, w.imag], axis=0).reshape(2, T, NQ, GQ, P, I)
        w = jnp.einsum('rsqgpj,gh->qsgjrhp', w, eye)
        w_parts.append(w.reshape(NQ, T * GQ * I, S5_NS))
        k = jnp.einsum('gip,tgp,gpj->tgij', c_mat, apow[:T], b_bar).real
        t_idx = jnp.arange(T)[:, None]
        s_idx = jnp.arange(T)[None, :]
        lag = (t_idx - s_idx) if d == 0 else (s_idx - t_idx)
        ktot = ktot + jnp.where((lag >= 0)[:, :, None, None, None], k[jnp.clip(lag, 0, T - 1)], 0.0)
        ey = (jnp.arange(T) + 1) if d == 0 else (T - jnp.arange(T))
        v = c_mat[None] * apow[ey][:, :, None, :]
        v = jnp.stack([v.real, -v.imag], axis=0).reshape(2, T, NQ, GQ, I, P)
        v = jnp.einsum('rtqgip,gh->qrgpthi', v, eye)
        v_parts.append(v.reshape(NQ, S5_NS, T * GQ * I))
        at = apow[T].reshape(NQ, GQ * P)
        at_parts.append(jnp.concatenate([at.real, at.imag], axis=-1)[:, None, :])
    m = ktot.reshape(T, T, NQ, GQ, I, I)
    m = jnp.einsum('tsqgij,gh->qsgjthi', m, eye).reshape(NQ, T * GQ * I, T * GQ * I)
    rhs = jnp.concatenate([w_parts[0], w_parts[1], m], axis=-1).astype(BF16)
    return rhs, jnp.stack(v_parts).astype(BF16), jnp.stack(at_parts)


def _s5_in_kernel(u_ref, rhs_ref, s_ref, y_ref):
    r = jnp.dot(u_ref[...], rhs_ref[...], preferred_element_type=F32)
    s_ref[0] = r[:, :S5_NS]
    s_ref[1] = r[:, S5_NS:2 * S5_NS]
    y_ref[...] = r[:, 2 * S5_NS:]


def _s5_scan_kernel(s_ref, at_ref, x_ref, *, n_ctx, n_lat, batch):
    d = pl.program_id(0)
    half = S5_NS // 2
    a_r = at_ref[:, :half]
    a_i = at_ref[:, half:]

    def run(b, base, n, carry):
        def body(i, st):
            xr, xi = st
            c = i + d * (n - 1 - 2 * i)
            row = base + b * n + c
            x_ref[pl.ds(row, 1), :] = jnp.concatenate([xr, xi], axis=-1)
            s = s_ref[pl.ds(row, 1), :]
            return a_r * xr - a_i * xi + s[:, :half], a_r * xi + a_i * xr + s[:, half:]
        return lax.fori_loop(0, n, body, carry)

    for b in range(batch):
        zero = jnp.zeros((1, half), F32)
        st = run(b, 0, n_ctx, (zero, zero))
        run(b, batch * n_ctx, n_lat, st)


def _s5_out_kernel(x_ref, v_ref, yin_ref, u_ref, dsk_ref, y_ref):
    y = yin_ref[...] + dsk_ref[...] * u_ref[...]
    y = y + jnp.dot(x_ref[0].astype(BF16), v_ref[0], preferred_element_type=F32)
    y = y + jnp.dot(x_ref[1].astype(BF16), v_ref[1], preferred_element_type=F32)
    y_ref[...] = y


def s5_core(uc, ul, ops, d_skip, B, L, Lc):
    rhs, vmat, a_t = ops
    T, NQ, NS = S5_T, S5_NQ, S5_NS
    n_ctx, n_lat = Lc // T, L // T
    R = B * (n_ctx + n_lat)
    tw = T * V7X_LANES
    u = jnp.concatenate([uc.reshape(B * n_ctx, T, NQ, V7X_LANES), ul.reshape(B * n_lat, T, NQ, V7X_LANES)], axis=0)
    u = u.transpose(2, 0, 1, 3).reshape(NQ, R, tw)
    params = pltpu.CompilerParams(dimension_semantics=("arbitrary",), vmem_limit_bytes=V7X_VMEM_LIMIT_BYTES)
    s, y_in = pl.pallas_call(
        _s5_in_kernel,
        grid=(NQ,),
        in_specs=[pl.BlockSpec((None, R, tw), lambda q: (q, 0, 0)),
                  pl.BlockSpec((None, tw, 2 * NS + tw), lambda q: (q, 0, 0))],
        out_specs=[pl.BlockSpec((2, None, R, NS), lambda q: (0, q, 0, 0)),
                   pl.BlockSpec((None, R, tw), lambda q: (q, 0, 0))],
        out_shape=[jax.ShapeDtypeStruct((2, NQ, R, NS), F32), jax.ShapeDtypeStruct((NQ, R, tw), F32)],
        compiler_params=params,
        name="s5_in",
    )(u.astype(BF16), rhs)
    x = pl.pallas_call(
        functools.partial(_s5_scan_kernel, n_ctx=n_ctx, n_lat=n_lat, batch=B),
        grid=(2, NQ),
        in_specs=[pl.BlockSpec((None, None, R, NS), lambda d, q: (d, q, 0, 0)),
                  pl.BlockSpec((None, None, 1, NS), lambda d, q: (d, q, 0, 0))],
        out_specs=pl.BlockSpec((None, None, R, NS), lambda d, q: (d, q, 0, 0)),
        out_shape=jax.ShapeDtypeStruct((2, NQ, R, NS), F32),
        compiler_params=pltpu.CompilerParams(dimension_semantics=("arbitrary", "arbitrary")),
        name="s5_scan",
    )(s, a_t)
    dsk = jnp.tile(d_skip.astype(F32).reshape(NQ, 1, V7X_LANES), (1, 1, T))
    y = pl.pallas_call(
        _s5_out_kernel,
        grid=(NQ,),
        in_specs=[pl.BlockSpec((2, None, R, NS), lambda q: (0, q, 0, 0)),
                  pl.BlockSpec((2, None, NS, tw), lambda q: (0, q, 0, 0)),
                  pl.BlockSpec((None, R, tw), lambda q: (q, 0, 0)),
                  pl.BlockSpec((None, R, tw), lambda q: (q, 0, 0)),
                  pl.BlockSpec((None, 1, tw), lambda q: (q, 0, 0))],
        out_specs=pl.BlockSpec((None, R, tw), lambda q: (q, 0, 0)),
        out_shape=jax.ShapeDtypeStruct((NQ, R, tw), F32),
        compiler_params=params,
        name="s5_out",
    )(x, vmat, y_in, u, dsk)
    y = y.reshape(NQ, R, T, V7X_LANES).transpose(1, 2, 0, 3).reshape(R * T, S5_WIDTH)
    return y[:B * Lc], y[B * Lc:]


def s5_glu(y, w, b):
    y = jax.nn.gelu(y)
    return y * jax.nn.sigmoid(mm(y, w) + b.astype(F32))


def merge_branches(h, branches, w_branch, w_gate, b_gate, w_out):
    merged = jnp.zeros(h.shape, h.dtype)
    for i, br in enumerate(branches):
        gate = jax.nn.sigmoid(mm_nd(h, w_gate[i]) + b_gate[i])
        merged = merged + gate * mm_nd(br.astype(h.dtype), w_branch[i])
    return mm_nd(merged, w_out)


def moe_ec(h, router_w, w_gu, w_down):
    B, N, D = h.shape
    cap = max(1, EC_CAPACITY * N // N_EXPERTS)
    logits = jnp.einsum('bnd,de->bne', h, router_w, precision=lax.Precision.HIGHEST)
    aff = jax.nn.softmax(logits.astype(F32), axis=-1)
    gate, idx = lax.top_k(jnp.swapaxes(aff, 1, 2), cap)
    xs = jax.vmap(lambda hb, ib: hb[ib])(h, idx)
    gu = expert_mm(xs, w_gu)
    g_, u_ = jnp.split(gu, 2, axis=-1)
    y = expert_mm(jax.nn.silu(g_) * u_, w_down) * gate[..., None].astype(h.dtype)
    return jax.vmap(lambda ib, yb: jnp.zeros((N, D), yb.dtype).at[ib.reshape(-1)].add(yb.reshape(-1, D)))(idx, y)


def kernel(x, c, ctx, c_ctx, ada_w, ada_b, mix_pre_g, mix_post_g, ffn_pre_g, ffn_post_g, w_in, na_rpb,
           hy_conv_w, hy_conv_b, hy_w1, hy_b1, hy_w2, hy_b2, hy_w3, hy_freq, hy_bias,
           gla_w2, gla_b2, gla_norm_g, s5_a_re, s5_a_im, s5_log_dt, s5_b_re, s5_b_im, s5_c_re, s5_c_im,
           s5_d, s5_glu_w, s5_glu_b, w_branch, w_gate, b_gate, w_out, router_w, ex_w_gu, ex_w_down):
    xl, xc = x, ctx
    B, L, Lc = x.shape[0], x.shape[1], ctx.shape[1]
    for l in range(DEPTH):
        need_ctx = l < DEPTH - 1
        cvecs = jnp.concatenate([c, c_ctx[None, :]], axis=0)
        mod = mm(jax.nn.silu(cvecs), ada_w[l]) + ada_b[l]
        ml = [p[:, None, :] for p in jnp.split(mod[:B], 6, axis=-1)]
        mc = jnp.split(mod[B], 6, axis=-1)

        hl = rms_norm(xl, mix_pre_g[l]) * (1.0 + ml[1]) + ml[0]
        hc = rms_norm(xc, mix_pre_g[l]) * (1.0 + mc[1]) + mc[0]
        hl2, hc2 = hl.reshape(B * L, D_MODEL), hc.reshape(B * Lc, D_MODEL)
        w_a, w_s5 = w_in[l][:, IN_MAIN:IN_MAIN + 2 * GLA_GATE_RANK], w_in[l][:, IN_MAIN + 2 * GLA_GATE_RANK:]
        zl, zc = mm(hl2, w_in[l], IN_MAIN), mm(hc2, w_in[l], IN_MAIN)
        zl3, zc3 = zl.reshape(B, L, IN_MAIN), zc.reshape(B, Lc, IN_MAIN)

        na_l = na_latent(zl, zc, na_bias_table(na_rpb[l], L // GRID_W), B, L, Lc).reshape(B, L, NA_WIDTH)
        hy_args = (hy_conv_w[l], hy_conv_b[l], hy_w1[l], hy_b1[l], hy_w2[l], hy_b2[l], hy_w3[l], hy_freq[l], hy_bias[l])
        hy_l = hyena_mixer(zl3[..., IN_OFF[1]:IN_OFF[2]], *hy_args)
        gla_lat = [zl3[..., IN_OFF[i]:IN_OFF[i + 1]] for i in range(2, 6)] + [mm(hl2, w_a).reshape(B, L, -1)]
        gla_ctx = [zc3[..., IN_OFF[i]:IN_OFF[i + 1]] for i in range(2, 6)] + [mm(hc2, w_a).reshape(B, Lc, -1)]
        gla_l, gla_c = gla_mixer(gla_lat, gla_ctx, gla_w2[l], gla_b2[l], gla_norm_g[l], need_ctx)
        s5_ops = s5_operators(s5_a_re[l], s5_a_im[l], s5_log_dt[l], s5_b_re[l], s5_b_im[l], s5_c_re[l], s5_c_im[l])
        s5_yc, s5_yl = s5_core(mm(hc2, w_s5), mm(hl2, w_s5), s5_ops, s5_d[l], B, L, Lc)
        s5_l = s5_glu(s5_yl, s5_glu_w[l], s5_glu_b[l]).reshape(B, L, S5_WIDTH)
        yl = merge_branches(hl, (na_l, hy_l, gla_l, s5_l), w_branch[l], w_gate[l], b_gate[l], w_out[l])
        xl = xl + ml[2] * rms_norm(yl, mix_post_g[l])
        if need_ctx:
            na_c = na_context(zc, B, Lc).reshape(B, Lc, NA_WIDTH)
            hy_c = hyena_mixer(zc3[..., IN_OFF[1]:IN_OFF[2]], *hy_args)
            s5_c = s5_glu(s5_yc, s5_glu_w[l], s5_glu_b[l]).reshape(B, Lc, S5_WIDTH)
            yc = merge_branches(hc, (na_c, hy_c, gla_c, s5_c), w_branch[l], w_gate[l], b_gate[l], w_out[l])
            xc = xc + mc[2] * rms_norm(yc, mix_post_g[l])

        hl = rms_norm(xl, ffn_pre_g[l]) * (1.0 + ml[4]) + ml[3]
        xl = xl + ml[5] * rms_norm(moe_ec(hl, router_w[l], ex_w_gu[l], ex_w_down[l]), ffn_post_g[l])
        if need_ctx:
            hc = rms_norm(xc, ffn_pre_g[l]) * (1.0 + mc[4]) + mc[3]
            xc = xc + mc[5] * rms_norm(moe_ec(hc, router_w[l], ex_w_gu[l], ex_w_down[l]), ffn_post_g[l])
    return xl
```

```python
import functools
import math

import jax
import jax.numpy as jnp
import numpy as np
from jax import lax
from jax.experimental import pallas as pl
from jax.experimental.pallas import tpu as pltpu

D_MODEL = 4096
BATCH = 2
SEQ = 4096
DEPTH = 2
CTX_LEN = 256
GRID_W = 64

NA_HEADS = 8
NA_HEAD_DIM = 128
NA_WIDTH = NA_HEADS * NA_HEAD_DIM
NA_WIN_R = 8
NA_WIN_C = 16

HY_WIDTH = 1024
HY_ORDER = 2
HY_BANDS = 16
HY_POS_DIM = 1 + 2 * HY_BANDS
HY_FILTER_HIDDEN = 64
HY_DECAY_TARGET = 1e-2
HY_FAST_DECAY = 0.3
HY_SLOW_DECAY = 1.5

GLA_HEADS = 4
GLA_DK = 128
GLA_DV = 256
GLA_QK_WIDTH = GLA_HEADS * GLA_DK
GLA_V_WIDTH = GLA_HEADS * GLA_DV
GLA_GATE_RANK = 16
GLA_TAU = 16.0
GLA_CHUNK = 64

S5_WIDTH = 1024
S5_GROUP = 16
S5_GROUPS = S5_WIDTH // S5_GROUP
S5_STATE = 64

N_BRANCH = 4
BRANCH_WIDTH = 1024
N_EXPERTS = 16
EXPERT_FF = 1024
EC_CAPACITY = 2

ROPE_BASE = 10000.0
RMS_EPS = 1e-6
NEG_INF = -1e30
F32 = jnp.float32
BF16 = jnp.bfloat16

IN_SPLITS = (3 * NA_WIDTH, (HY_ORDER + 1) * HY_WIDTH, GLA_QK_WIDTH, GLA_QK_WIDTH, GLA_V_WIDTH, GLA_V_WIDTH,
             2 * GLA_GATE_RANK, S5_WIDTH)
IN_COLS = sum(IN_SPLITS)
IN_OFF = tuple(int(v) for v in np.concatenate([[0], np.cumsum(IN_SPLITS)]))
IN_MAIN = IN_OFF[6]

V7X_LANES = 128
V7X_SUBLANES = 8
V7X_VMEM_LIMIT_BYTES = 56 * 1024 * 1024


def _mm_kernel(x_ref, w_ref, o_ref):
    o_ref[...] = jnp.dot(x_ref[...].astype(BF16), w_ref[...].astype(BF16), preferred_element_type=F32)


def _mm_tiles(M, K, N):
    tm = M if M <= 512 else 512
    tn = 512 if K > 1024 else 1024
    tn = min(tn, N)
    return tm, tn


def mm(x, w, ncols=None):
    M, K = x.shape
    N = w.shape[1] if ncols is None else ncols
    pad = (-M) % V7X_SUBLANES
    if pad:
        x = jnp.pad(x, ((0, pad), (0, 0)))
    Mp = M + pad
    tm, tn = _mm_tiles(Mp, K, N)
    assert Mp % tm == 0 and (N == w.shape[1] or N % tn == 0)
    out = pl.pallas_call(
        _mm_kernel,
        grid=(pl.cdiv(N, tn), Mp // tm),
        in_specs=[pl.BlockSpec((tm, K), lambda j, i: (i, 0)),
                  pl.BlockSpec((K, tn), lambda j, i: (0, j))],
        out_specs=pl.BlockSpec((tm, tn), lambda j, i: (i, j)),
        out_shape=jax.ShapeDtypeStruct((Mp, N), F32),
        compiler_params=pltpu.CompilerParams(dimension_semantics=("arbitrary", "arbitrary"),
                                             vmem_limit_bytes=V7X_VMEM_LIMIT_BYTES),
        name="mm",
    )(x, w)
    return out[:M] if pad else out


def mm_nd(x, w):
    lead = x.shape[:-1]
    return mm(x.reshape(-1, x.shape[-1]), w).reshape(*lead, w.shape[1])


def rms_norm(x, g):
    xf = x.astype(F32)
    y = xf * lax.rsqrt(jnp.mean(xf * xf, axis=-1, keepdims=True) + RMS_EPS)
    return (y * g.astype(F32)).astype(x.dtype)


def dense_attention(q, k, v):
    scale = q.shape[-1] ** -0.5
    s = jnp.einsum('bqhd,bkhd->bhqk', q, k, preferred_element_type=F32) * scale
    p = jax.nn.softmax(s, axis=-1).astype(v.dtype)
    o = jnp.einsum('bhqk,bkhd->bqhd', p, v)
    return o.reshape(q.shape[0], q.shape[1], -1)


NA_TILE_ROWS = 8


def na_bias_table(rpb, rows):
    nt = rows // NA_TILE_ROWS
    cases = jnp.array([0, min(1, nt - 1), nt - 1])
    rq = jnp.arange(NA_TILE_ROWS)
    rk = jnp.arange(2 * NA_TILE_ROWS)
    c = jnp.arange(GRID_W)
    r = cases[:, None] * NA_TILE_ROWS + rq[None, :]
    ks = jnp.clip(cases * NA_TILE_ROWS - NA_WIN_R // 2, 0, rows - 2 * NA_TILE_ROWS)
    krow = ks[:, None] + rk[None, :]
    ws = jnp.clip(r - NA_WIN_R // 2, 0, rows - NA_WIN_R)
    row_ok = (krow[:, None, :] >= ws[:, :, None]) & (krow[:, None, :] < ws[:, :, None] + NA_WIN_R)
    row_off = jnp.clip(krow[:, None, :] - r[:, :, None] + (NA_WIN_R - 1), 0, 2 * NA_WIN_R - 2)
    col_start = jnp.clip(c - NA_WIN_C // 2, 0, GRID_W - NA_WIN_C)
    col_ok = (c[None, :] >= col_start[:, None]) & (c[None, :] < col_start[:, None] + NA_WIN_C)
    col_off = jnp.clip(c[None, :] - c[:, None] + (NA_WIN_C - 1), 0, 2 * NA_WIN_C - 2)
    b = rpb.astype(F32)[:, row_off[:, :, None, :, None], col_off[None, None, :, None, :]]
    ok = row_ok[:, :, None, :, None] & col_ok[None, None, :, None, :]
    b = jnp.where(ok[None], b, NEG_INF).transpose(1, 0, 2, 3, 4, 5)
    return b.reshape(3, rpb.shape[0], NA_TILE_ROWS * GRID_W, 2 * NA_TILE_ROWS * GRID_W).astype(BF16)


def _na_kernel(q_ref, k_ref, v_ref, kc_ref, vc_ref, b_ref, o_ref, *, scale):
    q = q_ref[...].astype(BF16)
    nk = k_ref.shape[0]
    k = jnp.concatenate([k_ref[...].astype(BF16), kc_ref[...].astype(BF16)], axis=0)
    v = jnp.concatenate([v_ref[...].astype(BF16), vc_ref[...].astype(BF16)], axis=0)
    s = lax.dot_general(q, k, (((1,), (1,)), ((), ())), preferred_element_type=F32) * scale
    bias = jnp.concatenate([b_ref[...].astype(F32), jnp.zeros((s.shape[0], s.shape[1] - nk), F32)], axis=1)
    s = s + bias
    m = jnp.max(s, axis=-1, keepdims=True)
    p = jnp.exp(s - m)
    den = jnp.sum(p, axis=-1, keepdims=True)
    o = jnp.dot(p.astype(BF16), v, preferred_element_type=F32)
    o_ref[...] = o / den


def na_latent(zl, zc, bias_tab, B, L, Lc):
    rows = L // GRID_W
    nt = rows // NA_TILE_ROWS
    tq = NA_TILE_ROWS * GRID_W
    tk = 2 * tq
    H, dh = NA_HEADS, NA_HEAD_DIM

    def case(j):
        return jnp.where(j == 0, 0, jnp.where(j == nt - 1, 2, 1))

    def kstart(b, j):
        ks = jnp.clip(j * NA_TILE_ROWS - NA_WIN_R // 2, 0, rows - 2 * NA_TILE_ROWS)
        return pl.multiple_of(b * L + ks * GRID_W, (NA_WIN_R // 2) * GRID_W)

    def kv_spec(col0):
        return pl.BlockSpec((pl.Element(tk), pl.Element(dh)),
                            lambda b, j, h: (kstart(b, j), pl.multiple_of((col0 + h) * dh, dh)))

    return pl.pallas_call(
        functools.partial(_na_kernel, scale=dh ** -0.5),
        grid=(B, nt, H),
        in_specs=[
            pl.BlockSpec((tq, dh), lambda b, j, h: (b * nt + j, h)),
            kv_spec(H),
            kv_spec(2 * H),
            pl.BlockSpec((Lc, dh), lambda b, j, h: (b, H + h)),
            pl.BlockSpec((Lc, dh), lambda b, j, h: (b, 2 * H + h)),
            pl.BlockSpec((None, None, tq, tk), lambda b, j, h: (case(j), h, 0, 0)),
        ],
        out_specs=pl.BlockSpec((tq, dh), lambda b, j, h: (b * nt + j, h)),
        out_shape=jax.ShapeDtypeStruct((B * L, H * dh), F32),
        compiler_params=pltpu.CompilerParams(dimension_semantics=("arbitrary",) * 3),
        name="na_latent",
    )(zl, zl, zl, zc, zc, bias_tab)


def na_context(zc, B, Lc):
    qc, kc, vc = [t.reshape(B, Lc, NA_HEADS, NA_HEAD_DIM) for t in jnp.split(zc[:, :3 * NA_WIDTH], 3, axis=-1)]
    return dense_attention(qc, kc, vc).reshape(B * Lc, NA_WIDTH)


def short_conv3(u, w, b):
    up = jnp.pad(u, ((0, 0), (1, 1), (0, 0)))
    return up[:, :-2] * w[0] + up[:, 1:-1] * w[1] + up[:, 2:] * w[2] + b


def hyena_filters(L, w1, b1, w2, b2, w3, freq):
    t = jnp.arange(L, dtype=F32)
    t01 = t / max(L - 1, 1)
    bands = jnp.linspace(1e-4, HY_BANDS - 1, HY_BANDS, dtype=F32)
    ang = (2.0 * math.pi / L) * t[:, None] * bands[None, :]
    z = jnp.concatenate([t01[:, None], jnp.cos(ang), -jnp.sin(ang)], axis=-1)
    f = freq.astype(F32)
    h = jnp.sin(f * (z @ w1.astype(F32) + b1.astype(F32)))
    h = jnp.sin(f * (h @ w2.astype(F32) + b2.astype(F32)))
    h = (h @ w3.astype(F32)).reshape(L, 2, HY_ORDER, HY_WIDTH)
    deltas = jnp.abs(jnp.linspace(math.log(HY_DECAY_TARGET) / HY_SLOW_DECAY, math.log(HY_DECAY_TARGET) / HY_FAST_DECAY,
                                  HY_WIDTH, dtype=F32))
    h = h * jnp.exp(-t01[:, None] * deltas[None, :])[:, None, None, :]
    kern = jnp.concatenate([h[:, 0], jnp.zeros((1, HY_ORDER, HY_WIDTH), F32), h[:0:-1, 1]], axis=0)
    return kern / jnp.sum(jnp.abs(kern), axis=0, keepdims=True)


def fft_long_conv(u, kern, bias):
    L = u.shape[1]
    n = 2 * L
    spec = jnp.fft.rfft(u, n=n, axis=1) * jnp.fft.rfft(kern, n=n, axis=0)[None]
    return jnp.fft.irfft(spec, n=n, axis=1)[:, :L] + u * bias


def hyena_mixer(z, conv_w, conv_b, w1, b1, w2, b2, w3, freq, bias):
    L = z.shape[1]
    zc = short_conv3(z.astype(F32), conv_w.astype(F32), conv_b.astype(F32))
    parts = jnp.split(zc, HY_ORDER + 1, axis=-1)
    kern = hyena_filters(L, w1, b1, w2, b2, w3, freq)
    y = parts[0]
    for o in range(HY_ORDER):
        y = parts[o + 1] * fft_long_conv(y, kern[:, o], bias[o].astype(F32))
    return y


def rope_tables(L, rotary):
    if not rotary:
        return jnp.ones((L, GLA_DK), F32), jnp.zeros((L, GLA_DK), F32)
    t = jnp.arange(L)
    pos = jnp.stack([(t // GRID_W).astype(F32), (t % GRID_W).astype(F32)], axis=1)
    quarter = GLA_DK // 4
    inv = ROPE_BASE ** (-jnp.arange(quarter, dtype=F32) / quarter)
    ang = pos[:, :, None] * inv[None, None, :]
    cos = jnp.concatenate([jnp.cos(ang), jnp.cos(ang)], axis=-1).reshape(L, GLA_DK)
    sin = jnp.concatenate([-jnp.sin(ang), jnp.sin(ang)], axis=-1).reshape(L, GLA_DK)
    return cos, sin


def _gla_kernel(q_ref, k_ref, v_ref, g_ref, a_ref, cos_ref, sin_ref, w2_ref, b2_ref, ng_ref, s0_ref, o_ref, st_ref,
                *, n_chunks):
    C = GLA_CHUNK
    quarter = GLA_DK // 4
    lane = lax.broadcasted_iota(jnp.int32, (C, GLA_DK), 1)
    first_half = (lane % (2 * quarter)) < quarter
    row = lax.broadcasted_iota(jnp.int32, (C, C), 0)
    col = lax.broadcasted_iota(jnp.int32, (C, C), 1)
    tri = (row >= col, row <= col)
    o_ref[...] = jnp.zeros_like(o_ref)
    st_ref[...] = s0_ref[...]

    def rope(x, cos, sin):
        swapped = jnp.where(first_half, pltpu.roll(x, GLA_DK - quarter, 1), pltpu.roll(x, quarter, 1))
        return x * cos + swapped * sin

    def chunk(c, d):
        sl = pl.ds(pl.multiple_of(c * C, C), C)
        cos, sin = cos_ref[sl, :], sin_ref[sl, :]
        q = rope(q_ref[sl, :], cos, sin) * GLA_DK ** -0.5
        k = rope(k_ref[sl, :], cos, sin)
        v = v_ref[sl, :].astype(BF16)
        a = a_ref[sl, d * GLA_GATE_RANK:(d + 1) * GLA_GATE_RANK]
        pre = jnp.dot(a.astype(BF16), w2_ref[d].astype(BF16), preferred_element_type=F32) + b2_ref[d]
        log_a = jax.nn.log_sigmoid(pre) / GLA_TAU
        mask = tri[d]
        bcum = jnp.dot(mask.astype(F32), log_a, preferred_element_type=F32, precision=lax.Precision.HIGHEST)
        blast = jnp.sum(log_a, axis=0, keepdims=True)
        q_in = (q * jnp.exp(bcum)).astype(BF16)
        k_in = (k * jnp.exp(-bcum)).astype(BF16)
        k_st = (k * jnp.exp(blast - bcum)).astype(BF16)
        att = lax.dot_general(q_in, k_in, (((1,), (1,)), ((), ())), preferred_element_type=F32)
        att = jnp.where(mask, att, 0.0).astype(BF16)
        st = st_ref[d]
        o = jnp.dot(att, v, preferred_element_type=F32)
        o = o + lax.dot_general(q_in, st.astype(BF16), (((1,), (1,)), ((), ())), preferred_element_type=F32)
        o_ref[sl, :] += o
        kv_t = lax.dot_general(v, k_st, (((0,), (0,)), ((), ())), preferred_element_type=F32)
        st_ref[d] = st * jnp.exp(blast) + kv_t

    def body(i, carry):
        chunk(i, 0)
        chunk(n_chunks - 1 - i, 1)
        return carry

    lax.fori_loop(0, n_chunks, body, 0)
    o = o_ref[...]
    o = o * lax.rsqrt(jnp.mean(o * o, axis=-1, keepdims=True) + RMS_EPS) * ng_ref[...]
    o_ref[...] = o * jax.nn.silu(g_ref[...])


def gla_segment(z, za, s0, w2, b2, norm_g, B, L, rotary):
    H, dk, dv = GLA_HEADS, GLA_DK, GLA_DV
    cos, sin = rope_tables(L, rotary)
    qb, kb, vb, gb = IN_OFF[2] // dk, IN_OFF[3] // dk, IN_OFF[4] // dv, IN_OFF[5] // dv
    return pl.pallas_call(
        functools.partial(_gla_kernel, n_chunks=L // GLA_CHUNK),
        grid=(B, H),
        in_specs=[pl.BlockSpec((L, dk), lambda b, h: (b, qb + h)),
                  pl.BlockSpec((L, dk), lambda b, h: (b, kb + h)),
                  pl.BlockSpec((L, dv), lambda b, h: (b, vb + h)),
                  pl.BlockSpec((L, dv), lambda b, h: (b, gb + h)),
                  pl.BlockSpec((L, 2 * GLA_GATE_RANK), lambda b, h: (b, 0)),
                  pl.BlockSpec((L, dk), lambda b, h: (0, 0)),
                  pl.BlockSpec((L, dk), lambda b, h: (0, 0)),
                  pl.BlockSpec((2, GLA_GATE_RANK, dk), lambda b, h: (0, 0, h)),
                  pl.BlockSpec((2, 1, dk), lambda b, h: (0, 0, h)),
                  pl.BlockSpec((1, dv), lambda b, h: (0, 0)),
                  pl.BlockSpec((None, None, 2, dv, dk), lambda b, h: (b, h, 0, 0, 0))],
        out_specs=[pl.BlockSpec((L, dv), lambda b, h: (b, h)),
                   pl.BlockSpec((None, None, 2, dv, dk), lambda b, h: (b, h, 0, 0, 0))],
        out_shape=[jax.ShapeDtypeStruct((B * L, H * dv), F32), jax.ShapeDtypeStruct((B, H, 2, dv, dk), F32)],
        compiler_params=pltpu.CompilerParams(dimension_semantics=("arbitrary", "arbitrary"),
                                             vmem_limit_bytes=V7X_VMEM_LIMIT_BYTES),
        name="gla_segment",
    )(z, z, z, z, za, cos, sin, w2, b2.reshape(2, 1, -1), norm_g.reshape(1, dv), s0)


def gla_mixer(zl, zc, za_l, za_c, w2, b2, norm_g, B, L, Lc):
    s0 = jnp.zeros((B, GLA_HEADS, 2, GLA_DV, GLA_DK), F32)
    out_c, s_ctx = gla_segment(zc, za_c, s0, w2, b2, norm_g, B, Lc, False)
    out_l, _ = gla_segment(zl, za_l, s_ctx, w2, b2, norm_g, B, L, True)
    return out_l, out_c


def s5_discretise(a_re, a_im, log_dt, b_re, b_im):
    A = lax.complex(a_re.astype(F32), a_im.astype(F32))
    dt = jnp.exp(log_dt.astype(F32))[:, None]
    a_bar = jnp.exp(A * dt)
    b_bar = ((a_bar - 1.0) / A)[..., None] * lax.complex(b_re.astype(F32), b_im.astype(F32))
    return a_bar, b_bar


S5_T = 16
S5_GQ = V7X_LANES // S5_GROUP
S5_NQ = S5_GROUPS // S5_GQ
S5_NS = 2 * S5_GQ * S5_STATE


def s5_operators(a_re, a_im, log_dt, b_re, b_im, c_re, c_im):
    T, GQ, NQ, P, I = S5_T, S5_GQ, S5_NQ, S5_STATE, S5_GROUP
    eye = jnp.eye(GQ, dtype=F32)
    w_parts, v_parts, at_parts = [], [], []
    ktot = 0.0
    for d in range(2):
        a_bar, b_bar = s5_discretise(a_re[d], a_im[d], log_dt[d], b_re[d], b_im[d])
        c_mat = lax.complex(c_re[d].astype(F32), c_im[d].astype(F32))
        e = jnp.arange(T + 1, dtype=F32)
        apow = a_bar[None] ** e[:, None, None].astype(jnp.complex64)
        ex = (T - 1 - jnp.arange(T)) if d == 0 else jnp.arange(T)
        w = apow[ex][:, :, :, None] * b_bar[None]
        w = jnp.stack([w.real, w.imag], axis=0).reshape(2, T, NQ, GQ, P, I)
        w = jnp.einsum('rsqgpj,gh->qsgjrhp', w, eye)
        w_parts.append(w.reshape(NQ, T * GQ * I, S5_NS))
        k = jnp.einsum('gip,tgp,gpj->tgij', c_mat, apow[:T], b_bar).real
        t_idx = jnp.arange(T)[:, None]
        s_idx = jnp.arange(T)[None, :]
        lag = (t_idx - s_idx) if d == 0 else (s_idx - t_idx)
        ktot = ktot + jnp.where((lag >= 0)[:, :, None, None, None], k[jnp.clip(lag, 0, T - 1)], 0.0)
        ey = (jnp.arange(T) + 1) if d == 0 else (T - jnp.arange(T))
        v = c_mat[None] * apow[ey][:, :, None, :]
        v = jnp.stack([v.real, -v.imag], axis=0).reshape(2, T, NQ, GQ, I, P)
        v = jnp.einsum('rtqgip,gh->qrgpthi', v, eye)
        v_parts.append(v.reshape(NQ, S5_NS, T * GQ * I))
        at = apow[T].reshape(NQ, GQ * P)
        at_parts.append(jnp.concatenate([at.real, at.imag], axis=-1)[:, None, :])
    m = ktot.reshape(T, T, NQ, GQ, I, I)
    m = jnp.einsum('tsqgij,gh->qsgjthi', m, eye).reshape(NQ, T * GQ * I, T * GQ * I)
    rhs = jnp.concatenate([w_parts[0], w_parts[1], m], axis=-1).astype(BF16)
    return rhs, jnp.stack(v_parts).astype(BF16), jnp.stack(at_parts)


def _s5_in_kernel(u_ref, rhs_ref, s_ref, y_ref):
    r = jnp.dot(u_ref[...], rhs_ref[...], preferred_element_type=F32)
    s_ref[0] = r[:, :S5_NS]
    s_ref[1] = r[:, S5_NS:2 * S5_NS]
    y_ref[...] = r[:, 2 * S5_NS:]


def _s5_scan_kernel(s_ref, at_ref, x_ref, *, n_ctx, n_lat, batch):
    d = pl.program_id(0)
    half = S5_NS // 2
    a_r = at_ref[:, :half]
    a_i = at_ref[:, half:]

    def run(b, base, n, carry):
        def body(i, st):
            xr, xi = st
            c = i + d * (n - 1 - 2 * i)
            row = base + b * n + c
            x_ref[pl.ds(row, 1), :] = jnp.concatenate([xr, xi], axis=-1)
            s = s_ref[pl.ds(row, 1), :]
            return a_r * xr - a_i * xi + s[:, :half], a_r * xi + a_i * xr + s[:, half:]
        return lax.fori_loop(0, n, body, carry)

    for b in range(batch):
        zero = jnp.zeros((1, half), F32)
        st = run(b, 0, n_ctx, (zero, zero))
        run(b, batch * n_ctx, n_lat, st)


def _s5_out_kernel(x_ref, v_ref, yin_ref, u_ref, dsk_ref, y_ref):
    y = yin_ref[...] + dsk_ref[...] * u_ref[...]
    y = y + jnp.dot(x_ref[0].astype(BF16), v_ref[0], preferred_element_type=F32)
    y = y + jnp.dot(x_ref[1].astype(BF16), v_ref[1], preferred_element_type=F32)
    y_ref[...] = y


def s5_core(uc, ul, ops, d_skip, B, L, Lc):
    rhs, vmat, a_t = ops
    T, NQ, NS = S5_T, S5_NQ, S5_NS
    n_ctx, n_lat = Lc // T, L // T
    R = B * (n_ctx + n_lat)
    tw = T * V7X_LANES
    u = jnp.concatenate([uc.reshape(B * n_ctx, T, NQ, V7X_LANES), ul.reshape(B * n_lat, T, NQ, V7X_LANES)], axis=0)
    u = u.transpose(2, 0, 1, 3).reshape(NQ, R, tw)
    params = pltpu.CompilerParams(dimension_semantics=("arbitrary",), vmem_limit_bytes=V7X_VMEM_LIMIT_BYTES)
    s, y_in = pl.pallas_call(
        _s5_in_kernel,
        grid=(NQ,),
        in_specs=[pl.BlockSpec((None, R, tw), lambda q: (q, 0, 0)),
                  pl.BlockSpec((None, tw, 2 * NS + tw), lambda q: (q, 0, 0))],
        out_specs=[pl.BlockSpec((2, None, R, NS), lambda q: (0, q, 0, 0)),
                   pl.BlockSpec((None, R, tw), lambda q: (q, 0, 0))],
        out_shape=[jax.ShapeDtypeStruct((2, NQ, R, NS), F32), jax.ShapeDtypeStruct((NQ, R, tw), F32)],
        compiler_params=params,
        name="s5_in",
    )(u.astype(BF16), rhs)
    x = pl.pallas_call(
        functools.partial(_s5_scan_kernel, n_ctx=n_ctx, n_lat=n_lat, batch=B),
        grid=(2, NQ),
        in_specs=[pl.BlockSpec((None, None, R, NS), lambda d, q: (d, q, 0, 0)),
                  pl.BlockSpec((None, None, 1, NS), lambda d, q: (d, q, 0, 0))],
        out_specs=pl.BlockSpec((None, None, R, NS), lambda d, q: (d, q, 0, 0)),
        out_shape=jax.ShapeDtypeStruct((2, NQ, R, NS), F32),
        compiler_params=pltpu.CompilerParams(dimension_semantics=("arbitrary", "arbitrary")),
        name="s5_scan",
    )(s, a_t)
    dsk = jnp.tile(d_skip.astype(F32).reshape(NQ, 1, V7X_LANES), (1, 1, T))
    y = pl.pallas_call(
        _s5_out_kernel,
        grid=(NQ,),
        in_specs=[pl.BlockSpec((2, None, R, NS), lambda q: (0, q, 0, 0)),
                  pl.BlockSpec((2, None, NS, tw), lambda q: (0, q, 0, 0)),
                  pl.BlockSpec((None, R, tw), lambda q: (q, 0, 0)),
                  pl.BlockSpec((None, R, tw), lambda q: (q, 0, 0)),
                  pl.BlockSpec((None, 1, tw), lambda q: (q, 0, 0))],
        out_specs=pl.BlockSpec((None, R, tw), lambda q: (q, 0, 0)),
        out_shape=jax.ShapeDtypeStruct((NQ, R, tw), F32),
        compiler_params=params,
        name="s5_out",
    )(x, vmat, y_in, u, dsk)
    y = y.reshape(NQ, R, T, V7X_LANES).transpose(1, 2, 0, 3).reshape(R * T, S5_WIDTH)
    return y[:B * Lc], y[B * Lc:]


def s5_glu(y, w, b):
    y = jax.nn.gelu(y)
    return y * jax.nn.sigmoid(mm(y, w) + b.astype(F32))


def merge_branches(h, branches, w_branch, w_gate, b_gate, w_out):
    merged = jnp.zeros(h.shape, h.dtype)
    for i, br in enumerate(branches):
        gate = jax.nn.sigmoid(mm_nd(h, w_gate[i]) + b_gate[i])
        merged = merged + gate * mm_nd(br.astype(h.dtype), w_branch[i])
    return mm_nd(merged, w_out)


def _moe_gather_kernel(idx_ref, h_ref, o_ref, stage_ref, sem, *, n_tokens, n_experts):
    g = pl.program_id(0)
    cap = o_ref.shape[0]
    b = g // n_experts

    def row_copy(c):
        row = b * n_tokens + idx_ref[g * cap + c]
        return pltpu.make_async_copy(h_ref.at[pl.ds(row, 1), :], stage_ref.at[pl.ds(c, 1), :], sem)

    def start(c, carry):
        row_copy(c).start()
        return carry

    def wait(c, carry):
        row_copy(c).wait()
        return carry

    lax.fori_loop(0, cap, start, 0)
    lax.fori_loop(0, cap, wait, 0)
    o_ref[...] = stage_ref[...].astype(BF16)


def moe_gather(h, idx):
    B, N, D = h.shape
    _, E, cap = idx.shape
    return pl.pallas_call(
        functools.partial(_moe_gather_kernel, n_tokens=N, n_experts=E),
        grid_spec=pltpu.PrefetchScalarGridSpec(
            num_scalar_prefetch=1,
            grid=(B * E,),
            in_specs=[pl.BlockSpec(memory_space=pl.ANY)],
            out_specs=pl.BlockSpec((None, cap, D), lambda g, idx_ref: (g, 0, 0)),
            scratch_shapes=[pltpu.VMEM((cap, D), F32), pltpu.SemaphoreType.DMA(())],
        ),
        out_shape=jax.ShapeDtypeStruct((B * E, cap, D), BF16),
        compiler_params=pltpu.CompilerParams(dimension_semantics=("arbitrary",),
                                             vmem_limit_bytes=V7X_VMEM_LIMIT_BYTES),
        name="moe_gather",
    )(idx.reshape(-1).astype(jnp.int32), h.reshape(B * N, D)).reshape(B, E, cap, D)


def _expert_gu_kernel(x_ref, wg_ref, wu_ref, o_ref):
    x = x_ref[...]
    g = jnp.dot(x, wg_ref[...].astype(BF16), preferred_element_type=F32)
    u = jnp.dot(x, wu_ref[...].astype(BF16), preferred_element_type=F32)
    o_ref[...] = (jax.nn.silu(g) * u).astype(BF16)


def expert_gu(xs, w_gu):
    B, E, C, D = xs.shape
    F = w_gu.shape[2] // 2
    tn = 512
    nj = F // tn
    return pl.pallas_call(
        _expert_gu_kernel,
        grid=(E, nj, B),
        in_specs=[pl.BlockSpec((None, None, C, D), lambda e, j, b: (b, e, 0, 0)),
                  pl.BlockSpec((None, D, tn), lambda e, j, b: (e, 0, j)),
                  pl.BlockSpec((None, D, tn), lambda e, j, b: (e, 0, nj + j))],
        out_specs=pl.BlockSpec((None, None, C, tn), lambda e, j, b: (b, e, 0, j)),
        out_shape=jax.ShapeDtypeStruct((B, E, C, F), BF16),
        compiler_params=pltpu.CompilerParams(dimension_semantics=("arbitrary",) * 3,
                                             vmem_limit_bytes=V7X_VMEM_LIMIT_BYTES),
        name="expert_gu",
    )(xs, w_gu, w_gu)


def _expert_down_kernel(a_ref, w_ref, gate_ref, o_ref):
    y = jnp.dot(a_ref[...], w_ref[...].astype(BF16), preferred_element_type=F32)
    o_ref[...] = y * gate_ref[...]


def expert_down(act, w_down, gate):
    B, E, C, F = act.shape
    D = w_down.shape[2]
    tn = min(1024, D)
    assert D % tn == 0
    return pl.pallas_call(
        _expert_down_kernel,
        grid=(E, D // tn, B),
        in_specs=[pl.BlockSpec((None, None, C, F), lambda e, j, b: (b, e, 0, 0)),
                  pl.BlockSpec((None, F, tn), lambda e, j, b: (e, 0, j)),
                  pl.BlockSpec((None, None, C, 1), lambda e, j, b: (b, e, 0, 0))],
        out_specs=pl.BlockSpec((None, None, C, tn), lambda e, j, b: (b, e, 0, j)),
        out_shape=jax.ShapeDtypeStruct((B, E, C, D), F32),
        compiler_params=pltpu.CompilerParams(dimension_semantics=("arbitrary",) * 3,
                                             vmem_limit_bytes=V7X_VMEM_LIMIT_BYTES),
        name="expert_down",
    )(act, w_down, gate[..., None])


def moe_ec(h, router_w, w_gu, w_down):
    B, N, D = h.shape
    cap = max(1, EC_CAPACITY * N // N_EXPERTS)
    logits = jnp.einsum('bnd,de->bne', h, router_w, precision=lax.Precision.HIGHEST)
    aff = jax.nn.softmax(logits.astype(F32), axis=-1)
    gate, idx = lax.top_k(jnp.swapaxes(aff, 1, 2), cap)
    y = expert_down(expert_gu(moe_gather(h, idx), w_gu), w_down, gate)
    return jax.vmap(lambda ib, yb: jnp.zeros((N, D), yb.dtype).at[ib.reshape(-1)].add(yb.reshape(-1, D)))(idx, y)


def kernel(x, c, ctx, c_ctx, ada_w, ada_b, mix_pre_g, mix_post_g, ffn_pre_g, ffn_post_g, w_in, na_rpb,
           hy_conv_w, hy_conv_b, hy_w1, hy_b1, hy_w2, hy_b2, hy_w3, hy_freq, hy_bias,
           gla_w2, gla_b2, gla_norm_g, s5_a_re, s5_a_im, s5_log_dt, s5_b_re, s5_b_im, s5_c_re, s5_c_im,
           s5_d, s5_glu_w, s5_glu_b, w_branch, w_gate, b_gate, w_out, router_w, ex_w_gu, ex_w_down):
    xl, xc = x, ctx
    B, L, Lc = x.shape[0], x.shape[1], ctx.shape[1]
    for l in range(DEPTH):
        need_ctx = l < DEPTH - 1
        cvecs = jnp.concatenate([c, c_ctx[None, :]], axis=0)
        mod = mm(jax.nn.silu(cvecs), ada_w[l]) + ada_b[l]
        ml = [p[:, None, :] for p in jnp.split(mod[:B], 6, axis=-1)]
        mc = jnp.split(mod[B], 6, axis=-1)

        hl = rms_norm(xl, mix_pre_g[l]) * (1.0 + ml[1]) + ml[0]
        hc = rms_norm(xc, mix_pre_g[l]) * (1.0 + mc[1]) + mc[0]
        hl2, hc2 = hl.reshape(B * L, D_MODEL), hc.reshape(B * Lc, D_MODEL)
        w_a, w_s5 = w_in[l][:, IN_MAIN:IN_MAIN + 2 * GLA_GATE_RANK], w_in[l][:, IN_MAIN + 2 * GLA_GATE_RANK:]
        zl, zc = mm(hl2, w_in[l], IN_MAIN), mm(hc2, w_in[l], IN_MAIN)
        zl3, zc3 = zl.reshape(B, L, IN_MAIN), zc.reshape(B, Lc, IN_MAIN)

        na_l = na_latent(zl, zc, na_bias_table(na_rpb[l], L // GRID_W), B, L, Lc).reshape(B, L, NA_WIDTH)
        hy_args = (hy_conv_w[l], hy_conv_b[l], hy_w1[l], hy_b1[l], hy_w2[l], hy_b2[l], hy_w3[l], hy_freq[l], hy_bias[l])
        hy_l = hyena_mixer(zl3[..., IN_OFF[1]:IN_OFF[2]], *hy_args)
        gla_l, gla_c = gla_mixer(zl, zc, mm(hl2, w_a), mm(hc2, w_a), gla_w2[l], gla_b2[l], gla_norm_g[l], B, L, Lc)
        gla_l, gla_c = gla_l.reshape(B, L, GLA_V_WIDTH), gla_c.reshape(B, Lc, GLA_V_WIDTH)
        s5_ops = s5_operators(s5_a_re[l], s5_a_im[l], s5_log_dt[l], s5_b_re[l], s5_b_im[l], s5_c_re[l], s5_c_im[l])
        s5_yc, s5_yl = s5_core(mm(hc2, w_s5), mm(hl2, w_s5), s5_ops, s5_d[l], B, L, Lc)
        s5_l = s5_glu(s5_yl, s5_glu_w[l], s5_glu_b[l]).reshape(B, L, S5_WIDTH)
        yl = merge_branches(hl, (na_l, hy_l, gla_l, s5_l), w_branch[l], w_gate[l], b_gate[l], w_out[l])
        xl = xl + ml[2] * rms_norm(yl, mix_post_g[l])
        if need_ctx:
            na_c = na_context(zc, B, Lc).reshape(B, Lc, NA_WIDTH)
            hy_c = hyena_mixer(zc3[..., IN_OFF[1]:IN_OFF[2]], *hy_args)
            s5_c = s5_glu(s5_yc, s5_glu_w[l], s5_glu_b[l]).reshape(B, Lc, S5_WIDTH)
            yc = merge_branches(hc, (na_c, hy_c, gla_c, s5_c), w_branch[l], w_gate[l], b_gate[l], w_out[l])
            xc = xc + mc[2] * rms_norm(yc, mix_post_g[l])

        hl = rms_norm(xl, ffn_pre_g[l]) * (1.0 + ml[4]) + ml[3]
        xl = xl + ml[5] * rms_norm(moe_ec(hl, router_w[l], ex_w_gu[l], ex_w_down[l]), ffn_post_g[l])
        if need_ctx:
            hc = rms_norm(xc, ffn_pre_g[l]) * (1.0 + mc[4]) + mc[3]
            xc = xc + mc[5] * rms_norm(moe_ec(hc, router_w[l], ex_w_gu[l], ex_w_down[l]), ffn_post_g[l])
    return xl
```

```python
import functools
import math

import jax
import jax.numpy as jnp
import numpy as np
from jax import lax
from jax.experimental import pallas as pl
from jax.experimental.pallas import tpu as pltpu

D_MODEL = 4096
BATCH = 2
SEQ = 4096
DEPTH = 2
CTX_LEN = 256
GRID_W = 64

NA_HEADS = 8
NA_HEAD_DIM = 128
NA_WIDTH = NA_HEADS * NA_HEAD_DIM
NA_WIN_R = 8
NA_WIN_C = 16

HY_WIDTH = 1024
HY_ORDER = 2
HY_BANDS = 16
HY_POS_DIM = 1 + 2 * HY_BANDS
HY_FILTER_HIDDEN = 64
HY_DECAY_TARGET = 1e-2
HY_FAST_DECAY = 0.3
HY_SLOW_DECAY = 1.5

GLA_HEADS = 4
GLA_DK = 128
GLA_DV = 256
GLA_QK_WIDTH = GLA_HEADS * GLA_DK
GLA_V_WIDTH = GLA_HEADS * GLA_DV
GLA_GATE_RANK = 16
GLA_TAU = 16.0
GLA_CHUNK = 64

S5_WIDTH = 1024
S5_GROUP = 16
S5_GROUPS = S5_WIDTH // S5_GROUP
S5_STATE = 64

N_BRANCH = 4
BRANCH_WIDTH = 1024
N_EXPERTS = 16
EXPERT_FF = 1024
EC_CAPACITY = 2

ROPE_BASE = 10000.0
RMS_EPS = 1e-6
NEG_INF = -1e30
F32 = jnp.float32
BF16 = jnp.bfloat16

IN_SPLITS = (3 * NA_WIDTH, (HY_ORDER + 1) * HY_WIDTH, GLA_QK_WIDTH, GLA_QK_WIDTH, GLA_V_WIDTH, GLA_V_WIDTH,
             2 * GLA_GATE_RANK, S5_WIDTH)
IN_COLS = sum(IN_SPLITS)
IN_OFF = tuple(int(v) for v in np.concatenate([[0], np.cumsum(IN_SPLITS)]))
IN_MAIN = IN_OFF[6]

V7X_LANES = 128
V7X_SUBLANES = 8
V7X_VMEM_LIMIT_BYTES = 56 * 1024 * 1024


def _mm_kernel(x_ref, w_ref, o_ref):
    o_ref[...] = jnp.dot(x_ref[...].astype(BF16), w_ref[...].astype(BF16), preferred_element_type=F32)


def _mm_tiles(M, K, N):
    tm = M if M <= 512 else 512
    tn = 512 if K > 1024 else 1024
    tn = min(tn, N)
    return tm, tn


def mm(x, w, ncols=None):
    M, K = x.shape
    N = w.shape[1] if ncols is None else ncols
    pad = (-M) % V7X_SUBLANES
    if pad:
        x = jnp.pad(x, ((0, pad), (0, 0)))
    Mp = M + pad
    tm, tn = _mm_tiles(Mp, K, N)
    assert Mp % tm == 0 and (N == w.shape[1] or N % tn == 0)
    out = pl.pallas_call(
        _mm_kernel,
        grid=(pl.cdiv(N, tn), Mp // tm),
        in_specs=[pl.BlockSpec((tm, K), lambda j, i: (i, 0)),
                  pl.BlockSpec((K, tn), lambda j, i: (0, j))],
        out_specs=pl.BlockSpec((tm, tn), lambda j, i: (i, j)),
        out_shape=jax.ShapeDtypeStruct((Mp, N), F32),
        compiler_params=pltpu.CompilerParams(dimension_semantics=("arbitrary", "arbitrary"),
                                             vmem_limit_bytes=V7X_VMEM_LIMIT_BYTES),
        name="mm",
    )(x, w)
    return out[:M] if pad else out


def mm_nd(x, w):
    lead = x.shape[:-1]
    return mm(x.reshape(-1, x.shape[-1]), w).reshape(*lead, w.shape[1])


def rms_norm(x, g):
    xf = x.astype(F32)
    y = xf * lax.rsqrt(jnp.mean(xf * xf, axis=-1, keepdims=True) + RMS_EPS)
    return (y * g.astype(F32)).astype(x.dtype)


def dense_attention(q, k, v):
    scale = q.shape[-1] ** -0.5
    s = jnp.einsum('bqhd,bkhd->bhqk', q, k, preferred_element_type=F32) * scale
    p = jax.nn.softmax(s, axis=-1).astype(v.dtype)
    o = jnp.einsum('bhqk,bkhd->bqhd', p, v)
    return o.reshape(q.shape[0], q.shape[1], -1)


NA_TILE_ROWS = 8


def na_bias_table(rpb, rows):
    nt = rows // NA_TILE_ROWS
    cases = jnp.array([0, min(1, nt - 1), nt - 1])
    rq = jnp.arange(NA_TILE_ROWS)
    rk = jnp.arange(2 * NA_TILE_ROWS)
    c = jnp.arange(GRID_W)
    r = cases[:, None] * NA_TILE_ROWS + rq[None, :]
    ks = jnp.clip(cases * NA_TILE_ROWS - NA_WIN_R // 2, 0, rows - 2 * NA_TILE_ROWS)
    krow = ks[:, None] + rk[None, :]
    ws = jnp.clip(r - NA_WIN_R // 2, 0, rows - NA_WIN_R)
    row_ok = (krow[:, None, :] >= ws[:, :, None]) & (krow[:, None, :] < ws[:, :, None] + NA_WIN_R)
    row_off = jnp.clip(krow[:, None, :] - r[:, :, None] + (NA_WIN_R - 1), 0, 2 * NA_WIN_R - 2)
    col_start = jnp.clip(c - NA_WIN_C // 2, 0, GRID_W - NA_WIN_C)
    col_ok = (c[None, :] >= col_start[:, None]) & (c[None, :] < col_start[:, None] + NA_WIN_C)
    col_off = jnp.clip(c[None, :] - c[:, None] + (NA_WIN_C - 1), 0, 2 * NA_WIN_C - 2)
    hp = lax.Precision.HIGHEST
    row_hot = (row_off[..., None] == jnp.arange(2 * NA_WIN_R - 1)).astype(F32)
    col_hot = (col_off[..., None] == jnp.arange(2 * NA_WIN_C - 1)).astype(F32)
    b = jnp.einsum('hab,crka->hcrkb', rpb.astype(F32), row_hot, precision=hp)
    b = jnp.einsum('hcrkb,qpb->hcrqkp', b, col_hot, precision=hp)
    ok = row_ok[:, :, None, :, None] & col_ok[None, None, :, None, :]
    b = jnp.where(ok[None], b, NEG_INF).transpose(1, 0, 2, 3, 4, 5)
    return b.reshape(3, rpb.shape[0], NA_TILE_ROWS * GRID_W, 2 * NA_TILE_ROWS * GRID_W).astype(BF16)


def _na_kernel(q_ref, k_ref, v_ref, kc_ref, vc_ref, b_ref, o_ref, *, scale):
    q = q_ref[...].astype(BF16)
    nk = k_ref.shape[0]
    k = jnp.concatenate([k_ref[...].astype(BF16), kc_ref[...].astype(BF16)], axis=0)
    v = jnp.concatenate([v_ref[...].astype(BF16), vc_ref[...].astype(BF16)], axis=0)
    s = lax.dot_general(q, k, (((1,), (1,)), ((), ())), preferred_element_type=F32) * scale
    bias = jnp.concatenate([b_ref[...].astype(F32), jnp.zeros((s.shape[0], s.shape[1] - nk), F32)], axis=1)
    s = s + bias
    m = jnp.max(s, axis=-1, keepdims=True)
    p = jnp.exp(s - m)
    den = jnp.sum(p, axis=-1, keepdims=True)
    o = jnp.dot(p.astype(BF16), v, preferred_element_type=F32)
    o_ref[...] = o / den


def na_latent(zl, zc, bias_tab, B, L, Lc):
    rows = L // GRID_W
    nt = rows // NA_TILE_ROWS
    tq = NA_TILE_ROWS * GRID_W
    tk = 2 * tq
    H, dh = NA_HEADS, NA_HEAD_DIM

    def case(j):
        return jnp.where(j == 0, 0, jnp.where(j == nt - 1, 2, 1))

    def kstart(b, j):
        ks = jnp.clip(j * NA_TILE_ROWS - NA_WIN_R // 2, 0, rows - 2 * NA_TILE_ROWS)
        return pl.multiple_of(b * L + ks * GRID_W, (NA_WIN_R // 2) * GRID_W)

    def kv_spec(col0):
        return pl.BlockSpec((pl.Element(tk), pl.Element(dh)),
                            lambda b, j, h: (kstart(b, j), pl.multiple_of((col0 + h) * dh, dh)))

    return pl.pallas_call(
        functools.partial(_na_kernel, scale=dh ** -0.5),
        grid=(B, nt, H),
        in_specs=[
            pl.BlockSpec((tq, dh), lambda b, j, h: (b * nt + j, h)),
            kv_spec(H),
            kv_spec(2 * H),
            pl.BlockSpec((Lc, dh), lambda b, j, h: (b, H + h)),
            pl.BlockSpec((Lc, dh), lambda b, j, h: (b, 2 * H + h)),
            pl.BlockSpec((None, None, tq, tk), lambda b, j, h: (case(j), h, 0, 0)),
        ],
        out_specs=pl.BlockSpec((tq, dh), lambda b, j, h: (b * nt + j, h)),
        out_shape=jax.ShapeDtypeStruct((B * L, H * dh), F32),
        compiler_params=pltpu.CompilerParams(dimension_semantics=("arbitrary",) * 3),
        name="na_latent",
    )(zl, zl, zl, zc, zc, bias_tab)


def na_context(zc, B, Lc):
    qc, kc, vc = [t.reshape(B, Lc, NA_HEADS, NA_HEAD_DIM) for t in jnp.split(zc[:, :3 * NA_WIDTH], 3, axis=-1)]
    return dense_attention(qc, kc, vc).reshape(B * Lc, NA_WIDTH)


def short_conv3(u, w, b):
    up = jnp.pad(u, ((0, 0), (1, 1), (0, 0)))
    return up[:, :-2] * w[0] + up[:, 1:-1] * w[1] + up[:, 2:] * w[2] + b


def hyena_filters(L, w1, b1, w2, b2, w3, freq):
    t = jnp.arange(L, dtype=F32)
    t01 = t / max(L - 1, 1)
    bands = jnp.linspace(1e-4, HY_BANDS - 1, HY_BANDS, dtype=F32)
    ang = (2.0 * math.pi / L) * t[:, None] * bands[None, :]
    z = jnp.concatenate([t01[:, None], jnp.cos(ang), -jnp.sin(ang)], axis=-1)
    f = freq.astype(F32)
    h = jnp.sin(f * (z @ w1.astype(F32) + b1.astype(F32)))
    h = jnp.sin(f * (h @ w2.astype(F32) + b2.astype(F32)))
    h = (h @ w3.astype(F32)).reshape(L, 2, HY_ORDER, HY_WIDTH)
    deltas = jnp.abs(jnp.linspace(math.log(HY_DECAY_TARGET) / HY_SLOW_DECAY, math.log(HY_DECAY_TARGET) / HY_FAST_DECAY,
                                  HY_WIDTH, dtype=F32))
    h = h * jnp.exp(-t01[:, None] * deltas[None, :])[:, None, None, :]
    kern = jnp.concatenate([h[:, 0], jnp.zeros((1, HY_ORDER, HY_WIDTH), F32), h[:0:-1, 1]], axis=0)
    return kern / jnp.sum(jnp.abs(kern), axis=0, keepdims=True)


HY_N1 = 64


def _cis(num, den):
    ang = (2.0 * math.pi / den) * (num % den).astype(F32)
    return jnp.cos(ang), jnp.sin(ang)


def hyena_dft_tables(L):
    N, N1 = 2 * L, HY_N1
    N2 = N // N1
    n1 = jnp.arange(N1)[:, None, None]
    k2 = jnp.arange(N2)[None, :, None]
    n2 = jnp.arange(N2 // 2)[None, None, :]
    c, s = _cis(n1 * k2 + n2 * k2 * N1, N)
    g_fwd = jnp.concatenate([c, -s], axis=1)
    g_inv = g_fwd.transpose(0, 2, 1) / N
    a = jnp.arange(N1)
    fc, fs = _cis(a[:, None] * a[None, :], N1)
    f_fwd = jnp.concatenate([jnp.concatenate([fc, fs], axis=1), jnp.concatenate([-fs, fc], axis=1)], axis=0)
    f_inv = jnp.concatenate([jnp.concatenate([fc, -fs], axis=1), jnp.concatenate([fs, fc], axis=1)], axis=0)
    return g_fwd.astype(BF16), f_fwd.astype(BF16), f_inv.astype(BF16), g_inv.astype(BF16)


def hyena_kernel_spectrum(kern):
    N, O, C = kern.shape
    N1 = HY_N1
    N2 = N // N1
    hp = lax.Precision.HIGHEST
    x = kern.reshape(N2, N1, O * C)
    k2 = jnp.arange(N2)
    c2, s2 = _cis(k2[:, None] * k2[None, :], N2)
    ar = jnp.einsum('kn,nqc->kqc', c2, x, precision=hp)
    ai = -jnp.einsum('kn,nqc->kqc', s2, x, precision=hp)
    ct, st = _cis(k2[:, None] * jnp.arange(N1)[None, :], N)
    br = ar * ct[:, :, None] + ai * st[:, :, None]
    bi = ai * ct[:, :, None] - ar * st[:, :, None]
    a = jnp.arange(N1)
    fc, fs = _cis(a[:, None] * a[None, :], N1)
    xr = jnp.einsum('pq,kqc->kpc', fc, br, precision=hp) + jnp.einsum('pq,kqc->kpc', fs, bi, precision=hp)
    xi = jnp.einsum('pq,kqc->kpc', fc, bi, precision=hp) - jnp.einsum('pq,kqc->kpc', fs, br, precision=hp)
    h = jnp.stack([xr, xi], axis=0).reshape(2, N2, N1, O, C)
    return h.transpose(3, 0, 1, 2, 4)


def _hy_p1_kernel(x_ref, g_ref, o_ref):
    n2 = g_ref.shape[1] // 2
    for i in range(x_ref.shape[1]):
        r = jnp.dot(g_ref[i], x_ref[:, i, :].astype(BF16), preferred_element_type=F32)
        o_ref[0, :, i, :] = r[:n2]
        o_ref[1, :, i, :] = r[n2:]


def _hy_p2_kernel(b_ref, h_ref, ff_ref, fi_ref, o_ref):
    n1 = b_ref.shape[1]
    b = b_ref[...].reshape(2 * n1, b_ref.shape[2]).astype(BF16)
    x = jnp.dot(ff_ref[...], b, preferred_element_type=F32)
    xr, xi = x[:n1], x[n1:]
    hr, hi = h_ref[0], h_ref[1]
    y = jnp.concatenate([xr * hr - xi * hi, xr * hi + xi * hr], axis=0).astype(BF16)
    c = jnp.dot(fi_ref[...], y, preferred_element_type=F32)
    o_ref[0] = c[:n1]
    o_ref[1] = c[n1:]


def _hy_p3_kernel(c_ref, g_ref, y_ref, x_ref, bias_ref, o_ref):
    n2 = c_ref.shape[1]
    for i in range(y_ref.shape[1]):
        c = c_ref[:, :, i, :].reshape(2 * n2, c_ref.shape[3]).astype(BF16)
        conv = jnp.dot(g_ref[i], c, preferred_element_type=F32)
        o_ref[:, i, :] = x_ref[:, i, :] * (conv + y_ref[:, i, :] * bias_ref[...])


def hyena_long_conv(y, y_col, xg, xg_col, hspec, bias, tables, B, L):
    g_fwd, f_fwd, f_inv, g_inv = tables
    N1 = HY_N1
    N2 = 2 * L // N1
    H2 = N2 // 2
    C = HY_WIDTH
    NB = V7X_SUBLANES
    params = pltpu.CompilerParams(dimension_semantics=("arbitrary", "arbitrary"),
                                  vmem_limit_bytes=V7X_VMEM_LIMIT_BYTES)
    y4 = y.reshape(B, H2, N1, y.shape[1])
    xg4 = xg.reshape(B, H2, N1, xg.shape[1])
    bsp = pl.pallas_call(
        _hy_p1_kernel,
        grid=(B, N1 // NB),
        in_specs=[pl.BlockSpec((None, H2, NB, C), lambda b, n: (b, 0, n, y_col)),
                  pl.BlockSpec((NB, 2 * N2, H2), lambda b, n: (n, 0, 0))],
        out_specs=pl.BlockSpec((None, 2, N2, NB, C), lambda b, n: (b, 0, 0, n, 0)),
        out_shape=jax.ShapeDtypeStruct((B, 2, N2, N1, C), F32),
        compiler_params=params, name="hy_p1",
    )(y4, g_fwd)
    csp = pl.pallas_call(
        _hy_p2_kernel,
        grid=(B, N2),
        in_specs=[pl.BlockSpec((None, 2, None, N1, C), lambda b, k: (b, 0, k, 0, 0)),
                  pl.BlockSpec((2, None, N1, C), lambda b, k: (0, k, 0, 0)),
                  pl.BlockSpec((2 * N1, 2 * N1), lambda b, k: (0, 0)),
                  pl.BlockSpec((2 * N1, 2 * N1), lambda b, k: (0, 0))],
        out_specs=pl.BlockSpec((None, 2, None, N1, C), lambda b, k: (b, 0, k, 0, 0)),
        out_shape=jax.ShapeDtypeStruct((B, 2, N2, N1, C), F32),
        compiler_params=params, name="hy_p2",
    )(bsp, hspec, f_fwd, f_inv)
    out = pl.pallas_call(
        _hy_p3_kernel,
        grid=(B, N1 // NB),
        in_specs=[pl.BlockSpec((None, 2, N2, NB, C), lambda b, n: (b, 0, 0, n, 0)),
                  pl.BlockSpec((NB, H2, 2 * N2), lambda b, n: (n, 0, 0)),
                  pl.BlockSpec((None, H2, NB, C), lambda b, n: (b, 0, n, y_col)),
                  pl.BlockSpec((None, H2, NB, C), lambda b, n: (b, 0, n, xg_col)),
                  pl.BlockSpec((1, C), lambda b, n: (0, 0))],
        out_specs=pl.BlockSpec((None, H2, NB, C), lambda b, n: (b, 0, n, 0)),
        out_shape=jax.ShapeDtypeStruct((B, H2, N1, C), F32),
        compiler_params=params, name="hy_p3",
    )(csp, g_inv, y4, xg4, bias.reshape(1, C))
    return out.reshape(B * L, C)


def hyena_mixer(z, conv_w, conv_b, w1, b1, w2, b2, w3, freq, bias, B, L):
    zc = short_conv3(z.astype(F32), conv_w.astype(F32), conv_b.astype(F32)).reshape(B * L, -1)
    kern = hyena_filters(L, w1, b1, w2, b2, w3, freq)
    tables = hyena_dft_tables(L)
    hspec = hyena_kernel_spectrum(kern)
    y, y_col = zc, 0
    for o in range(HY_ORDER):
        y = hyena_long_conv(y, y_col, zc, o + 1, hspec[o], bias[o].astype(F32), tables, B, L)
        y_col = 0
    return y


def rope_tables(L, rotary):
    if not rotary:
        return jnp.ones((L, GLA_DK), F32), jnp.zeros((L, GLA_DK), F32)
    t = jnp.arange(L)
    pos = jnp.stack([(t // GRID_W).astype(F32), (t % GRID_W).astype(F32)], axis=1)
    quarter = GLA_DK // 4
    inv = ROPE_BASE ** (-jnp.arange(quarter, dtype=F32) / quarter)
    ang = pos[:, :, None] * inv[None, None, :]
    cos = jnp.concatenate([jnp.cos(ang), jnp.cos(ang)], axis=-1).reshape(L, GLA_DK)
    sin = jnp.concatenate([-jnp.sin(ang), jnp.sin(ang)], axis=-1).reshape(L, GLA_DK)
    return cos, sin


def _gla_kernel(q_ref, k_ref, v_ref, g_ref, a_ref, cos_ref, sin_ref, w2_ref, b2_ref, ng_ref, s0_ref, o_ref, st_ref,
                *, n_chunks):
    C = GLA_CHUNK
    quarter = GLA_DK // 4
    lane = lax.broadcasted_iota(jnp.int32, (C, GLA_DK), 1)
    first_half = (lane % (2 * quarter)) < quarter
    row = lax.broadcasted_iota(jnp.int32, (C, C), 0)
    col = lax.broadcasted_iota(jnp.int32, (C, C), 1)
    tri = (row >= col, row <= col)
    o_ref[...] = jnp.zeros_like(o_ref)
    st_ref[...] = s0_ref[...]

    def rope(x, cos, sin):
        swapped = jnp.where(first_half, pltpu.roll(x, GLA_DK - quarter, 1), pltpu.roll(x, quarter, 1))
        return x * cos + swapped * sin

    def chunk(c, d):
        sl = pl.ds(pl.multiple_of(c * C, C), C)
        cos, sin = cos_ref[sl, :], sin_ref[sl, :]
        q = rope(q_ref[sl, :], cos, sin) * GLA_DK ** -0.5
        k = rope(k_ref[sl, :], cos, sin)
        v = v_ref[sl, :].astype(BF16)
        a = a_ref[sl, d * GLA_GATE_RANK:(d + 1) * GLA_GATE_RANK]
        pre = jnp.dot(a.astype(BF16), w2_ref[d].astype(BF16), preferred_element_type=F32) + b2_ref[d]
        log_a = jax.nn.log_sigmoid(pre) / GLA_TAU
        mask = tri[d]
        bcum = jnp.dot(mask.astype(F32), log_a, preferred_element_type=F32, precision=lax.Precision.HIGHEST)
        blast = jnp.sum(log_a, axis=0, keepdims=True)
        q_in = (q * jnp.exp(bcum)).astype(BF16)
        k_in = (k * jnp.exp(-bcum)).astype(BF16)
        k_st = (k * jnp.exp(blast - bcum)).astype(BF16)
        att = lax.dot_general(q_in, k_in, (((1,), (1,)), ((), ())), preferred_element_type=F32)
        att = jnp.where(mask, att, 0.0).astype(BF16)
        st = st_ref[d]
        o = jnp.dot(att, v, preferred_element_type=F32)
        o = o + lax.dot_general(q_in, st.astype(BF16), (((1,), (1,)), ((), ())), preferred_element_type=F32)
        o_ref[sl, :] += o
        kv_t = lax.dot_general(v, k_st, (((0,), (0,)), ((), ())), preferred_element_type=F32)
        st_ref[d] = st * jnp.exp(blast) + kv_t

    def body(i, carry):
        chunk(i, 0)
        chunk(n_chunks - 1 - i, 1)
        return carry

    lax.fori_loop(0, n_chunks, body, 0)
    o = o_ref[...]
    o = o * lax.rsqrt(jnp.mean(o * o, axis=-1, keepdims=True) + RMS_EPS) * ng_ref[...]
    o_ref[...] = o * jax.nn.silu(g_ref[...])


def gla_segment(z, za, s0, w2, b2, norm_g, B, L, rotary):
    H, dk, dv = GLA_HEADS, GLA_DK, GLA_DV
    cos, sin = rope_tables(L, rotary)
    qb, kb, vb, gb = IN_OFF[2] // dk, IN_OFF[3] // dk, IN_OFF[4] // dv, IN_OFF[5] // dv
    return pl.pallas_call(
        functools.partial(_gla_kernel, n_chunks=L // GLA_CHUNK),
        grid=(B, H),
        in_specs=[pl.BlockSpec((L, dk), lambda b, h: (b, qb + h)),
                  pl.BlockSpec((L, dk), lambda b, h: (b, kb + h)),
                  pl.BlockSpec((L, dv), lambda b, h: (b, vb + h)),
                  pl.BlockSpec((L, dv), lambda b, h: (b, gb + h)),
                  pl.BlockSpec((L, 2 * GLA_GATE_RANK), lambda b, h: (b, 0)),
                  pl.BlockSpec((L, dk), lambda b, h: (0, 0)),
                  pl.BlockSpec((L, dk), lambda b, h: (0, 0)),
                  pl.BlockSpec((2, GLA_GATE_RANK, dk), lambda b, h: (0, 0, h)),
                  pl.BlockSpec((2, 1, dk), lambda b, h: (0, 0, h)),
                  pl.BlockSpec((1, dv), lambda b, h: (0, 0)),
                  pl.BlockSpec((None, None, 2, dv, dk), lambda b, h: (b, h, 0, 0, 0))],
        out_specs=[pl.BlockSpec((L, dv), lambda b, h: (b, h)),
                   pl.BlockSpec((None, None, 2, dv, dk), lambda b, h: (b, h, 0, 0, 0))],
        out_shape=[jax.ShapeDtypeStruct((B * L, H * dv), F32), jax.ShapeDtypeStruct((B, H, 2, dv, dk), F32)],
        compiler_params=pltpu.CompilerParams(dimension_semantics=("arbitrary", "arbitrary"),
                                             vmem_limit_bytes=V7X_VMEM_LIMIT_BYTES),
        name="gla_segment",
    )(z, z, z, z, za, cos, sin, w2, b2.reshape(2, 1, -1), norm_g.reshape(1, dv), s0)


def gla_mixer(zl, zc, za_l, za_c, w2, b2, norm_g, B, L, Lc):
    s0 = jnp.zeros((B, GLA_HEADS, 2, GLA_DV, GLA_DK), F32)
    out_c, s_ctx = gla_segment(zc, za_c, s0, w2, b2, norm_g, B, Lc, False)
    out_l, _ = gla_segment(zl, za_l, s_ctx, w2, b2, norm_g, B, L, True)
    return out_l, out_c


def s5_discretise(a_re, a_im, log_dt, b_re, b_im):
    A = lax.complex(a_re.astype(F32), a_im.astype(F32))
    dt = jnp.exp(log_dt.astype(F32))[:, None]
    a_bar = jnp.exp(A * dt)
    b_bar = ((a_bar - 1.0) / A)[..., None] * lax.complex(b_re.astype(F32), b_im.astype(F32))
    return a_bar, b_bar


S5_T = 16
S5_GQ = V7X_LANES // S5_GROUP
S5_NQ = S5_GROUPS // S5_GQ
S5_NS = 2 * S5_GQ * S5_STATE


def s5_operators(a_re, a_im, log_dt, b_re, b_im, c_re, c_im):
    T, GQ, NQ, P, I = S5_T, S5_GQ, S5_NQ, S5_STATE, S5_GROUP
    eye = jnp.eye(GQ, dtype=F32)
    w_parts, v_parts, at_parts = [], [], []
    ktot = 0.0
    for d in range(2):
        a_bar, b_bar = s5_discretise(a_re[d], a_im[d], log_dt[d], b_re[d], b_im[d])
        c_mat = lax.complex(c_re[d].astype(F32), c_im[d].astype(F32))
        e = jnp.arange(T + 1, dtype=F32)
        apow = a_bar[None] ** e[:, None, None].astype(jnp.complex64)
        ex = (T - 1 - jnp.arange(T)) if d == 0 else jnp.arange(T)
        w = apow[ex][:, :, :, None] * b_bar[None]
        w = jnp.stack([w.real, w.imag], axis=0).reshape(2, T, NQ, GQ, P, I)
        w = jnp.einsum('rsqgpj,gh->qsgjrhp', w, eye)
        w_parts.append(w.reshape(NQ, T * GQ * I, S5_NS))
        k = jnp.einsum('gip,tgp,gpj->tgij', c_mat, apow[:T], b_bar).real
        t_idx = jnp.arange(T)[:, None]
        s_idx = jnp.arange(T)[None, :]
        lag = (t_idx - s_idx) if d == 0 else (s_idx - t_idx)
        ktot = ktot + jnp.where((lag >= 0)[:, :, None, None, None], k[jnp.clip(lag, 0, T - 1)], 0.0)
        ey = (jnp.arange(T) + 1) if d == 0 else (T - jnp.arange(T))
        v = c_mat[None] * apow[ey][:, :, None, :]
        v = jnp.stack([v.real, -v.imag], axis=0).reshape(2, T, NQ, GQ, I, P)
        v = jnp.einsum('rtqgip,gh->qrgpthi', v, eye)
        v_parts.append(v.reshape(NQ, S5_NS, T * GQ * I))
        at = apow[T].reshape(NQ, GQ * P)
        at_parts.append(jnp.concatenate([at.real, at.imag], axis=-1)[:, None, :])
    m = ktot.reshape(T, T, NQ, GQ, I, I)
    m = jnp.einsum('tsqgij,gh->qsgjthi', m, eye).reshape(NQ, T * GQ * I, T * GQ * I)
    rhs = jnp.concatenate([w_parts[0], w_parts[1], m], axis=-1).astype(BF16)
    return rhs, jnp.stack(v_parts).astype(BF16), jnp.stack(at_parts)


def _s5_in_kernel(u_ref, rhs_ref, s_ref, y_ref):
    r = jnp.dot(u_ref[...], rhs_ref[...], preferred_element_type=F32)
    s_ref[0] = r[:, :S5_NS]
    s_ref[1] = r[:, S5_NS:2 * S5_NS]
    y_ref[...] = r[:, 2 * S5_NS:]


def _s5_scan_kernel(s_ref, at_ref, x_ref, *, n_ctx, n_lat, batch):
    d = pl.program_id(0)
    half = S5_NS // 2
    a_r = at_ref[:, :half]
    a_i = at_ref[:, half:]

    def run(b, base, n, carry):
        def body(i, st):
            xr, xi = st
            c = i + d * (n - 1 - 2 * i)
            row = base + b * n + c
            x_ref[pl.ds(row, 1), :] = jnp.concatenate([xr, xi], axis=-1)
            s = s_ref[pl.ds(row, 1), :]
            return a_r * xr - a_i * xi + s[:, :half], a_r * xi + a_i * xr + s[:, half:]
        return lax.fori_loop(0, n, body, carry)

    for b in range(batch):
        zero = jnp.zeros((1, half), F32)
        st = run(b, 0, n_ctx, (zero, zero))
        run(b, batch * n_ctx, n_lat, st)


def _s5_out_kernel(x_ref, v_ref, yin_ref, u_ref, dsk_ref, y_ref):
    y = yin_ref[...] + dsk_ref[...] * u_ref[...]
    y = y + jnp.dot(x_ref[0].astype(BF16), v_ref[0], preferred_element_type=F32)
    y = y + jnp.dot(x_ref[1].astype(BF16), v_ref[1], preferred_element_type=F32)
    y_ref[...] = y


def s5_core(uc, ul, ops, d_skip, B, L, Lc):
    rhs, vmat, a_t = ops
    T, NQ, NS = S5_T, S5_NQ, S5_NS
    n_ctx, n_lat = Lc // T, L // T
    R = B * (n_ctx + n_lat)
    tw = T * V7X_LANES
    u = jnp.concatenate([uc.reshape(B * n_ctx, T, NQ, V7X_LANES), ul.reshape(B * n_lat, T, NQ, V7X_LANES)], axis=0)
    u = u.transpose(2, 0, 1, 3).reshape(NQ, R, tw)
    params = pltpu.CompilerParams(dimension_semantics=("arbitrary",), vmem_limit_bytes=V7X_VMEM_LIMIT_BYTES)
    s, y_in = pl.pallas_call(
        _s5_in_kernel,
        grid=(NQ,),
        in_specs=[pl.BlockSpec((None, R, tw), lambda q: (q, 0, 0)),
                  pl.BlockSpec((None, tw, 2 * NS + tw), lambda q: (q, 0, 0))],
        out_specs=[pl.BlockSpec((2, None, R, NS), lambda q: (0, q, 0, 0)),
                   pl.BlockSpec((None, R, tw), lambda q: (q, 0, 0))],
        out_shape=[jax.ShapeDtypeStruct((2, NQ, R, NS), F32), jax.ShapeDtypeStruct((NQ, R, tw), F32)],
        compiler_params=params,
        name="s5_in",
    )(u.astype(BF16), rhs)
    x = pl.pallas_call(
        functools.partial(_s5_scan_kernel, n_ctx=n_ctx, n_lat=n_lat, batch=B),
        grid=(2, NQ),
        in_specs=[pl.BlockSpec((None, None, R, NS), lambda d, q: (d, q, 0, 0)),
                  pl.BlockSpec((None, None, 1, NS), lambda d, q: (d, q, 0, 0))],
        out_specs=pl.BlockSpec((None, None, R, NS), lambda d, q: (d, q, 0, 0)),
        out_shape=jax.ShapeDtypeStruct((2, NQ, R, NS), F32),
        compiler_params=pltpu.CompilerParams(dimension_semantics=("arbitrary", "arbitrary")),
        name="s5_scan",
    )(s, a_t)
    dsk = jnp.tile(d_skip.astype(F32).reshape(NQ, 1, V7X_LANES), (1, 1, T))
    y = pl.pallas_call(
        _s5_out_kernel,
        grid=(NQ,),
        in_specs=[pl.BlockSpec((2, None, R, NS), lambda q: (0, q, 0, 0)),
                  pl.BlockSpec((2, None, NS, tw), lambda q: (0, q, 0, 0)),
                  pl.BlockSpec((None, R, tw), lambda q: (q, 0, 0)),
                  pl.BlockSpec((None, R, tw), lambda q: (q, 0, 0)),
                  pl.BlockSpec((None, 1, tw), lambda q: (q, 0, 0))],
        out_specs=pl.BlockSpec((None, R, tw), lambda q: (q, 0, 0)),
        out_shape=jax.ShapeDtypeStruct((NQ, R, tw), F32),
        compiler_params=params,
        name="s5_out",
    )(x, vmat, y_in, u, dsk)
    y = y.reshape(NQ, R, T, V7X_LANES).transpose(1, 2, 0, 3).reshape(R * T, S5_WIDTH)
    return y[:B * Lc], y[B * Lc:]


def s5_glu(y, w, b):
    y = jax.nn.gelu(y)
    return y * jax.nn.sigmoid(mm(y, w) + b.astype(F32))


def merge_branches(h, branches, w_branch, w_gate, b_gate, w_out):
    merged = jnp.zeros(h.shape, h.dtype)
    for i, br in enumerate(branches):
        gate = jax.nn.sigmoid(mm_nd(h, w_gate[i]) + b_gate[i])
        merged = merged + gate * mm_nd(br.astype(h.dtype), w_branch[i])
    return mm_nd(merged, w_out)


def _moe_gather_kernel(idx_ref, h_ref, o_ref, stage_ref, sem, *, n_tokens, n_experts):
    g = pl.program_id(0)
    cap = o_ref.shape[0]
    b = g // n_experts

    def row_copy(c):
        row = b * n_tokens + idx_ref[g * cap + c]
        return pltpu.make_async_copy(h_ref.at[pl.ds(row, 1), :], stage_ref.at[pl.ds(c, 1), :], sem)

    def start(c, carry):
        row_copy(c).start()
        return carry

    def wait(c, carry):
        row_copy(c).wait()
        return carry

    lax.fori_loop(0, cap, start, 0)
    lax.fori_loop(0, cap, wait, 0)
    o_ref[...] = stage_ref[...].astype(BF16)


def moe_gather(h, idx):
    B, N, D = h.shape
    _, E, cap = idx.shape
    return pl.pallas_call(
        functools.partial(_moe_gather_kernel, n_tokens=N, n_experts=E),
        grid_spec=pltpu.PrefetchScalarGridSpec(
            num_scalar_prefetch=1,
            grid=(B * E,),
            in_specs=[pl.BlockSpec(memory_space=pl.ANY)],
            out_specs=pl.BlockSpec((None, cap, D), lambda g, idx_ref: (g, 0, 0)),
            scratch_shapes=[pltpu.VMEM((cap, D), F32), pltpu.SemaphoreType.DMA(())],
        ),
        out_shape=jax.ShapeDtypeStruct((B * E, cap, D), BF16),
        compiler_params=pltpu.CompilerParams(dimension_semantics=("arbitrary",),
                                             vmem_limit_bytes=V7X_VMEM_LIMIT_BYTES),
        name="moe_gather",
    )(idx.reshape(-1).astype(jnp.int32), h.reshape(B * N, D)).reshape(B, E, cap, D)


def _expert_gu_kernel(x_ref, wg_ref, wu_ref, o_ref):
    x = x_ref[...]
    g = jnp.dot(x, wg_ref[...].astype(BF16), preferred_element_type=F32)
    u = jnp.dot(x, wu_ref[...].astype(BF16), preferred_element_type=F32)
    o_ref[...] = (jax.nn.silu(g) * u).astype(BF16)


def expert_gu(xs, w_gu):
    B, E, C, D = xs.shape
    F = w_gu.shape[2] // 2
    tn = 512
    nj = F // tn
    return pl.pallas_call(
        _expert_gu_kernel,
        grid=(E, nj, B),
        in_specs=[pl.BlockSpec((None, None, C, D), lambda e, j, b: (b, e, 0, 0)),
                  pl.BlockSpec((None, D, tn), lambda e, j, b: (e, 0, j)),
                  pl.BlockSpec((None, D, tn), lambda e, j, b: (e, 0, nj + j))],
        out_specs=pl.BlockSpec((None, None, C, tn), lambda e, j, b: (b, e, 0, j)),
        out_shape=jax.ShapeDtypeStruct((B, E, C, F), BF16),
        compiler_params=pltpu.CompilerParams(dimension_semantics=("arbitrary",) * 3,
                                             vmem_limit_bytes=V7X_VMEM_LIMIT_BYTES),
        name="expert_gu",
    )(xs, w_gu, w_gu)


def _expert_down_kernel(a_ref, w_ref, gate_ref, o_ref):
    y = jnp.dot(a_ref[...], w_ref[...].astype(BF16), preferred_element_type=F32)
    o_ref[...] = y * gate_ref[...]


def expert_down(act, w_down, gate):
    B, E, C, F = act.shape
    D = w_down.shape[2]
    tn = min(1024, D)
    assert D % tn == 0
    return pl.pallas_call(
        _expert_down_kernel,
        grid=(E, D // tn, B),
        in_specs=[pl.BlockSpec((None, None, C, F), lambda e, j, b: (b, e, 0, 0)),
                  pl.BlockSpec((None, F, tn), lambda e, j, b: (e, 0, j)),
                  pl.BlockSpec((None, None, C, 1), lambda e, j, b: (b, e, 0, 0))],
        out_specs=pl.BlockSpec((None, None, C, tn), lambda e, j, b: (b, e, 0, j)),
        out_shape=jax.ShapeDtypeStruct((B, E, C, D), F32),
        compiler_params=pltpu.CompilerParams(dimension_semantics=("arbitrary",) * 3,
                                             vmem_limit_bytes=V7X_VMEM_LIMIT_BYTES),
        name="expert_down",
    )(act, w_down, gate[..., None])


def moe_ec(h, router_w, w_gu, w_down):
    B, N, D = h.shape
    cap = max(1, EC_CAPACITY * N // N_EXPERTS)
    logits = jnp.einsum('bnd,de->bne', h, router_w, precision=lax.Precision.HIGHEST)
    aff = jax.nn.softmax(logits.astype(F32), axis=-1)
    gate, idx = lax.top_k(jnp.swapaxes(aff, 1, 2), cap)
    y = expert_down(expert_gu(moe_gather(h, idx), w_gu), w_down, gate)
    return jax.vmap(lambda ib, yb: jnp.zeros((N, D), yb.dtype).at[ib.reshape(-1)].add(yb.reshape(-1, D)))(idx, y)


def kernel(x, c, ctx, c_ctx, ada_w, ada_b, mix_pre_g, mix_post_g, ffn_pre_g, ffn_post_g, w_in, na_rpb,
           hy_conv_w, hy_conv_b, hy_w1, hy_b1, hy_w2, hy_b2, hy_w3, hy_freq, hy_bias,
           gla_w2, gla_b2, gla_norm_g, s5_a_re, s5_a_im, s5_log_dt, s5_b_re, s5_b_im, s5_c_re, s5_c_im,
           s5_d, s5_glu_w, s5_glu_b, w_branch, w_gate, b_gate, w_out, router_w, ex_w_gu, ex_w_down):
    xl, xc = x, ctx
    B, L, Lc = x.shape[0], x.shape[1], ctx.shape[1]
    for l in range(DEPTH):
        need_ctx = l < DEPTH - 1
        cvecs = jnp.concatenate([c, c_ctx[None, :]], axis=0)
        mod = mm(jax.nn.silu(cvecs), ada_w[l]) + ada_b[l]
        ml = [p[:, None, :] for p in jnp.split(mod[:B], 6, axis=-1)]
        mc = jnp.split(mod[B], 6, axis=-1)

        hl = rms_norm(xl, mix_pre_g[l]) * (1.0 + ml[1]) + ml[0]
        hc = rms_norm(xc, mix_pre_g[l]) * (1.0 + mc[1]) + mc[0]
        hl2, hc2 = hl.reshape(B * L, D_MODEL), hc.reshape(B * Lc, D_MODEL)
        w_a, w_s5 = w_in[l][:, IN_MAIN:IN_MAIN + 2 * GLA_GATE_RANK], w_in[l][:, IN_MAIN + 2 * GLA_GATE_RANK:]
        zl, zc = mm(hl2, w_in[l], IN_MAIN), mm(hc2, w_in[l], IN_MAIN)
        zl3, zc3 = zl.reshape(B, L, IN_MAIN), zc.reshape(B, Lc, IN_MAIN)

        na_l = na_latent(zl, zc, na_bias_table(na_rpb[l], L // GRID_W), B, L, Lc).reshape(B, L, NA_WIDTH)
        hy_args = (hy_conv_w[l], hy_conv_b[l], hy_w1[l], hy_b1[l], hy_w2[l], hy_b2[l], hy_w3[l], hy_freq[l], hy_bias[l])
        hy_l = hyena_mixer(zl3[..., IN_OFF[1]:IN_OFF[2]], *hy_args, B, L).reshape(B, L, HY_WIDTH)
        gla_l, gla_c = gla_mixer(zl, zc, mm(hl2, w_a), mm(hc2, w_a), gla_w2[l], gla_b2[l], gla_norm_g[l], B, L, Lc)
        gla_l, gla_c = gla_l.reshape(B, L, GLA_V_WIDTH), gla_c.reshape(B, Lc, GLA_V_WIDTH)
        s5_ops = s5_operators(s5_a_re[l], s5_a_im[l], s5_log_dt[l], s5_b_re[l], s5_b_im[l], s5_c_re[l], s5_c_im[l])
        s5_yc, s5_yl = s5_core(mm(hc2, w_s5), mm(hl2, w_s5), s5_ops, s5_d[l], B, L, Lc)
        s5_l = s5_glu(s5_yl, s5_glu_w[l], s5_glu_b[l]).reshape(B, L, S5_WIDTH)
        yl = merge_branches(hl, (na_l, hy_l, gla_l, s5_l), w_branch[l], w_gate[l], b_gate[l], w_out[l])
        xl = xl + ml[2] * rms_norm(yl, mix_post_g[l])
        if need_ctx:
            na_c = na_context(zc, B, Lc).reshape(B, Lc, NA_WIDTH)
            hy_c = hyena_mixer(zc3[..., IN_OFF[1]:IN_OFF[2]], *hy_args, B, Lc).reshape(B, Lc, HY_WIDTH)
            s5_c = s5_glu(s5_yc, s5_glu_w[l], s5_glu_b[l]).reshape(B, Lc, S5_WIDTH)
            yc = merge_branches(hc, (na_c, hy_c, gla_c, s5_c), w_branch[l], w_gate[l], b_gate[l], w_out[l])
            xc = xc + mc[2] * rms_norm(yc, mix_post_g[l])

        hl = rms_norm(xl, ffn_pre_g[l]) * (1.0 + ml[4]) + ml[3]
        xl = xl + ml[5] * rms_norm(moe_ec(hl, router_w[l], ex_w_gu[l], ex_w_down[l]), ffn_post_g[l])
        if need_ctx:
            hc = rms_norm(xc, ffn_pre_g[l]) * (1.0 + mc[4]) + mc[3]
            xc = xc + mc[5] * rms_norm(moe_ec(hc, router_w[l], ex_w_gu[l], ex_w_down[l]), ffn_post_g[l])
    return xl
```

```python
import functools
import math

import jax
import jax.numpy as jnp
import numpy as np
from jax import lax
from jax.experimental import pallas as pl
from jax.experimental.pallas import tpu as pltpu

D_MODEL = 4096
BATCH = 2
SEQ = 4096
DEPTH = 2
CTX_LEN = 256
GRID_W = 64

NA_HEADS = 8
NA_HEAD_DIM = 128
NA_WIDTH = NA_HEADS * NA_HEAD_DIM
NA_WIN_R = 8
NA_WIN_C = 16

HY_WIDTH = 1024
HY_ORDER = 2
HY_BANDS = 16
HY_POS_DIM = 1 + 2 * HY_BANDS
HY_FILTER_HIDDEN = 64
HY_DECAY_TARGET = 1e-2
HY_FAST_DECAY = 0.3
HY_SLOW_DECAY = 1.5

GLA_HEADS = 4
GLA_DK = 128
GLA_DV = 256
GLA_QK_WIDTH = GLA_HEADS * GLA_DK
GLA_V_WIDTH = GLA_HEADS * GLA_DV
GLA_GATE_RANK = 16
GLA_TAU = 16.0
GLA_CHUNK = 64

S5_WIDTH = 1024
S5_GROUP = 16
S5_GROUPS = S5_WIDTH // S5_GROUP
S5_STATE = 64

N_BRANCH = 4
BRANCH_WIDTH = 1024
N_EXPERTS = 16
EXPERT_FF = 1024
EC_CAPACITY = 2

ROPE_BASE = 10000.0
RMS_EPS = 1e-6
NEG_INF = -1e30
F32 = jnp.float32
BF16 = jnp.bfloat16

IN_SPLITS = (3 * NA_WIDTH, (HY_ORDER + 1) * HY_WIDTH, GLA_QK_WIDTH, GLA_QK_WIDTH, GLA_V_WIDTH, GLA_V_WIDTH,
             2 * GLA_GATE_RANK, S5_WIDTH)
IN_COLS = sum(IN_SPLITS)
IN_OFF = tuple(int(v) for v in np.concatenate([[0], np.cumsum(IN_SPLITS)]))
IN_MAIN = IN_OFF[6]

V7X_LANES = 128
V7X_SUBLANES = 8
V7X_VMEM_LIMIT_BYTES = 56 * 1024 * 1024


def _mm_kernel(x_ref, w_ref, o_ref, wb_ref):
    @pl.when(pl.program_id(1) == 0)
    def _():
        wb_ref[...] = w_ref[...].astype(BF16)

    o_ref[...] = jnp.dot(x_ref[...].astype(BF16), wb_ref[...], preferred_element_type=F32).astype(o_ref.dtype)


def _mm_tiles(M, K, N):
    tm = M if M <= 1024 else 1024
    tn = 512 if K > 1024 else 1024
    tn = min(tn, N)
    return tm, tn


def mm(x, w, lead=(), ncols=None, out_dtype=F32):
    M, K = x.shape
    nw = w.shape[-1]
    N = nw if ncols is None else ncols
    pad = (-M) % V7X_SUBLANES
    if pad:
        x = jnp.pad(x, ((0, pad), (0, 0)))
    Mp = M + pad
    tm, tn = _mm_tiles(Mp, K, N)
    assert Mp % tm == 0 and (N == nw or N % tn == 0) and w.shape[-2] == K and w.ndim == len(lead) + 2
    out = pl.pallas_call(
        _mm_kernel,
        grid=(pl.cdiv(N, tn), Mp // tm),
        in_specs=[pl.BlockSpec((tm, K), lambda j, i: (i, 0)),
                  pl.BlockSpec((None,) * len(lead) + (K, tn), lambda j, i: tuple(lead) + (0, j))],
        out_specs=pl.BlockSpec((tm, tn), lambda j, i: (i, j)),
        out_shape=jax.ShapeDtypeStruct((Mp, N), out_dtype),
        scratch_shapes=[pltpu.VMEM((K, tn), BF16)],
        compiler_params=pltpu.CompilerParams(dimension_semantics=("arbitrary", "arbitrary"),
                                             vmem_limit_bytes=V7X_VMEM_LIMIT_BYTES),
        name="mm",
    )(x, w)
    return out[:M] if pad else out


def rms_norm(x, g):
    xf = x.astype(F32)
    y = xf * lax.rsqrt(jnp.mean(xf * xf, axis=-1, keepdims=True) + RMS_EPS)
    return (y * g.astype(F32)).astype(x.dtype)


def dense_attention(q, k, v):
    scale = q.shape[-1] ** -0.5
    s = jnp.einsum('bqhd,bkhd->bhqk', q, k, preferred_element_type=F32) * scale
    p = jax.nn.softmax(s, axis=-1).astype(v.dtype)
    o = jnp.einsum('bhqk,bkhd->bqhd', p, v)
    return o.reshape(q.shape[0], q.shape[1], -1)


NA_TILE_ROWS = 8


def na_bias_table(rpb, rows):
    nt = rows // NA_TILE_ROWS
    cases = jnp.array([0, min(1, nt - 1), nt - 1])
    rq = jnp.arange(NA_TILE_ROWS)
    rk = jnp.arange(2 * NA_TILE_ROWS)
    c = jnp.arange(GRID_W)
    r = cases[:, None] * NA_TILE_ROWS + rq[None, :]
    ks = jnp.clip(cases * NA_TILE_ROWS - NA_WIN_R // 2, 0, rows - 2 * NA_TILE_ROWS)
    krow = ks[:, None] + rk[None, :]
    ws = jnp.clip(r - NA_WIN_R // 2, 0, rows - NA_WIN_R)
    row_ok = (krow[:, None, :] >= ws[:, :, None]) & (krow[:, None, :] < ws[:, :, None] + NA_WIN_R)
    row_off = jnp.clip(krow[:, None, :] - r[:, :, None] + (NA_WIN_R - 1), 0, 2 * NA_WIN_R - 2)
    col_start = jnp.clip(c - NA_WIN_C // 2, 0, GRID_W - NA_WIN_C)
    col_ok = (c[None, :] >= col_start[:, None]) & (c[None, :] < col_start[:, None] + NA_WIN_C)
    col_off = jnp.clip(c[None, :] - c[:, None] + (NA_WIN_C - 1), 0, 2 * NA_WIN_C - 2)
    hp = lax.Precision.HIGHEST
    row_hot = (row_off[..., None] == jnp.arange(2 * NA_WIN_R - 1)).astype(F32)
    col_hot = (col_off[..., None] == jnp.arange(2 * NA_WIN_C - 1)).astype(F32)
    b = jnp.einsum('hab,crka->hcrkb', rpb.astype(F32), row_hot, precision=hp)
    b = jnp.einsum('hcrkb,qpb->hcrqkp', b, col_hot, precision=hp)
    ok = row_ok[:, :, None, :, None] & col_ok[None, None, :, None, :]
    b = jnp.where(ok[None], b, NEG_INF).transpose(1, 0, 2, 3, 4, 5)
    return b.reshape(3, rpb.shape[0], NA_TILE_ROWS * GRID_W, 2 * NA_TILE_ROWS * GRID_W).astype(BF16)


def _na_kernel(q_ref, k_ref, v_ref, kc_ref, vc_ref, b_ref, o_ref, *, scale):
    q = q_ref[...].astype(BF16)
    nk = k_ref.shape[0]
    k = jnp.concatenate([k_ref[...].astype(BF16), kc_ref[...].astype(BF16)], axis=0)
    v = jnp.concatenate([v_ref[...].astype(BF16), vc_ref[...].astype(BF16)], axis=0)
    s = lax.dot_general(q, k, (((1,), (1,)), ((), ())), preferred_element_type=F32) * scale
    bias = jnp.concatenate([b_ref[...].astype(F32), jnp.zeros((s.shape[0], s.shape[1] - nk), F32)], axis=1)
    s = s + bias
    m = jnp.max(s, axis=-1, keepdims=True)
    p = jnp.exp(s - m)
    den = jnp.sum(p, axis=-1, keepdims=True)
    o = jnp.dot(p.astype(BF16), v, preferred_element_type=F32)
    o_ref[...] = (o / den).astype(o_ref.dtype)


def na_latent(zl, zc, bias_tab, B, L, Lc):
    rows = L // GRID_W
    nt = rows // NA_TILE_ROWS
    tq = NA_TILE_ROWS * GRID_W
    tk = 2 * tq
    H, dh = NA_HEADS, NA_HEAD_DIM

    def case(j):
        return jnp.where(j == 0, 0, jnp.where(j == nt - 1, 2, 1))

    def kstart(b, j):
        ks = jnp.clip(j * NA_TILE_ROWS - NA_WIN_R // 2, 0, rows - 2 * NA_TILE_ROWS)
        return pl.multiple_of(b * L + ks * GRID_W, (NA_WIN_R // 2) * GRID_W)

    def kv_spec(col0):
        return pl.BlockSpec((pl.Element(tk), pl.Element(dh)),
                            lambda b, j, h: (kstart(b, j), pl.multiple_of((col0 + h) * dh, dh)))

    return pl.pallas_call(
        functools.partial(_na_kernel, scale=dh ** -0.5),
        grid=(B, nt, H),
        in_specs=[
            pl.BlockSpec((tq, dh), lambda b, j, h: (b * nt + j, h)),
            kv_spec(H),
            kv_spec(2 * H),
            pl.BlockSpec((Lc, dh), lambda b, j, h: (b, H + h)),
            pl.BlockSpec((Lc, dh), lambda b, j, h: (b, 2 * H + h)),
            pl.BlockSpec((None, None, tq, tk), lambda b, j, h: (case(j), h, 0, 0)),
        ],
        out_specs=pl.BlockSpec((tq, dh), lambda b, j, h: (b * nt + j, h)),
        out_shape=jax.ShapeDtypeStruct((B * L, H * dh), BF16),
        compiler_params=pltpu.CompilerParams(dimension_semantics=("arbitrary",) * 3),
        name="na_latent",
    )(zl, zl, zl, zc, zc, bias_tab)


def na_context(zc, B, Lc):
    qc, kc, vc = [t.reshape(B, Lc, NA_HEADS, NA_HEAD_DIM) for t in jnp.split(zc[:, :3 * NA_WIDTH], 3, axis=-1)]
    return dense_attention(qc, kc, vc).reshape(B * Lc, NA_WIDTH)


def short_conv3(u, w, b):
    up = jnp.pad(u, ((0, 0), (1, 1), (0, 0)))
    return up[:, :-2] * w[0] + up[:, 1:-1] * w[1] + up[:, 2:] * w[2] + b


def hyena_filters(L, w1, b1, w2, b2, w3, freq):
    t = jnp.arange(L, dtype=F32)
    t01 = t / max(L - 1, 1)
    bands = jnp.linspace(1e-4, HY_BANDS - 1, HY_BANDS, dtype=F32)
    ang = (2.0 * math.pi / L) * t[:, None] * bands[None, :]
    z = jnp.concatenate([t01[:, None], jnp.cos(ang), -jnp.sin(ang)], axis=-1)
    f = freq.astype(F32)
    h = jnp.sin(f * (z @ w1.astype(F32) + b1.astype(F32)))
    h = jnp.sin(f * (h @ w2.astype(F32) + b2.astype(F32)))
    h = (h @ w3.astype(F32)).reshape(L, 2, HY_ORDER, HY_WIDTH)
    deltas = jnp.abs(jnp.linspace(math.log(HY_DECAY_TARGET) / HY_SLOW_DECAY, math.log(HY_DECAY_TARGET) / HY_FAST_DECAY,
                                  HY_WIDTH, dtype=F32))
    h = h * jnp.exp(-t01[:, None] * deltas[None, :])[:, None, None, :]
    kern = jnp.concatenate([h[:, 0], jnp.zeros((1, HY_ORDER, HY_WIDTH), F32), h[:0:-1, 1]], axis=0)
    return kern / jnp.sum(jnp.abs(kern), axis=0, keepdims=True)


HY_N1 = 64


def _cis(num, den):
    ang = (2.0 * math.pi / den) * (num % den).astype(F32)
    return jnp.cos(ang), jnp.sin(ang)


def hyena_dft_tables(L):
    N, N1 = 2 * L, HY_N1
    N2 = N // N1
    n1 = jnp.arange(N1)[:, None, None]
    k2 = jnp.arange(N2)[None, :, None]
    n2 = jnp.arange(N2 // 2)[None, None, :]
    c, s = _cis(n1 * k2 + n2 * k2 * N1, N)
    g_fwd = jnp.concatenate([c, -s], axis=1)
    g_inv = g_fwd.transpose(0, 2, 1) / N
    a = jnp.arange(N1)
    fc, fs = _cis(a[:, None] * a[None, :], N1)
    f_fwd = jnp.concatenate([jnp.concatenate([fc, fs], axis=1), jnp.concatenate([-fs, fc], axis=1)], axis=0)
    f_inv = jnp.concatenate([jnp.concatenate([fc, -fs], axis=1), jnp.concatenate([fs, fc], axis=1)], axis=0)
    return g_fwd.astype(BF16), f_fwd.astype(BF16), f_inv.astype(BF16), g_inv.astype(BF16)


def hyena_kernel_spectrum(kern):
    N, O, C = kern.shape
    N1 = HY_N1
    N2 = N // N1
    hp = lax.Precision.HIGHEST
    x = kern.reshape(N2, N1, O * C)
    k2 = jnp.arange(N2)
    c2, s2 = _cis(k2[:, None] * k2[None, :], N2)
    ar = jnp.einsum('kn,nqc->kqc', c2, x, precision=hp)
    ai = -jnp.einsum('kn,nqc->kqc', s2, x, precision=hp)
    ct, st = _cis(k2[:, None] * jnp.arange(N1)[None, :], N)
    br = ar * ct[:, :, None] + ai * st[:, :, None]
    bi = ai * ct[:, :, None] - ar * st[:, :, None]
    a = jnp.arange(N1)
    fc, fs = _cis(a[:, None] * a[None, :], N1)
    xr = jnp.einsum('pq,kqc->kpc', fc, br, precision=hp) + jnp.einsum('pq,kqc->kpc', fs, bi, precision=hp)
    xi = jnp.einsum('pq,kqc->kpc', fc, bi, precision=hp) - jnp.einsum('pq,kqc->kpc', fs, br, precision=hp)
    h = jnp.stack([xr, xi], axis=0).reshape(2, N2, N1, O, C)
    return h.transpose(3, 0, 1, 2, 4)


def _hy_p1_kernel(x_ref, g_ref, o_ref):
    n2 = g_ref.shape[1] // 2
    for i in range(x_ref.shape[1]):
        r = jnp.dot(g_ref[i], x_ref[:, i, :].astype(BF16), preferred_element_type=F32)
        o_ref[0, :, i, :] = r[:n2]
        o_ref[1, :, i, :] = r[n2:]


def _hy_p2_kernel(b_ref, h_ref, ff_ref, fi_ref, o_ref):
    n1 = b_ref.shape[1]
    b = b_ref[...].reshape(2 * n1, b_ref.shape[2]).astype(BF16)
    x = jnp.dot(ff_ref[...], b, preferred_element_type=F32)
    xr, xi = x[:n1], x[n1:]
    hr, hi = h_ref[0], h_ref[1]
    y = jnp.concatenate([xr * hr - xi * hi, xr * hi + xi * hr], axis=0).astype(BF16)
    c = jnp.dot(fi_ref[...], y, preferred_element_type=F32)
    o_ref[0] = c[:n1]
    o_ref[1] = c[n1:]


def _hy_p3_kernel(c_ref, g_ref, y_ref, x_ref, bias_ref, o_ref):
    n2 = c_ref.shape[1]
    for i in range(y_ref.shape[1]):
        c = c_ref[:, :, i, :].reshape(2 * n2, c_ref.shape[3]).astype(BF16)
        conv = jnp.dot(g_ref[i], c, preferred_element_type=F32)
        o_ref[:, i, :] = x_ref[:, i, :] * (conv + y_ref[:, i, :] * bias_ref[...])


def hyena_long_conv(y, y_col, xg, xg_col, hspec, bias, tables, B, L):
    g_fwd, f_fwd, f_inv, g_inv = tables
    N1 = HY_N1
    N2 = 2 * L // N1
    H2 = N2 // 2
    C = HY_WIDTH
    NB = V7X_SUBLANES
    params = pltpu.CompilerParams(dimension_semantics=("arbitrary", "arbitrary"),
                                  vmem_limit_bytes=V7X_VMEM_LIMIT_BYTES)
    y4 = y.reshape(B, H2, N1, y.shape[1])
    xg4 = xg.reshape(B, H2, N1, xg.shape[1])
    bsp = pl.pallas_call(
        _hy_p1_kernel,
        grid=(B, N1 // NB),
        in_specs=[pl.BlockSpec((None, H2, NB, C), lambda b, n: (b, 0, n, y_col)),
                  pl.BlockSpec((NB, 2 * N2, H2), lambda b, n: (n, 0, 0))],
        out_specs=pl.BlockSpec((None, 2, N2, NB, C), lambda b, n: (b, 0, 0, n, 0)),
        out_shape=jax.ShapeDtypeStruct((B, 2, N2, N1, C), F32),
        compiler_params=params, name="hy_p1",
    )(y4, g_fwd)
    csp = pl.pallas_call(
        _hy_p2_kernel,
        grid=(B, N2),
        in_specs=[pl.BlockSpec((None, 2, None, N1, C), lambda b, k: (b, 0, k, 0, 0)),
                  pl.BlockSpec((2, None, N1, C), lambda b, k: (0, k, 0, 0)),
                  pl.BlockSpec((2 * N1, 2 * N1), lambda b, k: (0, 0)),
                  pl.BlockSpec((2 * N1, 2 * N1), lambda b, k: (0, 0))],
        out_specs=pl.BlockSpec((None, 2, None, N1, C), lambda b, k: (b, 0, k, 0, 0)),
        out_shape=jax.ShapeDtypeStruct((B, 2, N2, N1, C), F32),
        compiler_params=params, name="hy_p2",
    )(bsp, hspec, f_fwd, f_inv)
    out = pl.pallas_call(
        _hy_p3_kernel,
        grid=(B, N1 // NB),
        in_specs=[pl.BlockSpec((None, 2, N2, NB, C), lambda b, n: (b, 0, 0, n, 0)),
                  pl.BlockSpec((NB, H2, 2 * N2), lambda b, n: (n, 0, 0)),
                  pl.BlockSpec((None, H2, NB, C), lambda b, n: (b, 0, n, y_col)),
                  pl.BlockSpec((None, H2, NB, C), lambda b, n: (b, 0, n, xg_col)),
                  pl.BlockSpec((1, C), lambda b, n: (0, 0))],
        out_specs=pl.BlockSpec((None, H2, NB, C), lambda b, n: (b, 0, n, 0)),
        out_shape=jax.ShapeDtypeStruct((B, H2, N1, C), F32),
        compiler_params=params, name="hy_p3",
    )(csp, g_inv, y4, xg4, bias.reshape(1, C))
    return out.reshape(B * L, C)


def hyena_mixer(z, conv_w, conv_b, w1, b1, w2, b2, w3, freq, bias, B, L):
    zc = short_conv3(z.astype(F32), conv_w.astype(F32), conv_b.astype(F32)).reshape(B * L, -1)
    kern = hyena_filters(L, w1, b1, w2, b2, w3, freq)
    tables = hyena_dft_tables(L)
    hspec = hyena_kernel_spectrum(kern)
    y, y_col = zc, 0
    for o in range(HY_ORDER):
        y = hyena_long_conv(y, y_col, zc, o + 1, hspec[o], bias[o].astype(F32), tables, B, L)
        y_col = 0
    return y


def rope_tables(L, rotary):
    if not rotary:
        return jnp.ones((L, GLA_DK), F32), jnp.zeros((L, GLA_DK), F32)
    t = jnp.arange(L)
    pos = jnp.stack([(t // GRID_W).astype(F32), (t % GRID_W).astype(F32)], axis=1)
    quarter = GLA_DK // 4
    inv = ROPE_BASE ** (-jnp.arange(quarter, dtype=F32) / quarter)
    ang = pos[:, :, None] * inv[None, None, :]
    cos = jnp.concatenate([jnp.cos(ang), jnp.cos(ang)], axis=-1).reshape(L, GLA_DK)
    sin = jnp.concatenate([-jnp.sin(ang), jnp.sin(ang)], axis=-1).reshape(L, GLA_DK)
    return cos, sin


def _gla_kernel(q_ref, k_ref, v_ref, g_ref, a_ref, cos_ref, sin_ref, w2_ref, b2_ref, ng_ref, s0_ref, o_ref, st_ref,
                *, n_chunks):
    C = GLA_CHUNK
    quarter = GLA_DK // 4
    lane = lax.broadcasted_iota(jnp.int32, (C, GLA_DK), 1)
    first_half = (lane % (2 * quarter)) < quarter
    row = lax.broadcasted_iota(jnp.int32, (C, C), 0)
    col = lax.broadcasted_iota(jnp.int32, (C, C), 1)
    tri = (row >= col, row <= col)
    o_ref[...] = jnp.zeros_like(o_ref)
    st_ref[...] = s0_ref[...]

    def rope(x, cos, sin):
        swapped = jnp.where(first_half, pltpu.roll(x, GLA_DK - quarter, 1), pltpu.roll(x, quarter, 1))
        return x * cos + swapped * sin

    def chunk(c, d):
        sl = pl.ds(pl.multiple_of(c * C, C), C)
        cos, sin = cos_ref[sl, :], sin_ref[sl, :]
        q = rope(q_ref[sl, :], cos, sin) * GLA_DK ** -0.5
        k = rope(k_ref[sl, :], cos, sin)
        v = v_ref[sl, :].astype(BF16)
        a = a_ref[sl, d * GLA_GATE_RANK:(d + 1) * GLA_GATE_RANK]
        pre = jnp.dot(a.astype(BF16), w2_ref[d].astype(BF16), preferred_element_type=F32) + b2_ref[d]
        log_a = jax.nn.log_sigmoid(pre) / GLA_TAU
        mask = tri[d]
        bcum = jnp.dot(mask.astype(F32), log_a, preferred_element_type=F32, precision=lax.Precision.HIGHEST)
        blast = jnp.sum(log_a, axis=0, keepdims=True)
        q_in = (q * jnp.exp(bcum)).astype(BF16)
        k_in = (k * jnp.exp(-bcum)).astype(BF16)
        k_st = (k * jnp.exp(blast - bcum)).astype(BF16)
        att = lax.dot_general(q_in, k_in, (((1,), (1,)), ((), ())), preferred_element_type=F32)
        att = jnp.where(mask, att, 0.0).astype(BF16)
        st = st_ref[d]
        o = jnp.dot(att, v, preferred_element_type=F32)
        o = o + lax.dot_general(q_in, st.astype(BF16), (((1,), (1,)), ((), ())), preferred_element_type=F32)
        o_ref[sl, :] += o
        kv_t = lax.dot_general(v, k_st, (((0,), (0,)), ((), ())), preferred_element_type=F32)
        st_ref[d] = st * jnp.exp(blast) + kv_t

    def body(i, carry):
        chunk(i, 0)
        chunk(n_chunks - 1 - i, 1)
        return carry

    lax.fori_loop(0, n_chunks, body, 0)
    o = o_ref[...]
    o = o * lax.rsqrt(jnp.mean(o * o, axis=-1, keepdims=True) + RMS_EPS) * ng_ref[...]
    o_ref[...] = o * jax.nn.silu(g_ref[...])


def gla_segment(z, za, s0, w2, b2, norm_g, B, L, rotary):
    H, dk, dv = GLA_HEADS, GLA_DK, GLA_DV
    cos, sin = rope_tables(L, rotary)
    qb, kb, vb, gb = IN_OFF[2] // dk, IN_OFF[3] // dk, IN_OFF[4] // dv, IN_OFF[5] // dv
    return pl.pallas_call(
        functools.partial(_gla_kernel, n_chunks=L // GLA_CHUNK),
        grid=(B, H),
        in_specs=[pl.BlockSpec((L, dk), lambda b, h: (b, qb + h)),
                  pl.BlockSpec((L, dk), lambda b, h: (b, kb + h)),
                  pl.BlockSpec((L, dv), lambda b, h: (b, vb + h)),
                  pl.BlockSpec((L, dv), lambda b, h: (b, gb + h)),
                  pl.BlockSpec((L, 2 * GLA_GATE_RANK), lambda b, h: (b, 0)),
                  pl.BlockSpec((L, dk), lambda b, h: (0, 0)),
                  pl.BlockSpec((L, dk), lambda b, h: (0, 0)),
                  pl.BlockSpec((2, GLA_GATE_RANK, dk), lambda b, h: (0, 0, h)),
                  pl.BlockSpec((2, 1, dk), lambda b, h: (0, 0, h)),
                  pl.BlockSpec((1, dv), lambda b, h: (0, 0)),
                  pl.BlockSpec((None, None, 2, dv, dk), lambda b, h: (b, h, 0, 0, 0))],
        out_specs=[pl.BlockSpec((L, dv), lambda b, h: (b, h)),
                   pl.BlockSpec((None, None, 2, dv, dk), lambda b, h: (b, h, 0, 0, 0))],
        out_shape=[jax.ShapeDtypeStruct((B * L, H * dv), F32), jax.ShapeDtypeStruct((B, H, 2, dv, dk), F32)],
        compiler_params=pltpu.CompilerParams(dimension_semantics=("arbitrary", "arbitrary"),
                                             vmem_limit_bytes=V7X_VMEM_LIMIT_BYTES),
        name="gla_segment",
    )(z, z, z, z, za, cos, sin, w2, b2.reshape(2, 1, -1), norm_g.reshape(1, dv), s0)


def gla_mixer(zl, zc, za_l, za_c, w2, b2, norm_g, B, L, Lc):
    s0 = jnp.zeros((B, GLA_HEADS, 2, GLA_DV, GLA_DK), F32)
    out_c, s_ctx = gla_segment(zc, za_c, s0, w2, b2, norm_g, B, Lc, False)
    out_l, _ = gla_segment(zl, za_l, s_ctx, w2, b2, norm_g, B, L, True)
    return out_l, out_c


def s5_discretise(a_re, a_im, log_dt, b_re, b_im):
    A = lax.complex(a_re.astype(F32), a_im.astype(F32))
    dt = jnp.exp(log_dt.astype(F32))[:, None]
    a_bar = jnp.exp(A * dt)
    b_bar = ((a_bar - 1.0) / A)[..., None] * lax.complex(b_re.astype(F32), b_im.astype(F32))
    return a_bar, b_bar


S5_T = 16
S5_GQ = V7X_LANES // S5_GROUP
S5_NQ = S5_GROUPS // S5_GQ
S5_NS = 2 * S5_GQ * S5_STATE
S5_GP = V7X_LANES // S5_STATE
S5_NP = S5_GQ // S5_GP
S5_PU = S5_GP * S5_T * S5_GROUP
S5_PS = S5_GP * S5_STATE


def s5_operators(a_re, a_im, log_dt, b_re, b_im, c_re, c_im):
    T, GQ, NQ, NP, GP, P, I = S5_T, S5_GQ, S5_NQ, S5_NP, S5_GP, S5_STATE, S5_GROUP
    eye = jnp.eye(GP, dtype=F32)
    w_parts, v_parts, at_parts = [], [], []
    ktot = 0.0
    for d in range(2):
        a_bar, b_bar = s5_discretise(a_re[d], a_im[d], log_dt[d], b_re[d], b_im[d])
        c_mat = lax.complex(c_re[d].astype(F32), c_im[d].astype(F32))
        e = jnp.arange(T + 1, dtype=F32)
        apow = a_bar[None] ** e[:, None, None].astype(jnp.complex64)
        ex = (T - 1 - jnp.arange(T)) if d == 0 else jnp.arange(T)
        w = apow[ex][:, :, :, None] * b_bar[None]
        w = jnp.stack([w.real, w.imag], axis=0).reshape(2, T, NQ, NP, GP, P, I)
        w = jnp.einsum('rsqngpj,gh->qngsjrhp', w, eye)
        w_parts.append(w.reshape(NQ, NP, S5_PU, 2 * S5_PS))
        k = jnp.einsum('gip,tgp,gpj->tgij', c_mat, apow[:T], b_bar).real
        t_idx = jnp.arange(T)[:, None]
        s_idx = jnp.arange(T)[None, :]
        lag = (t_idx - s_idx) if d == 0 else (s_idx - t_idx)
        ktot = ktot + jnp.where((lag >= 0)[:, :, None, None, None], k[jnp.clip(lag, 0, T - 1)], 0.0)
        ey = (jnp.arange(T) + 1) if d == 0 else (T - jnp.arange(T))
        v = c_mat[None] * apow[ey][:, :, None, :]
        v = jnp.stack([v.real, -v.imag], axis=0).reshape(2, T, NQ, NP, GP, I, P)
        v = jnp.einsum('rtqngip,gh->qnrgphti', v, eye)
        v_parts.append(v.reshape(NQ, NP, 2 * S5_PS, S5_PU))
        at = apow[T].reshape(NQ, GQ * P)
        at_parts.append(jnp.concatenate([at.real, at.imag], axis=-1)[:, None, :])
    m = ktot.reshape(T, T, NQ, NP, GP, I, I)
    m = jnp.einsum('tsqngij,gh->qngsjhti', m, eye).reshape(NQ, NP, S5_PU, S5_PU)
    rhs = jnp.concatenate([w_parts[0], w_parts[1], m], axis=-1).astype(BF16)
    return rhs, jnp.stack(v_parts).astype(BF16), jnp.stack(at_parts)


def _s5_in_kernel(u_ref, rhs_ref, s_ref, y_ref):
    half = S5_NS // 2
    for n in range(S5_NP):
        r = jnp.dot(u_ref[:, n * S5_PU:(n + 1) * S5_PU], rhs_ref[n], preferred_element_type=F32)
        for d in range(2):
            s_ref[d, :, n * S5_PS:(n + 1) * S5_PS] = r[:, 2 * d * S5_PS:(2 * d + 1) * S5_PS]
            s_ref[d, :, half + n * S5_PS:half + (n + 1) * S5_PS] = r[:, (2 * d + 1) * S5_PS:(2 * d + 2) * S5_PS]
        y_ref[:, n * S5_PU:(n + 1) * S5_PU] = r[:, 4 * S5_PS:]


def _s5_scan_kernel(s_ref, at_ref, x_ref, *, n_ctx, n_lat, batch):
    d = pl.program_id(0)
    half = S5_NS // 2
    a_r = at_ref[:, :half]
    a_i = at_ref[:, half:]

    def run(b, base, n, carry):
        def body(i, st):
            xr, xi = st
            c = i + d * (n - 1 - 2 * i)
            row = base + b * n + c
            x_ref[pl.ds(row, 1), :] = jnp.concatenate([xr, xi], axis=-1)
            s = s_ref[pl.ds(row, 1), :]
            return a_r * xr - a_i * xi + s[:, :half], a_r * xi + a_i * xr + s[:, half:]
        return lax.fori_loop(0, n, body, carry)

    for b in range(batch):
        zero = jnp.zeros((1, half), F32)
        st = run(b, 0, n_ctx, (zero, zero))
        run(b, batch * n_ctx, n_lat, st)


def _s5_out_kernel(x_ref, v_ref, yin_ref, u_ref, dsk_ref, y_ref):
    half = S5_NS // 2
    for n in range(S5_NP):
        cols = slice(n * S5_PU, (n + 1) * S5_PU)
        y = yin_ref[:, cols] + dsk_ref[:, cols] * u_ref[:, cols]
        for d in range(2):
            x = jnp.concatenate([x_ref[d, :, n * S5_PS:(n + 1) * S5_PS],
                                 x_ref[d, :, half + n * S5_PS:half + (n + 1) * S5_PS]], axis=-1)
            y = y + jnp.dot(x.astype(BF16), v_ref[d, n], preferred_element_type=F32)
        y_ref[:, cols] = y


def s5_core(uc, ul, ops, d_skip, B, L, Lc):
    rhs, vmat, a_t = ops
    T, NQ, NS, NP, GP = S5_T, S5_NQ, S5_NS, S5_NP, S5_GP
    n_ctx, n_lat = Lc // T, L // T
    R = B * (n_ctx + n_lat)
    tw = T * V7X_LANES
    chunked = (T, NQ, NP, GP, S5_GROUP)
    u = jnp.concatenate([uc.reshape(B * n_ctx, *chunked), ul.reshape(B * n_lat, *chunked)], axis=0)
    u = u.transpose(2, 0, 3, 4, 1, 5).reshape(NQ, R, tw)
    params = pltpu.CompilerParams(dimension_semantics=("arbitrary",), vmem_limit_bytes=V7X_VMEM_LIMIT_BYTES)
    s, y_in = pl.pallas_call(
        _s5_in_kernel,
        grid=(NQ,),
        in_specs=[pl.BlockSpec((None, R, tw), lambda q: (q, 0, 0)),
                  pl.BlockSpec((None, NP, S5_PU, 4 * S5_PS + S5_PU), lambda q: (q, 0, 0, 0))],
        out_specs=[pl.BlockSpec((2, None, R, NS), lambda q: (0, q, 0, 0)),
                   pl.BlockSpec((None, R, tw), lambda q: (q, 0, 0))],
        out_shape=[jax.ShapeDtypeStruct((2, NQ, R, NS), F32), jax.ShapeDtypeStruct((NQ, R, tw), F32)],
        compiler_params=params,
        name="s5_in",
    )(u.astype(BF16), rhs)
    x = pl.pallas_call(
        functools.partial(_s5_scan_kernel, n_ctx=n_ctx, n_lat=n_lat, batch=B),
        grid=(2, NQ),
        in_specs=[pl.BlockSpec((None, None, R, NS), lambda d, q: (d, q, 0, 0)),
                  pl.BlockSpec((None, None, 1, NS), lambda d, q: (d, q, 0, 0))],
        out_specs=pl.BlockSpec((None, None, R, NS), lambda d, q: (d, q, 0, 0)),
        out_shape=jax.ShapeDtypeStruct((2, NQ, R, NS), F32),
        compiler_params=pltpu.CompilerParams(dimension_semantics=("arbitrary", "arbitrary")),
        name="s5_scan",
    )(s, a_t)
    dsk = jnp.broadcast_to(d_skip.astype(F32).reshape(NQ, NP * GP, 1, S5_GROUP), (NQ, NP * GP, T, S5_GROUP))
    dsk = dsk.reshape(NQ, 1, tw)
    y = pl.pallas_call(
        _s5_out_kernel,
        grid=(NQ,),
        in_specs=[pl.BlockSpec((2, None, R, NS), lambda q: (0, q, 0, 0)),
                  pl.BlockSpec((2, None, NP, 2 * S5_PS, S5_PU), lambda q: (0, q, 0, 0, 0)),
                  pl.BlockSpec((None, R, tw), lambda q: (q, 0, 0)),
                  pl.BlockSpec((None, R, tw), lambda q: (q, 0, 0)),
                  pl.BlockSpec((None, 1, tw), lambda q: (q, 0, 0))],
        out_specs=pl.BlockSpec((None, R, tw), lambda q: (q, 0, 0)),
        out_shape=jax.ShapeDtypeStruct((NQ, R, tw), F32),
        compiler_params=params,
        name="s5_out",
    )(x, vmat, y_in, u, dsk)
    y = y.reshape(NQ, R, NP, GP, T, S5_GROUP).transpose(1, 4, 0, 2, 3, 5).reshape(R * T, S5_WIDTH)
    return y[:B * Lc], y[B * Lc:]


def s5_glu(y, w, b, l):
    y = jax.nn.gelu(y)
    return (y * jax.nn.sigmoid(mm(y, w, (l,)) + b[l].astype(F32))).astype(BF16)


def _gate_merge_kernel(h_ref, b0_ref, b1_ref, b2_ref, b3_ref, wg0, wg1, wg2, wg3, wb0, wb1, wb2, wb3, bias_ref,
                       o_ref, wgb_ref, wbb_ref):
    wg_refs = (wg0, wg1, wg2, wg3)
    wb_refs = (wb0, wb1, wb2, wb3)
    br_refs = (b0_ref, b1_ref, b2_ref, b3_ref)

    @pl.when(pl.program_id(1) == 0)
    def _():
        for k in range(N_BRANCH):
            wgb_ref[k] = wg_refs[k][...].astype(BF16)
            wbb_ref[k] = wb_refs[k][...].astype(BF16)

    h = h_ref[...]
    acc = None
    for k in range(N_BRANCH):
        gate = jax.nn.sigmoid(jnp.dot(h, wgb_ref[k], preferred_element_type=F32) + bias_ref[k])
        term = gate * jnp.dot(br_refs[k][...], wbb_ref[k], preferred_element_type=F32)
        acc = term if acc is None else acc + term
    o_ref[...] = acc.astype(o_ref.dtype)


def gate_merge(h, branches, w_gate, b_gate, w_branch, l):
    M, D = h.shape
    W = branches[0].shape[1]
    tn = 256
    tm = min(M, 256)

    def wspec(k, rows):
        return pl.BlockSpec((None, None, rows, tn), lambda j, i: (l, k, 0, j), pipeline_mode=pl.Buffered(1))

    return pl.pallas_call(
        _gate_merge_kernel,
        grid=(D // tn, M // tm),
        in_specs=[pl.BlockSpec((tm, D), lambda j, i: (i, 0))]
                 + [pl.BlockSpec((tm, W), lambda j, i: (i, 0)) for _ in range(N_BRANCH)]
                 + [wspec(k, D) for k in range(N_BRANCH)]
                 + [wspec(k, W) for k in range(N_BRANCH)]
                 + [pl.BlockSpec((None, N_BRANCH, 1, tn), lambda j, i: (l, 0, 0, j))],
        out_specs=pl.BlockSpec((tm, tn), lambda j, i: (i, j)),
        out_shape=jax.ShapeDtypeStruct((M, D), BF16),
        scratch_shapes=[pltpu.VMEM((N_BRANCH, D, tn), BF16), pltpu.VMEM((N_BRANCH, W, tn), BF16)],
        compiler_params=pltpu.CompilerParams(dimension_semantics=("arbitrary", "arbitrary"),
                                             vmem_limit_bytes=V7X_VMEM_LIMIT_BYTES),
        name="gate_merge",
    )(h, *branches, *([w_gate] * N_BRANCH), *([w_branch] * N_BRANCH),
      b_gate.reshape(b_gate.shape[0], N_BRANCH, 1, D))


def _moe_gather_kernel(idx_ref, h_ref, o_ref, stage_ref, sem, *, n_tokens, n_experts):
    g = pl.program_id(0)
    cap = o_ref.shape[0]
    b = g // n_experts

    def row_copy(c):
        row = b * n_tokens + idx_ref[g * cap + c]
        return pltpu.make_async_copy(h_ref.at[pl.ds(row, 1), :], stage_ref.at[pl.ds(c, 1), :], sem)

    def start(c, carry):
        row_copy(c).start()
        return carry

    def wait(c, carry):
        row_copy(c).wait()
        return carry

    lax.fori_loop(0, cap, start, 0)
    lax.fori_loop(0, cap, wait, 0)
    o_ref[...] = stage_ref[...].astype(BF16)


def moe_gather(h, idx):
    B, N, D = h.shape
    _, E, cap = idx.shape
    return pl.pallas_call(
        functools.partial(_moe_gather_kernel, n_tokens=N, n_experts=E),
        grid_spec=pltpu.PrefetchScalarGridSpec(
            num_scalar_prefetch=1,
            grid=(B * E,),
            in_specs=[pl.BlockSpec(memory_space=pl.ANY)],
            out_specs=pl.BlockSpec((None, cap, D), lambda g, idx_ref: (g, 0, 0)),
            scratch_shapes=[pltpu.VMEM((cap, D), F32), pltpu.SemaphoreType.DMA(())],
        ),
        out_shape=jax.ShapeDtypeStruct((B * E, cap, D), BF16),
        compiler_params=pltpu.CompilerParams(dimension_semantics=("arbitrary",),
                                             vmem_limit_bytes=V7X_VMEM_LIMIT_BYTES),
        name="moe_gather",
    )(idx.reshape(-1).astype(jnp.int32), h.reshape(B * N, D)).reshape(B, E, cap, D)


def _expert_gu_kernel(x_ref, wg_ref, wu_ref, o_ref, wgb_ref, wub_ref):
    @pl.when(pl.program_id(2) == 0)
    def _():
        wgb_ref[...] = wg_ref[...].astype(BF16)
        wub_ref[...] = wu_ref[...].astype(BF16)

    x = x_ref[...]
    g = jnp.dot(x, wgb_ref[...], preferred_element_type=F32)
    u = jnp.dot(x, wub_ref[...], preferred_element_type=F32)
    o_ref[...] = (jax.nn.silu(g) * u).astype(BF16)


def expert_gu(xs, w_gu, l):
    B, E, C, D = xs.shape
    F = w_gu.shape[3] // 2
    tn = 512
    nj = F // tn
    return pl.pallas_call(
        _expert_gu_kernel,
        grid=(E, nj, B),
        in_specs=[pl.BlockSpec((None, None, C, D), lambda e, j, b: (b, e, 0, 0)),
                  pl.BlockSpec((None, None, D, tn), lambda e, j, b: (l, e, 0, j)),
                  pl.BlockSpec((None, None, D, tn), lambda e, j, b: (l, e, 0, nj + j))],
        out_specs=pl.BlockSpec((None, None, C, tn), lambda e, j, b: (b, e, 0, j)),
        out_shape=jax.ShapeDtypeStruct((B, E, C, F), BF16),
        scratch_shapes=[pltpu.VMEM((D, tn), BF16), pltpu.VMEM((D, tn), BF16)],
        compiler_params=pltpu.CompilerParams(dimension_semantics=("arbitrary",) * 3,
                                             vmem_limit_bytes=V7X_VMEM_LIMIT_BYTES),
        name="expert_gu",
    )(xs, w_gu, w_gu)


def _expert_down_kernel(a_ref, w_ref, gate_ref, o_ref, wb_ref):
    @pl.when(pl.program_id(2) == 0)
    def _():
        wb_ref[...] = w_ref[...].astype(BF16)

    y = jnp.dot(a_ref[...], wb_ref[...], preferred_element_type=F32)
    o_ref[...] = y * gate_ref[...]


def expert_down(act, w_down, gate, l):
    B, E, C, F = act.shape
    D = w_down.shape[3]
    tn = min(1024, D)
    assert D % tn == 0
    return pl.pallas_call(
        _expert_down_kernel,
        grid=(E, D // tn, B),
        in_specs=[pl.BlockSpec((None, None, C, F), lambda e, j, b: (b, e, 0, 0)),
                  pl.BlockSpec((None, None, F, tn), lambda e, j, b: (l, e, 0, j)),
                  pl.BlockSpec((None, None, C, 1), lambda e, j, b: (b, e, 0, 0))],
        out_specs=pl.BlockSpec((None, None, C, tn), lambda e, j, b: (b, e, 0, j)),
        out_shape=jax.ShapeDtypeStruct((B, E, C, D), F32),
        scratch_shapes=[pltpu.VMEM((F, tn), BF16)],
        compiler_params=pltpu.CompilerParams(dimension_semantics=("arbitrary",) * 3,
                                             vmem_limit_bytes=V7X_VMEM_LIMIT_BYTES),
        name="expert_down",
    )(act, w_down, gate[..., None])


def moe_ec(h, router_w, w_gu, w_down, l):
    B, N, D = h.shape
    cap = max(1, EC_CAPACITY * N // N_EXPERTS)
    logits = jnp.einsum('bnd,de->bne', h, router_w[l], precision=lax.Precision.HIGHEST)
    aff = jax.nn.softmax(logits.astype(F32), axis=-1)
    gate, idx = lax.top_k(jnp.swapaxes(aff, 1, 2), cap)
    y = expert_down(expert_gu(moe_gather(h, idx), w_gu, l), w_down, gate, l)
    return jax.vmap(lambda ib, yb: jnp.zeros((N, D), yb.dtype).at[ib.reshape(-1)].add(yb.reshape(-1, D)))(idx, y)


def kernel(x, c, ctx, c_ctx, ada_w, ada_b, mix_pre_g, mix_post_g, ffn_pre_g, ffn_post_g, w_in, na_rpb,
           hy_conv_w, hy_conv_b, hy_w1, hy_b1, hy_w2, hy_b2, hy_w3, hy_freq, hy_bias,
           gla_w2, gla_b2, gla_norm_g, s5_a_re, s5_a_im, s5_log_dt, s5_b_re, s5_b_im, s5_c_re, s5_c_im,
           s5_d, s5_glu_w, s5_glu_b, w_branch, w_gate, b_gate, w_out, router_w, ex_w_gu, ex_w_down):
    xl, xc = x, ctx
    B, L, Lc = x.shape[0], x.shape[1], ctx.shape[1]
    for l in range(DEPTH):
        need_ctx = l < DEPTH - 1
        cvecs = jnp.concatenate([c, c_ctx[None, :]], axis=0)
        mod = mm(jax.nn.silu(cvecs), ada_w, (l,)) + ada_b[l]
        ml = [p[:, None, :] for p in jnp.split(mod[:B], 6, axis=-1)]
        mc = jnp.split(mod[B], 6, axis=-1)

        hl = rms_norm(xl, mix_pre_g[l]) * (1.0 + ml[1]) + ml[0]
        hc = rms_norm(xc, mix_pre_g[l]) * (1.0 + mc[1]) + mc[0]
        hl2, hc2 = hl.reshape(B * L, D_MODEL).astype(BF16), hc.reshape(B * Lc, D_MODEL).astype(BF16)
        w_a, w_s5 = w_in[l, :, IN_MAIN:IN_MAIN + 2 * GLA_GATE_RANK], w_in[l, :, IN_MAIN + 2 * GLA_GATE_RANK:]
        zl, zc = mm(hl2, w_in, (l,), IN_MAIN), mm(hc2, w_in, (l,), IN_MAIN)
        zl3, zc3 = zl.reshape(B, L, IN_MAIN), zc.reshape(B, Lc, IN_MAIN)

        na_l = na_latent(zl, zc, na_bias_table(na_rpb[l], L // GRID_W), B, L, Lc)
        hy_args = (hy_conv_w[l], hy_conv_b[l], hy_w1[l], hy_b1[l], hy_w2[l], hy_b2[l], hy_w3[l], hy_freq[l], hy_bias[l])
        hy_l = hyena_mixer(zl3[..., IN_OFF[1]:IN_OFF[2]], *hy_args, B, L).astype(BF16)
        gla_l, gla_c = gla_mixer(zl, zc, mm(hl2, w_a), mm(hc2, w_a), gla_w2[l], gla_b2[l], gla_norm_g[l], B, L, Lc)
        s5_ops = s5_operators(s5_a_re[l], s5_a_im[l], s5_log_dt[l], s5_b_re[l], s5_b_im[l], s5_c_re[l], s5_c_im[l])
        s5_yc, s5_yl = s5_core(mm(hc2, w_s5), mm(hl2, w_s5), s5_ops, s5_d[l], B, L, Lc)
        s5_l = s5_glu(s5_yl, s5_glu_w, s5_glu_b, l)
        merged = gate_merge(hl2, (na_l, hy_l, gla_l.astype(BF16), s5_l), w_gate, b_gate, w_branch, l)
        yl = mm(merged, w_out, (l,)).reshape(B, L, D_MODEL)
        xl = xl + ml[2] * rms_norm(yl, mix_post_g[l])
        if need_ctx:
            na_c = na_context(zc, B, Lc).astype(BF16)
            hy_c = hyena_mixer(zc3[..., IN_OFF[1]:IN_OFF[2]], *hy_args, B, Lc).astype(BF16)
            s5_c = s5_glu(s5_yc, s5_glu_w, s5_glu_b, l)
            merged = gate_merge(hc2, (na_c, hy_c, gla_c.astype(BF16), s5_c), w_gate, b_gate, w_branch, l)
            yc = mm(merged, w_out, (l,)).reshape(B, Lc, D_MODEL)
            xc = xc + mc[2] * rms_norm(yc, mix_post_g[l])

        hl = rms_norm(xl, ffn_pre_g[l]) * (1.0 + ml[4]) + ml[3]
        xl = xl + ml[5] * rms_norm(moe_ec(hl, router_w, ex_w_gu, ex_w_down, l), ffn_post_g[l])
        if need_ctx:
            hc = rms_norm(xc, ffn_pre_g[l]) * (1.0 + mc[4]) + mc[3]
            xc = xc + mc[5] * rms_norm(moe_ec(hc, router_w, ex_w_gu, ex_w_down, l), ffn_post_g[l])
    return xl
```

```python
import functools
import math

import jax
import jax.numpy as jnp
import numpy as np
from jax import lax
from jax.experimental import pallas as pl
from jax.experimental.pallas import tpu as pltpu

D_MODEL = 4096
BATCH = 2
SEQ = 4096
DEPTH = 2
CTX_LEN = 256
GRID_W = 64

NA_HEADS = 8
NA_HEAD_DIM = 128
NA_WIDTH = NA_HEADS * NA_HEAD_DIM
NA_WIN_R = 8
NA_WIN_C = 16

HY_WIDTH = 1024
HY_ORDER = 2
HY_BANDS = 16
HY_POS_DIM = 1 + 2 * HY_BANDS
HY_FILTER_HIDDEN = 64
HY_DECAY_TARGET = 1e-2
HY_FAST_DECAY = 0.3
HY_SLOW_DECAY = 1.5

GLA_HEADS = 4
GLA_DK = 128
GLA_DV = 256
GLA_QK_WIDTH = GLA_HEADS * GLA_DK
GLA_V_WIDTH = GLA_HEADS * GLA_DV
GLA_GATE_RANK = 16
GLA_TAU = 16.0
GLA_CHUNK = 64

S5_WIDTH = 1024
S5_GROUP = 16
S5_GROUPS = S5_WIDTH // S5_GROUP
S5_STATE = 64

N_BRANCH = 4
BRANCH_WIDTH = 1024
N_EXPERTS = 16
EXPERT_FF = 1024
EC_CAPACITY = 2

ROPE_BASE = 10000.0
RMS_EPS = 1e-6
NEG_INF = -1e30
F32 = jnp.float32
BF16 = jnp.bfloat16

IN_SPLITS = (3 * NA_WIDTH, (HY_ORDER + 1) * HY_WIDTH, GLA_QK_WIDTH, GLA_QK_WIDTH, GLA_V_WIDTH, GLA_V_WIDTH,
             2 * GLA_GATE_RANK, S5_WIDTH)
IN_COLS = sum(IN_SPLITS)
IN_OFF = tuple(int(v) for v in np.concatenate([[0], np.cumsum(IN_SPLITS)]))
IN_MAIN = IN_OFF[6]

V7X_LANES = 128
V7X_SUBLANES = 8
V7X_VMEM_LIMIT_BYTES = 56 * 1024 * 1024


def _mm_kernel(x_ref, w_ref, o_ref, wb_ref):
    @pl.when(pl.program_id(1) == 0)
    def _():
        wb_ref[...] = w_ref[...].astype(BF16)

    o_ref[...] = jnp.dot(x_ref[...].astype(BF16), wb_ref[...], preferred_element_type=F32).astype(o_ref.dtype)


def _mm_tiles(M, K, N):
    tm = M if M <= 1024 else 1024
    tn = 512 if K > 1024 else 1024
    tn = min(tn, N)
    return tm, tn


def mm(x, w, lead=(), ncols=None, out_dtype=F32):
    M, K = x.shape
    nw = w.shape[-1]
    N = nw if ncols is None else ncols
    pad = (-M) % V7X_SUBLANES
    if pad:
        x = jnp.pad(x, ((0, pad), (0, 0)))
    Mp = M + pad
    tm, tn = _mm_tiles(Mp, K, N)
    assert Mp % tm == 0 and (N == nw or N % tn == 0) and w.shape[-2] == K and w.ndim == len(lead) + 2
    out = pl.pallas_call(
        _mm_kernel,
        grid=(pl.cdiv(N, tn), Mp // tm),
        in_specs=[pl.BlockSpec((tm, K), lambda j, i: (i, 0)),
                  pl.BlockSpec((None,) * len(lead) + (K, tn), lambda j, i: tuple(lead) + (0, j))],
        out_specs=pl.BlockSpec((tm, tn), lambda j, i: (i, j)),
        out_shape=jax.ShapeDtypeStruct((Mp, N), out_dtype),
        scratch_shapes=[pltpu.VMEM((K, tn), BF16)],
        compiler_params=pltpu.CompilerParams(dimension_semantics=("arbitrary", "arbitrary"),
                                             vmem_limit_bytes=V7X_VMEM_LIMIT_BYTES),
        name="mm",
    )(x, w)
    return out[:M] if pad else out


def rms_norm(x, g):
    xf = x.astype(F32)
    y = xf * lax.rsqrt(jnp.mean(xf * xf, axis=-1, keepdims=True) + RMS_EPS)
    return (y * g.astype(F32)).astype(x.dtype)


def dense_attention(q, k, v):
    scale = q.shape[-1] ** -0.5
    s = jnp.einsum('bqhd,bkhd->bhqk', q, k, preferred_element_type=F32) * scale
    p = jax.nn.softmax(s, axis=-1).astype(v.dtype)
    o = jnp.einsum('bhqk,bkhd->bqhd', p, v)
    return o.reshape(q.shape[0], q.shape[1], -1)


NA_TILE_ROWS = 8


def na_bias_table(rpb, rows):
    nt = rows // NA_TILE_ROWS
    cases = jnp.array([0, min(1, nt - 1), nt - 1])
    rq = jnp.arange(NA_TILE_ROWS)
    rk = jnp.arange(2 * NA_TILE_ROWS)
    c = jnp.arange(GRID_W)
    r = cases[:, None] * NA_TILE_ROWS + rq[None, :]
    ks = jnp.clip(cases * NA_TILE_ROWS - NA_WIN_R // 2, 0, rows - 2 * NA_TILE_ROWS)
    krow = ks[:, None] + rk[None, :]
    ws = jnp.clip(r - NA_WIN_R // 2, 0, rows - NA_WIN_R)
    row_ok = (krow[:, None, :] >= ws[:, :, None]) & (krow[:, None, :] < ws[:, :, None] + NA_WIN_R)
    row_off = jnp.clip(krow[:, None, :] - r[:, :, None] + (NA_WIN_R - 1), 0, 2 * NA_WIN_R - 2)
    col_start = jnp.clip(c - NA_WIN_C // 2, 0, GRID_W - NA_WIN_C)
    col_ok = (c[None, :] >= col_start[:, None]) & (c[None, :] < col_start[:, None] + NA_WIN_C)
    col_off = jnp.clip(c[None, :] - c[:, None] + (NA_WIN_C - 1), 0, 2 * NA_WIN_C - 2)
    hp = lax.Precision.HIGHEST
    row_hot = (row_off[..., None] == jnp.arange(2 * NA_WIN_R - 1)).astype(F32)
    col_hot = (col_off[..., None] == jnp.arange(2 * NA_WIN_C - 1)).astype(F32)
    b = jnp.einsum('hab,crka->hcrkb', rpb.astype(F32), row_hot, precision=hp)
    b = jnp.einsum('hcrkb,qpb->hcrqkp', b, col_hot, precision=hp)
    ok = row_ok[:, :, None, :, None] & col_ok[None, None, :, None, :]
    b = jnp.where(ok[None], b, NEG_INF).transpose(1, 0, 2, 3, 4, 5)
    return b.reshape(3, rpb.shape[0], NA_TILE_ROWS * GRID_W, 2 * NA_TILE_ROWS * GRID_W).astype(BF16)


def _na_kernel(q_ref, k_ref, v_ref, kc_ref, vc_ref, b_ref, o_ref, *, scale):
    q = q_ref[...].astype(BF16)
    nk = k_ref.shape[0]
    k = jnp.concatenate([k_ref[...].astype(BF16), kc_ref[...].astype(BF16)], axis=0)
    v = jnp.concatenate([v_ref[...].astype(BF16), vc_ref[...].astype(BF16)], axis=0)
    s = lax.dot_general(q, k, (((1,), (1,)), ((), ())), preferred_element_type=F32) * scale
    bias = jnp.concatenate([b_ref[...].astype(F32), jnp.zeros((s.shape[0], s.shape[1] - nk), F32)], axis=1)
    s = s + bias
    m = jnp.max(s, axis=-1, keepdims=True)
    p = jnp.exp(s - m)
    den = jnp.sum(p, axis=-1, keepdims=True)
    o = jnp.dot(p.astype(BF16), v, preferred_element_type=F32)
    o_ref[...] = (o / den).astype(o_ref.dtype)


def na_latent(zl, zc, bias_tab, B, L, Lc):
    rows = L // GRID_W
    nt = rows // NA_TILE_ROWS
    tq = NA_TILE_ROWS * GRID_W
    tk = 2 * tq
    H, dh = NA_HEADS, NA_HEAD_DIM

    def case(j):
        return jnp.where(j == 0, 0, jnp.where(j == nt - 1, 2, 1))

    def kstart(b, j):
        ks = jnp.clip(j * NA_TILE_ROWS - NA_WIN_R // 2, 0, rows - 2 * NA_TILE_ROWS)
        return pl.multiple_of(b * L + ks * GRID_W, (NA_WIN_R // 2) * GRID_W)

    def kv_spec(col0):
        return pl.BlockSpec((pl.Element(tk), pl.Element(dh)),
                            lambda b, j, h: (kstart(b, j), pl.multiple_of((col0 + h) * dh, dh)))

    return pl.pallas_call(
        functools.partial(_na_kernel, scale=dh ** -0.5),
        grid=(B, nt, H),
        in_specs=[
            pl.BlockSpec((tq, dh), lambda b, j, h: (b * nt + j, h)),
            kv_spec(H),
            kv_spec(2 * H),
            pl.BlockSpec((Lc, dh), lambda b, j, h: (b, H + h)),
            pl.BlockSpec((Lc, dh), lambda b, j, h: (b, 2 * H + h)),
            pl.BlockSpec((None, None, tq, tk), lambda b, j, h: (case(j), h, 0, 0)),
        ],
        out_specs=pl.BlockSpec((tq, dh), lambda b, j, h: (b * nt + j, h)),
        out_shape=jax.ShapeDtypeStruct((B * L, H * dh), BF16),
        compiler_params=pltpu.CompilerParams(dimension_semantics=("arbitrary",) * 3),
        name="na_latent",
    )(zl, zl, zl, zc, zc, bias_tab)


def na_context(zc, B, Lc):
    qc, kc, vc = [t.reshape(B, Lc, NA_HEADS, NA_HEAD_DIM) for t in jnp.split(zc[:, :3 * NA_WIDTH], 3, axis=-1)]
    return dense_attention(qc, kc, vc).reshape(B * Lc, NA_WIDTH)


def short_conv3(u, w, b):
    up = jnp.pad(u, ((0, 0), (1, 1), (0, 0)))
    return up[:, :-2] * w[0] + up[:, 1:-1] * w[1] + up[:, 2:] * w[2] + b


def hyena_filters(L, w1, b1, w2, b2, w3, freq):
    r = jnp.arange(2 * L)
    lag = jnp.where(r < L, r, 2 * L - r)
    t = jnp.minimum(lag, L - 1).astype(F32)
    t01 = t / max(L - 1, 1)
    bands = jnp.linspace(1e-4, HY_BANDS - 1, HY_BANDS, dtype=F32)
    ang = (2.0 * math.pi / L) * t[:, None] * bands[None, :]
    z = jnp.concatenate([t01[:, None], jnp.cos(ang), -jnp.sin(ang)], axis=-1)
    f = freq.astype(F32)
    h = jnp.sin(f * (z @ w1.astype(F32) + b1.astype(F32)))
    h = jnp.sin(f * (h @ w2.astype(F32) + b2.astype(F32)))
    w3d = w3.astype(F32).reshape(-1, 2, HY_ORDER * HY_WIDTH)
    h = jnp.where((r < L)[:, None], h @ w3d[:, 0], h @ w3d[:, 1]).reshape(2 * L, HY_ORDER, HY_WIDTH)
    deltas = jnp.abs(jnp.linspace(math.log(HY_DECAY_TARGET) / HY_SLOW_DECAY, math.log(HY_DECAY_TARGET) / HY_FAST_DECAY,
                                  HY_WIDTH, dtype=F32))
    decay = jnp.where((r != L)[:, None], jnp.exp(-t01[:, None] * deltas[None, :]), 0.0)
    kern = h * decay[:, None, :]
    return kern / jnp.sum(jnp.abs(kern), axis=0, keepdims=True)


HY_N1 = 64


def _cis(num, den):
    ang = (2.0 * math.pi / den) * (num % den).astype(F32)
    return jnp.cos(ang), jnp.sin(ang)


def hyena_dft_tables(L):
    N, N1 = 2 * L, HY_N1
    N2 = N // N1
    n1 = jnp.arange(N1)[:, None, None]
    k2 = jnp.arange(N2)[None, :, None]
    n2 = jnp.arange(N2)[None, None, :]
    c, s = _cis(n1 * k2 + n2 * k2 * N1, N)
    g_full = jnp.concatenate([c, -s], axis=1)
    g_fwd = g_full[:, :, :N2 // 2]
    g_inv = g_fwd.transpose(0, 2, 1) / N
    a = jnp.arange(N1)
    fc, fs = _cis(a[:, None] * a[None, :], N1)
    f_fwd = jnp.concatenate([jnp.concatenate([fc, fs], axis=1), jnp.concatenate([-fs, fc], axis=1)], axis=0)
    f_inv = jnp.concatenate([jnp.concatenate([fc, -fs], axis=1), jnp.concatenate([fs, fc], axis=1)], axis=0)
    return (g_fwd.astype(BF16), f_fwd.astype(BF16), f_inv.astype(BF16), g_inv.astype(BF16)), (g_full, f_fwd)


def _split_bf16(x):
    hi = x.astype(BF16)
    return hi, (x - hi.astype(F32)).astype(BF16)


def _dot_hi_lo(a, b):
    a_hi, a_lo = _split_bf16(a)
    b_hi, b_lo = _split_bf16(b)
    return (jnp.dot(a_hi, b_hi, preferred_element_type=F32) + jnp.dot(a_hi, b_lo, preferred_element_type=F32)
            + jnp.dot(a_lo, b_hi, preferred_element_type=F32))


def _hy_s1_kernel(x_ref, g_ref, o_ref):
    n2 = g_ref.shape[1] // 2
    for i in range(x_ref.shape[1]):
        r = _dot_hi_lo(g_ref[i], x_ref[:, i, :])
        o_ref[0, :, i, :] = r[:n2]
        o_ref[1, :, i, :] = r[n2:]


def _hy_s2_kernel(b_ref, f_ref, o_ref):
    n1 = b_ref.shape[1]
    x = _dot_hi_lo(f_ref[...], b_ref[...].reshape(2 * n1, b_ref.shape[2]))
    o_ref[0] = x[:n1]
    o_ref[1] = x[n1:]


def hyena_kernel_spectrum(kern, tables_f32):
    g_full, f_fwd = tables_f32
    N, O, C = kern.shape
    N1 = HY_N1
    N2 = N // N1
    NB = V7X_SUBLANES
    params = pltpu.CompilerParams(dimension_semantics=("arbitrary", "arbitrary"),
                                  vmem_limit_bytes=V7X_VMEM_LIMIT_BYTES)
    bsp = pl.pallas_call(
        _hy_s1_kernel,
        grid=(O, N1 // NB),
        in_specs=[pl.BlockSpec((N2, NB, C), lambda o, n: (0, n, o)),
                  pl.BlockSpec((NB, 2 * N2, N2), lambda o, n: (n, 0, 0))],
        out_specs=pl.BlockSpec((None, 2, N2, NB, C), lambda o, n: (o, 0, 0, n, 0)),
        out_shape=jax.ShapeDtypeStruct((O, 2, N2, N1, C), F32),
        compiler_params=params, name="hy_s1",
    )(kern.reshape(N2, N1, O * C), g_full)
    return pl.pallas_call(
        _hy_s2_kernel,
        grid=(O, N2),
        in_specs=[pl.BlockSpec((None, 2, None, N1, C), lambda o, k: (o, 0, k, 0, 0)),
                  pl.BlockSpec((2 * N1, 2 * N1), lambda o, k: (0, 0))],
        out_specs=pl.BlockSpec((None, 2, None, N1, C), lambda o, k: (o, 0, k, 0, 0)),
        out_shape=jax.ShapeDtypeStruct((O, 2, N2, N1, C), F32),
        compiler_params=params, name="hy_s2",
    )(bsp, f_fwd)


def _hy_p1_kernel(x_ref, g_ref, o_ref):
    n2 = g_ref.shape[1] // 2
    for i in range(x_ref.shape[1]):
        r = jnp.dot(g_ref[i], x_ref[:, i, :].astype(BF16), preferred_element_type=F32)
        o_ref[0, :, i, :] = r[:n2]
        o_ref[1, :, i, :] = r[n2:]


def _hy_p2_kernel(b_ref, h_ref, ff_ref, fi_ref, o_ref):
    n1 = b_ref.shape[1]
    b = b_ref[...].reshape(2 * n1, b_ref.shape[2]).astype(BF16)
    x = jnp.dot(ff_ref[...], b, preferred_element_type=F32)
    xr, xi = x[:n1], x[n1:]
    hr, hi = h_ref[0], h_ref[1]
    y = jnp.concatenate([xr * hr - xi * hi, xr * hi + xi * hr], axis=0).astype(BF16)
    c = jnp.dot(fi_ref[...], y, preferred_element_type=F32)
    o_ref[0] = c[:n1]
    o_ref[1] = c[n1:]


def _hy_p3_kernel(c_ref, g_ref, y_ref, x_ref, bias_ref, o_ref):
    n2 = c_ref.shape[1]
    for i in range(y_ref.shape[1]):
        c = c_ref[:, :, i, :].reshape(2 * n2, c_ref.shape[3]).astype(BF16)
        conv = jnp.dot(g_ref[i], c, preferred_element_type=F32)
        o_ref[:, i, :] = x_ref[:, i, :] * (conv + y_ref[:, i, :] * bias_ref[...])


def hyena_long_conv(y, y_col, xg, xg_col, hspec, order, bias, tables, B, L):
    g_fwd, f_fwd, f_inv, g_inv = tables
    N1 = HY_N1
    N2 = 2 * L // N1
    H2 = N2 // 2
    C = HY_WIDTH
    NB = V7X_SUBLANES
    params = pltpu.CompilerParams(dimension_semantics=("arbitrary", "arbitrary"),
                                  vmem_limit_bytes=V7X_VMEM_LIMIT_BYTES)
    y4 = y.reshape(B, H2, N1, y.shape[1])
    xg4 = xg.reshape(B, H2, N1, xg.shape[1])
    bsp = pl.pallas_call(
        _hy_p1_kernel,
        grid=(B, N1 // NB),
        in_specs=[pl.BlockSpec((None, H2, NB, C), lambda b, n: (b, 0, n, y_col)),
                  pl.BlockSpec((NB, 2 * N2, H2), lambda b, n: (n, 0, 0))],
        out_specs=pl.BlockSpec((None, 2, N2, NB, C), lambda b, n: (b, 0, 0, n, 0)),
        out_shape=jax.ShapeDtypeStruct((B, 2, N2, N1, C), F32),
        compiler_params=params, name="hy_p1",
    )(y4, g_fwd)
    csp = pl.pallas_call(
        _hy_p2_kernel,
        grid=(B, N2),
        in_specs=[pl.BlockSpec((None, 2, None, N1, C), lambda b, k: (b, 0, k, 0, 0)),
                  pl.BlockSpec((None, 2, None, N1, C), lambda b, k: (order, 0, k, 0, 0)),
                  pl.BlockSpec((2 * N1, 2 * N1), lambda b, k: (0, 0)),
                  pl.BlockSpec((2 * N1, 2 * N1), lambda b, k: (0, 0))],
        out_specs=pl.BlockSpec((None, 2, None, N1, C), lambda b, k: (b, 0, k, 0, 0)),
        out_shape=jax.ShapeDtypeStruct((B, 2, N2, N1, C), F32),
        compiler_params=params, name="hy_p2",
    )(bsp, hspec, f_fwd, f_inv)
    out = pl.pallas_call(
        _hy_p3_kernel,
        grid=(B, N1 // NB),
        in_specs=[pl.BlockSpec((None, 2, N2, NB, C), lambda b, n: (b, 0, 0, n, 0)),
                  pl.BlockSpec((NB, H2, 2 * N2), lambda b, n: (n, 0, 0)),
                  pl.BlockSpec((None, H2, NB, C), lambda b, n: (b, 0, n, y_col)),
                  pl.BlockSpec((None, H2, NB, C), lambda b, n: (b, 0, n, xg_col)),
                  pl.BlockSpec((1, C), lambda b, n: (0, 0))],
        out_specs=pl.BlockSpec((None, H2, NB, C), lambda b, n: (b, 0, n, 0)),
        out_shape=jax.ShapeDtypeStruct((B, H2, N1, C), F32),
        compiler_params=params, name="hy_p3",
    )(csp, g_inv, y4, xg4, bias.reshape(1, C))
    return out.reshape(B * L, C)


def hyena_mixer(z, conv_w, conv_b, w1, b1, w2, b2, w3, freq, bias, B, L):
    zc = short_conv3(z.astype(F32), conv_w.astype(F32), conv_b.astype(F32)).reshape(B * L, -1)
    kern = hyena_filters(L, w1, b1, w2, b2, w3, freq)
    tables, tables_f32 = hyena_dft_tables(L)
    hspec = hyena_kernel_spectrum(kern, tables_f32)
    y = zc
    for o in range(HY_ORDER):
        y = hyena_long_conv(y, 0, zc, o + 1, hspec, o, bias[o].astype(F32), tables, B, L)
    return y


def rope_tables(L, rotary):
    if not rotary:
        return jnp.ones((L, GLA_DK), F32), jnp.zeros((L, GLA_DK), F32)
    t = jnp.arange(L)
    pos = jnp.stack([(t // GRID_W).astype(F32), (t % GRID_W).astype(F32)], axis=1)
    quarter = GLA_DK // 4
    inv = ROPE_BASE ** (-jnp.arange(quarter, dtype=F32) / quarter)
    ang = pos[:, :, None] * inv[None, None, :]
    cos = jnp.concatenate([jnp.cos(ang), jnp.cos(ang)], axis=-1).reshape(L, GLA_DK)
    sin = jnp.concatenate([-jnp.sin(ang), jnp.sin(ang)], axis=-1).reshape(L, GLA_DK)
    return cos, sin


def _gla_kernel(q_ref, k_ref, v_ref, g_ref, a_ref, cos_ref, sin_ref, w2_ref, b2_ref, ng_ref, s0_ref, o_ref, st_ref,
                *, n_chunks):
    C = GLA_CHUNK
    quarter = GLA_DK // 4
    lane = lax.broadcasted_iota(jnp.int32, (C, GLA_DK), 1)
    first_half = (lane % (2 * quarter)) < quarter
    row = lax.broadcasted_iota(jnp.int32, (C, C), 0)
    col = lax.broadcasted_iota(jnp.int32, (C, C), 1)
    tri = (row >= col, row <= col)
    o_ref[...] = jnp.zeros_like(o_ref)
    st_ref[...] = s0_ref[...]

    def rope(x, cos, sin):
        swapped = jnp.where(first_half, pltpu.roll(x, GLA_DK - quarter, 1), pltpu.roll(x, quarter, 1))
        return x * cos + swapped * sin

    def chunk(c, d):
        sl = pl.ds(pl.multiple_of(c * C, C), C)
        cos, sin = cos_ref[sl, :], sin_ref[sl, :]
        q = rope(q_ref[sl, :], cos, sin) * GLA_DK ** -0.5
        k = rope(k_ref[sl, :], cos, sin)
        v = v_ref[sl, :].astype(BF16)
        a = a_ref[sl, d * GLA_GATE_RANK:(d + 1) * GLA_GATE_RANK]
        pre = jnp.dot(a.astype(BF16), w2_ref[d].astype(BF16), preferred_element_type=F32) + b2_ref[d]
        log_a = jax.nn.log_sigmoid(pre) / GLA_TAU
        mask = tri[d]
        bcum = jnp.dot(mask.astype(F32), log_a, preferred_element_type=F32, precision=lax.Precision.HIGHEST)
        blast = jnp.sum(log_a, axis=0, keepdims=True)
        q_in = (q * jnp.exp(bcum)).astype(BF16)
        k_in = (k * jnp.exp(-bcum)).astype(BF16)
        k_st = (k * jnp.exp(blast - bcum)).astype(BF16)
        att = lax.dot_general(q_in, k_in, (((1,), (1,)), ((), ())), preferred_element_type=F32)
        att = jnp.where(mask, att, 0.0).astype(BF16)
        st = st_ref[d]
        o = jnp.dot(att, v, preferred_element_type=F32)
        o = o + lax.dot_general(q_in, st.astype(BF16), (((1,), (1,)), ((), ())), preferred_element_type=F32)
        o_ref[sl, :] += o
        kv_t = lax.dot_general(v, k_st, (((0,), (0,)), ((), ())), preferred_element_type=F32)
        st_ref[d] = st * jnp.exp(blast) + kv_t

    def body(i, carry):
        chunk(i, 0)
        chunk(n_chunks - 1 - i, 1)
        return carry

    lax.fori_loop(0, n_chunks, body, 0)
    o = o_ref[...]
    o = o * lax.rsqrt(jnp.mean(o * o, axis=-1, keepdims=True) + RMS_EPS) * ng_ref[...]
    o_ref[...] = o * jax.nn.silu(g_ref[...])


def gla_segment(z, za, s0, w2, b2, norm_g, B, L, rotary):
    H, dk, dv = GLA_HEADS, GLA_DK, GLA_DV
    cos, sin = rope_tables(L, rotary)
    qb, kb, vb, gb = IN_OFF[2] // dk, IN_OFF[3] // dk, IN_OFF[4] // dv, IN_OFF[5] // dv
    return pl.pallas_call(
        functools.partial(_gla_kernel, n_chunks=L // GLA_CHUNK),
        grid=(B, H),
        in_specs=[pl.BlockSpec((L, dk), lambda b, h: (b, qb + h)),
                  pl.BlockSpec((L, dk), lambda b, h: (b, kb + h)),
                  pl.BlockSpec((L, dv), lambda b, h: (b, vb + h)),
                  pl.BlockSpec((L, dv), lambda b, h: (b, gb + h)),
                  pl.BlockSpec((L, 2 * GLA_GATE_RANK), lambda b, h: (b, 0)),
                  pl.BlockSpec((L, dk), lambda b, h: (0, 0)),
                  pl.BlockSpec((L, dk), lambda b, h: (0, 0)),
                  pl.BlockSpec((2, GLA_GATE_RANK, dk), lambda b, h: (0, 0, h)),
                  pl.BlockSpec((2, 1, dk), lambda b, h: (0, 0, h)),
                  pl.BlockSpec((1, dv), lambda b, h: (0, 0)),
                  pl.BlockSpec((None, None, 2, dv, dk), lambda b, h: (b, h, 0, 0, 0))],
        out_specs=[pl.BlockSpec((L, dv), lambda b, h: (b, h)),
                   pl.BlockSpec((None, None, 2, dv, dk), lambda b, h: (b, h, 0, 0, 0))],
        out_shape=[jax.ShapeDtypeStruct((B * L, H * dv), F32), jax.ShapeDtypeStruct((B, H, 2, dv, dk), F32)],
        compiler_params=pltpu.CompilerParams(dimension_semantics=("arbitrary", "arbitrary"),
                                             vmem_limit_bytes=V7X_VMEM_LIMIT_BYTES),
        name="gla_segment",
    )(z, z, z, z, za, cos, sin, w2, b2.reshape(2, 1, -1), norm_g.reshape(1, dv), s0)


def gla_mixer(zl, zc, za_l, za_c, w2, b2, norm_g, B, L, Lc):
    s0 = jnp.zeros((B, GLA_HEADS, 2, GLA_DV, GLA_DK), F32)
    out_c, s_ctx = gla_segment(zc, za_c, s0, w2, b2, norm_g, B, Lc, False)
    out_l, _ = gla_segment(zl, za_l, s_ctx, w2, b2, norm_g, B, L, True)
    return out_l, out_c


def s5_discretise(a_re, a_im, log_dt, b_re, b_im):
    A = lax.complex(a_re.astype(F32), a_im.astype(F32))
    dt = jnp.exp(log_dt.astype(F32))[:, None]
    a_bar = jnp.exp(A * dt)
    b_bar = ((a_bar - 1.0) / A)[..., None] * lax.complex(b_re.astype(F32), b_im.astype(F32))
    return a_bar, b_bar


S5_T = 16
S5_GQ = V7X_LANES // S5_GROUP
S5_NQ = S5_GROUPS // S5_GQ
S5_NS = 2 * S5_GQ * S5_STATE
S5_GP = V7X_LANES // S5_STATE
S5_NP = S5_GQ // S5_GP
S5_PU = S5_GP * S5_T * S5_GROUP
S5_PS = S5_GP * S5_STATE


def s5_operators(a_re, a_im, log_dt, b_re, b_im, c_re, c_im):
    T, GQ, NQ, NP, GP, P, I = S5_T, S5_GQ, S5_NQ, S5_NP, S5_GP, S5_STATE, S5_GROUP
    eye = jnp.eye(GP, dtype=F32)
    w_parts, v_parts, at_parts = [], [], []
    ktot = 0.0
    for d in range(2):
        a_bar, b_bar = s5_discretise(a_re[d], a_im[d], log_dt[d], b_re[d], b_im[d])
        c_mat = lax.complex(c_re[d].astype(F32), c_im[d].astype(F32))
        e = jnp.arange(T + 1, dtype=F32)
        apow = a_bar[None] ** e[:, None, None].astype(jnp.complex64)
        ex = (T - 1 - jnp.arange(T)) if d == 0 else jnp.arange(T)
        w = apow[ex][:, :, :, None] * b_bar[None]
        w = jnp.stack([w.real, w.imag], axis=0).reshape(2, T, NQ, NP, GP, P, I)
        w = jnp.einsum('rsqngpj,gh->qngsjrhp', w, eye)
        w_parts.append(w.reshape(NQ, NP, S5_PU, 2 * S5_PS))
        k = jnp.einsum('gip,tgp,gpj->tgij', c_mat, apow[:T], b_bar).real
        t_idx = jnp.arange(T)[:, None]
        s_idx = jnp.arange(T)[None, :]
        lag = (t_idx - s_idx) if d == 0 else (s_idx - t_idx)
        ktot = ktot + jnp.where((lag >= 0)[:, :, None, None, None], k[jnp.clip(lag, 0, T - 1)], 0.0)
        ey = (jnp.arange(T) + 1) if d == 0 else (T - jnp.arange(T))
        v = c_mat[None] * apow[ey][:, :, None, :]
        v = jnp.stack([v.real, -v.imag], axis=0).reshape(2, T, NQ, NP, GP, I, P)
        v = jnp.einsum('rtqngip,gh->qnrgphti', v, eye)
        v_parts.append(v.reshape(NQ, NP, 2 * S5_PS, S5_PU))
        at = apow[T].reshape(NQ, GQ * P)
        at_parts.append(jnp.concatenate([at.real, at.imag], axis=-1)[:, None, :])
    m = ktot.reshape(T, T, NQ, NP, GP, I, I)
    m = jnp.einsum('tsqngij,gh->qngsjhti', m, eye).reshape(NQ, NP, S5_PU, S5_PU)
    rhs = jnp.concatenate([w_parts[0], w_parts[1], m], axis=-1).astype(BF16)
    return rhs, jnp.stack(v_parts).astype(BF16), jnp.stack(at_parts)


def s5_lane_permutation():
    r = jnp.arange(S5_T * V7X_LANES)
    s, g, j = r // V7X_LANES, (r % V7X_LANES) // S5_GROUP, r % S5_GROUP
    c = (g // S5_GP) * S5_PU + (g % S5_GP) * (S5_T * S5_GROUP) + s * S5_GROUP + j
    return (c[:, None] == r[None, :]).astype(BF16)


def _s5_in_kernel(u_ref, perm_ref, rhs_ref, s_ref, y_ref):
    half = S5_NS // 2
    u_nat = jnp.concatenate([u_ref[:, s, :] for s in range(S5_T)], axis=-1).astype(BF16)
    u = jnp.dot(u_nat, perm_ref[...], preferred_element_type=F32).astype(BF16)
    for n in range(S5_NP):
        r = jnp.dot(u[:, n * S5_PU:(n + 1) * S5_PU], rhs_ref[n], preferred_element_type=F32)
        for d in range(2):
            s_ref[d, :, n * S5_PS:(n + 1) * S5_PS] = r[:, 2 * d * S5_PS:(2 * d + 1) * S5_PS]
            s_ref[d, :, half + n * S5_PS:half + (n + 1) * S5_PS] = r[:, (2 * d + 1) * S5_PS:(2 * d + 2) * S5_PS]
        y_ref[:, n * S5_PU:(n + 1) * S5_PU] = r[:, 4 * S5_PS:]


def _s5_scan_kernel(s_ref, at_ref, x_ref, *, n_ctx, n_lat, batch):
    d = pl.program_id(0)
    half = S5_NS // 2
    a_r = at_ref[:, :half]
    a_i = at_ref[:, half:]

    def run(b, base, n, carry):
        def body(i, st):
            xr, xi = st
            c = i + d * (n - 1 - 2 * i)
            row = base + b * n + c
            x_ref[pl.ds(row, 1), :] = jnp.concatenate([xr, xi], axis=-1)
            s = s_ref[pl.ds(row, 1), :]
            return a_r * xr - a_i * xi + s[:, :half], a_r * xi + a_i * xr + s[:, half:]
        return lax.fori_loop(0, n, body, carry)

    for b in range(batch):
        zero = jnp.zeros((1, half), F32)
        st = run(b, 0, n_ctx, (zero, zero))
        run(b, batch * n_ctx, n_lat, st)


def _s5_out_kernel(x_ref, v_ref, yin_ref, u_ref, perm_ref, dsk_ref, y_ref):
    half = S5_NS // 2
    parts = []
    for n in range(S5_NP):
        y = yin_ref[:, n * S5_PU:(n + 1) * S5_PU]
        for d in range(2):
            x = jnp.concatenate([x_ref[d, :, n * S5_PS:(n + 1) * S5_PS],
                                 x_ref[d, :, half + n * S5_PS:half + (n + 1) * S5_PS]], axis=-1)
            y = y + jnp.dot(x.astype(BF16), v_ref[d, n], preferred_element_type=F32)
        parts.append(y)
    y = jnp.concatenate(parts, axis=-1)
    y_hi = y.astype(BF16)
    y_lo = (y - y_hi.astype(F32)).astype(BF16)
    nt = (((1,), (1,)), ((), ()))
    y = (lax.dot_general(y_hi, perm_ref[...], nt, preferred_element_type=F32)
         + lax.dot_general(y_lo, perm_ref[...], nt, preferred_element_type=F32))
    for t in range(S5_T):
        y_ref[:, t, :] = y[:, t * V7X_LANES:(t + 1) * V7X_LANES] + dsk_ref[...] * u_ref[:, t, :]


def s5_core(uc, ul, ops, d_skip, B, L, Lc):
    rhs, vmat, a_t = ops
    T, NQ, NS, NP = S5_T, S5_NQ, S5_NS, S5_NP
    n_ctx, n_lat = Lc // T, L // T
    R = B * (n_ctx + n_lat)
    RT = R // 2
    tw = T * V7X_LANES
    assert RT % V7X_SUBLANES == 0
    u = jnp.concatenate([uc.reshape(B * n_ctx, T, S5_WIDTH), ul.reshape(B * n_lat, T, S5_WIDTH)], axis=0)
    perm = s5_lane_permutation()
    params = pltpu.CompilerParams(dimension_semantics=("arbitrary", "arbitrary"),
                                  vmem_limit_bytes=V7X_VMEM_LIMIT_BYTES)
    perm_spec = pl.BlockSpec((tw, tw), lambda q, i: (0, 0), pipeline_mode=pl.Buffered(1))
    s, y_in = pl.pallas_call(
        _s5_in_kernel,
        grid=(NQ, R // RT),
        in_specs=[pl.BlockSpec((RT, T, V7X_LANES), lambda q, i: (i, 0, q)),
                  perm_spec,
                  pl.BlockSpec((None, NP, S5_PU, 4 * S5_PS + S5_PU), lambda q, i: (q, 0, 0, 0))],
        out_specs=[pl.BlockSpec((2, None, RT, NS), lambda q, i: (0, q, i, 0)),
                   pl.BlockSpec((None, RT, tw), lambda q, i: (q, i, 0))],
        out_shape=[jax.ShapeDtypeStruct((2, NQ, R, NS), F32), jax.ShapeDtypeStruct((NQ, R, tw), F32)],
        compiler_params=params,
        name="s5_in",
    )(u, perm, rhs)
    x = pl.pallas_call(
        functools.partial(_s5_scan_kernel, n_ctx=n_ctx, n_lat=n_lat, batch=B),
        grid=(2, NQ),
        in_specs=[pl.BlockSpec((None, None, R, NS), lambda d, q: (d, q, 0, 0)),
                  pl.BlockSpec((None, None, 1, NS), lambda d, q: (d, q, 0, 0))],
        out_specs=pl.BlockSpec((None, None, R, NS), lambda d, q: (d, q, 0, 0)),
        out_shape=jax.ShapeDtypeStruct((2, NQ, R, NS), F32),
        compiler_params=pltpu.CompilerParams(dimension_semantics=("arbitrary", "arbitrary")),
        name="s5_scan",
    )(s, a_t)
    y = pl.pallas_call(
        _s5_out_kernel,
        grid=(NQ, R // RT),
        in_specs=[pl.BlockSpec((2, None, RT, NS), lambda q, i: (0, q, i, 0)),
                  pl.BlockSpec((2, None, NP, 2 * S5_PS, S5_PU), lambda q, i: (0, q, 0, 0, 0)),
                  pl.BlockSpec((None, RT, tw), lambda q, i: (q, i, 0)),
                  pl.BlockSpec((RT, T, V7X_LANES), lambda q, i: (i, 0, q)),
                  perm_spec,
                  pl.BlockSpec((1, V7X_LANES), lambda q, i: (0, q))],
        out_specs=pl.BlockSpec((RT, T, V7X_LANES), lambda q, i: (i, 0, q)),
        out_shape=jax.ShapeDtypeStruct((R, T, S5_WIDTH), F32),
        compiler_params=params,
        name="s5_out",
    )(x, vmat, y_in, u, perm, d_skip.astype(F32).reshape(1, S5_WIDTH))
    y = y.reshape(R * T, S5_WIDTH)
    return y[:B * Lc], y[B * Lc:]


def s5_glu(y, w, b, l):
    y = jax.nn.gelu(y)
    return (y * jax.nn.sigmoid(mm(y, w, (l,)) + b[l].astype(F32))).astype(BF16)


def _gate_merge_kernel(h_ref, b0_ref, b1_ref, b2_ref, b3_ref, wg0, wg1, wg2, wg3, wb0, wb1, wb2, wb3, bias_ref,
                       o_ref, wgb_ref, wbb_ref):
    wg_refs = (wg0, wg1, wg2, wg3)
    wb_refs = (wb0, wb1, wb2, wb3)
    br_refs = (b0_ref, b1_ref, b2_ref, b3_ref)

    @pl.when(pl.program_id(1) == 0)
    def _():
        for k in range(N_BRANCH):
            wgb_ref[k] = wg_refs[k][...].astype(BF16)
            wbb_ref[k] = wb_refs[k][...].astype(BF16)

    h = h_ref[...]
    acc = None
    for k in range(N_BRANCH):
        gate = jax.nn.sigmoid(jnp.dot(h, wgb_ref[k], preferred_element_type=F32) + bias_ref[k])
        term = gate * jnp.dot(br_refs[k][...], wbb_ref[k], preferred_element_type=F32)
        acc = term if acc is None else acc + term
    o_ref[...] = acc.astype(o_ref.dtype)


def gate_merge(h, branches, w_gate, b_gate, w_branch, l):
    M, D = h.shape
    W = branches[0].shape[1]
    tn = 256
    tm = min(M, 256)

    def wspec(k, rows):
        return pl.BlockSpec((None, None, rows, tn), lambda j, i: (l, k, 0, j), pipeline_mode=pl.Buffered(1))

    return pl.pallas_call(
        _gate_merge_kernel,
        grid=(D // tn, M // tm),
        in_specs=[pl.BlockSpec((tm, D), lambda j, i: (i, 0))]
                 + [pl.BlockSpec((tm, W), lambda j, i: (i, 0)) for _ in range(N_BRANCH)]
                 + [wspec(k, D) for k in range(N_BRANCH)]
                 + [wspec(k, W) for k in range(N_BRANCH)]
                 + [pl.BlockSpec((None, N_BRANCH, 1, tn), lambda j, i: (l, 0, 0, j))],
        out_specs=pl.BlockSpec((tm, tn), lambda j, i: (i, j)),
        out_shape=jax.ShapeDtypeStruct((M, D), BF16),
        scratch_shapes=[pltpu.VMEM((N_BRANCH, D, tn), BF16), pltpu.VMEM((N_BRANCH, W, tn), BF16)],
        compiler_params=pltpu.CompilerParams(dimension_semantics=("arbitrary", "arbitrary"),
                                             vmem_limit_bytes=V7X_VMEM_LIMIT_BYTES),
        name="gate_merge",
    )(h, *branches, *([w_gate] * N_BRANCH), *([w_branch] * N_BRANCH),
      b_gate.reshape(b_gate.shape[0], N_BRANCH, 1, D))


def _moe_gather_kernel(idx_ref, h_ref, o_ref, stage_ref, sem, *, n_tokens, n_experts):
    g = pl.program_id(0)
    cap = o_ref.shape[0]
    b = g // n_experts

    def row_copy(c):
        row = b * n_tokens + idx_ref[g * cap + c]
        return pltpu.make_async_copy(h_ref.at[pl.ds(row, 1), :], stage_ref.at[pl.ds(c, 1), :], sem)

    def start(c, carry):
        row_copy(c).start()
        return carry

    def wait(c, carry):
        row_copy(c).wait()
        return carry

    lax.fori_loop(0, cap, start, 0)
    lax.fori_loop(0, cap, wait, 0)
    o_ref[...] = stage_ref[...].astype(BF16)


def moe_gather(h, idx):
    B, N, D = h.shape
    _, E, cap = idx.shape
    return pl.pallas_call(
        functools.partial(_moe_gather_kernel, n_tokens=N, n_experts=E),
        grid_spec=pltpu.PrefetchScalarGridSpec(
            num_scalar_prefetch=1,
            grid=(B * E,),
            in_specs=[pl.BlockSpec(memory_space=pl.ANY)],
            out_specs=pl.BlockSpec((None, cap, D), lambda g, idx_ref: (g, 0, 0)),
            scratch_shapes=[pltpu.VMEM((cap, D), F32), pltpu.SemaphoreType.DMA(())],
        ),
        out_shape=jax.ShapeDtypeStruct((B * E, cap, D), BF16),
        compiler_params=pltpu.CompilerParams(dimension_semantics=("arbitrary",),
                                             vmem_limit_bytes=V7X_VMEM_LIMIT_BYTES),
        name="moe_gather",
    )(idx.reshape(-1).astype(jnp.int32), h.reshape(B * N, D)).reshape(B, E, cap, D)


def _expert_gu_kernel(x_ref, wg_ref, wu_ref, o_ref, wgb_ref, wub_ref):
    @pl.when(pl.program_id(2) == 0)
    def _():
        wgb_ref[...] = wg_ref[...].astype(BF16)
        wub_ref[...] = wu_ref[...].astype(BF16)

    x = x_ref[...]
    g = jnp.dot(x, wgb_ref[...], preferred_element_type=F32)
    u = jnp.dot(x, wub_ref[...], preferred_element_type=F32)
    o_ref[...] = (jax.nn.silu(g) * u).astype(BF16)


def expert_gu(xs, w_gu, l):
    B, E, C, D = xs.shape
    F = w_gu.shape[3] // 2
    tn = 512
    nj = F // tn
    return pl.pallas_call(
        _expert_gu_kernel,
        grid=(E, nj, B),
        in_specs=[pl.BlockSpec((None, None, C, D), lambda e, j, b: (b, e, 0, 0)),
                  pl.BlockSpec((None, None, D, tn), lambda e, j, b: (l, e, 0, j)),
                  pl.BlockSpec((None, None, D, tn), lambda e, j, b: (l, e, 0, nj + j))],
        out_specs=pl.BlockSpec((None, None, C, tn), lambda e, j, b: (b, e, 0, j)),
        out_shape=jax.ShapeDtypeStruct((B, E, C, F), BF16),
        scratch_shapes=[pltpu.VMEM((D, tn), BF16), pltpu.VMEM((D, tn), BF16)],
        compiler_params=pltpu.CompilerParams(dimension_semantics=("arbitrary",) * 3,
                                             vmem_limit_bytes=V7X_VMEM_LIMIT_BYTES),
        name="expert_gu",
    )(xs, w_gu, w_gu)


def _expert_down_kernel(idx_ref, a_ref, w_ref, gate_ref, o_ref, y_ref, *, n_experts):
    b, e = pl.program_id(0), pl.program_id(2)
    cap = a_ref.shape[0]

    @pl.when(e == 0)
    def _():
        o_ref[...] = jnp.zeros_like(o_ref)

    y = jnp.dot(a_ref[...], w_ref[...].astype(BF16), preferred_element_type=F32)
    y_ref[...] = y * gate_ref[...]
    base = (b * n_experts + e) * cap

    def add_row(r, carry):
        tok = idx_ref[base + r]
        o_ref[pl.ds(tok, 1), :] += y_ref[pl.ds(r, 1), :]
        return carry

    lax.fori_loop(0, cap, add_row, 0, unroll=4)


def expert_down(act, w_down, gate, idx, n_tokens, l):
    B, E, C, F = act.shape
    D = w_down.shape[3]
    tn = min(1024, D)
    assert D % tn == 0
    return pl.pallas_call(
        functools.partial(_expert_down_kernel, n_experts=E),
        grid_spec=pltpu.PrefetchScalarGridSpec(
            num_scalar_prefetch=1,
            grid=(B, D // tn, E),
            in_specs=[pl.BlockSpec((None, None, C, F), lambda b, j, e, idx_ref: (b, e, 0, 0)),
                      pl.BlockSpec((None, None, F, tn), lambda b, j, e, idx_ref: (l, e, 0, j)),
                      pl.BlockSpec((None, None, C, 1), lambda b, j, e, idx_ref: (b, e, 0, 0))],
            out_specs=pl.BlockSpec((None, n_tokens, tn), lambda b, j, e, idx_ref: (b, 0, j)),
            scratch_shapes=[pltpu.VMEM((C, tn), F32)],
        ),
        out_shape=jax.ShapeDtypeStruct((B, n_tokens, D), F32),
        compiler_params=pltpu.CompilerParams(dimension_semantics=("arbitrary",) * 3,
                                             vmem_limit_bytes=V7X_VMEM_LIMIT_BYTES),
        name="expert_down",
    )(idx.reshape(-1).astype(jnp.int32), act, w_down, gate[..., None])


def moe_ec(h, router_w, w_gu, w_down, l):
    B, N, D = h.shape
    cap = max(1, EC_CAPACITY * N // N_EXPERTS)
    logits = jnp.einsum('bnd,de->bne', h, router_w[l], precision=lax.Precision.HIGHEST)
    aff = jax.nn.softmax(logits.astype(F32), axis=-1)
    gate, idx = lax.top_k(jnp.swapaxes(aff, 1, 2), cap)
    return expert_down(expert_gu(moe_gather(h, idx), w_gu, l), w_down, gate, idx, N, l)


def kernel(x, c, ctx, c_ctx, ada_w, ada_b, mix_pre_g, mix_post_g, ffn_pre_g, ffn_post_g, w_in, na_rpb,
           hy_conv_w, hy_conv_b, hy_w1, hy_b1, hy_w2, hy_b2, hy_w3, hy_freq, hy_bias,
           gla_w2, gla_b2, gla_norm_g, s5_a_re, s5_a_im, s5_log_dt, s5_b_re, s5_b_im, s5_c_re, s5_c_im,
           s5_d, s5_glu_w, s5_glu_b, w_branch, w_gate, b_gate, w_out, router_w, ex_w_gu, ex_w_down):
    xl, xc = x, ctx
    B, L, Lc = x.shape[0], x.shape[1], ctx.shape[1]
    for l in range(DEPTH):
        need_ctx = l < DEPTH - 1
        cvecs = jnp.concatenate([c, c_ctx[None, :]], axis=0)
        mod = mm(jax.nn.silu(cvecs), ada_w, (l,)) + ada_b[l]
        ml = [p[:, None, :] for p in jnp.split(mod[:B], 6, axis=-1)]
        mc = jnp.split(mod[B], 6, axis=-1)

        hl = rms_norm(xl, mix_pre_g[l]) * (1.0 + ml[1]) + ml[0]
        hc = rms_norm(xc, mix_pre_g[l]) * (1.0 + mc[1]) + mc[0]
        hl2, hc2 = hl.reshape(B * L, D_MODEL).astype(BF16), hc.reshape(B * Lc, D_MODEL).astype(BF16)
        w_a, w_s5 = w_in[l, :, IN_MAIN:IN_MAIN + 2 * GLA_GATE_RANK], w_in[l, :, IN_MAIN + 2 * GLA_GATE_RANK:]
        zl, zc = mm(hl2, w_in, (l,), IN_MAIN), mm(hc2, w_in, (l,), IN_MAIN)
        zl3, zc3 = zl.reshape(B, L, IN_MAIN), zc.reshape(B, Lc, IN_MAIN)

        na_l = na_latent(zl, zc, na_bias_table(na_rpb[l], L // GRID_W), B, L, Lc)
        hy_args = (hy_conv_w[l], hy_conv_b[l], hy_w1[l], hy_b1[l], hy_w2[l], hy_b2[l], hy_w3[l], hy_freq[l], hy_bias[l])
        hy_l = hyena_mixer(zl3[..., IN_OFF[1]:IN_OFF[2]], *hy_args, B, L).astype(BF16)
        gla_l, gla_c = gla_mixer(zl, zc, mm(hl2, w_a), mm(hc2, w_a), gla_w2[l], gla_b2[l], gla_norm_g[l], B, L, Lc)
        s5_ops = s5_operators(s5_a_re[l], s5_a_im[l], s5_log_dt[l], s5_b_re[l], s5_b_im[l], s5_c_re[l], s5_c_im[l])
        s5_yc, s5_yl = s5_core(mm(hc2, w_s5), mm(hl2, w_s5), s5_ops, s5_d[l], B, L, Lc)
        s5_l = s5_glu(s5_yl, s5_glu_w, s5_glu_b, l)
        merged = gate_merge(hl2, (na_l, hy_l, gla_l.astype(BF16), s5_l), w_gate, b_gate, w_branch, l)
        yl = mm(merged, w_out, (l,)).reshape(B, L, D_MODEL)
        xl = xl + ml[2] * rms_norm(yl, mix_post_g[l])
        if need_ctx:
            na_c = na_context(zc, B, Lc).astype(BF16)
            hy_c = hyena_mixer(zc3[..., IN_OFF[1]:IN_OFF[2]], *hy_args, B, Lc).astype(BF16)
            s5_c = s5_glu(s5_yc, s5_glu_w, s5_glu_b, l)
            merged = gate_merge(hc2, (na_c, hy_c, gla_c.astype(BF16), s5_c), w_gate, b_gate, w_branch, l)
            yc = mm(merged, w_out, (l,)).reshape(B, Lc, D_MODEL)
            xc = xc + mc[2] * rms_norm(yc, mix_post_g[l])

        hl = rms_norm(xl, ffn_pre_g[l]) * (1.0 + ml[4]) + ml[3]
        xl = xl + ml[5] * rms_norm(moe_ec(hl, router_w, ex_w_gu, ex_w_down, l), ffn_post_g[l])
        if need_ctx:
            hc = rms_norm(xc, ffn_pre_g[l]) * (1.0 + mc[4]) + mc[3]
            xc = xc + mc[5] * rms_norm(moe_ec(hc, router_w, ex_w_gu, ex_w_down, l), ffn_post_g[l])
    return xl
```

```python
import functools
import math

import jax
import jax.numpy as jnp
import numpy as np
from jax import lax
from jax.experimental import pallas as pl
from jax.experimental.pallas import tpu as pltpu

D_MODEL = 4096
BATCH = 2
SEQ = 4096
DEPTH = 2
CTX_LEN = 256
GRID_W = 64

NA_HEADS = 8
NA_HEAD_DIM = 128
NA_WIDTH = NA_HEADS * NA_HEAD_DIM
NA_WIN_R = 8
NA_WIN_C = 16

HY_WIDTH = 1024
HY_ORDER = 2
HY_BANDS = 16
HY_POS_DIM = 1 + 2 * HY_BANDS
HY_FILTER_HIDDEN = 64
HY_DECAY_TARGET = 1e-2
HY_FAST_DECAY = 0.3
HY_SLOW_DECAY = 1.5

GLA_HEADS = 4
GLA_DK = 128
GLA_DV = 256
GLA_QK_WIDTH = GLA_HEADS * GLA_DK
GLA_V_WIDTH = GLA_HEADS * GLA_DV
GLA_GATE_RANK = 16
GLA_TAU = 16.0
GLA_CHUNK = 64

S5_WIDTH = 1024
S5_GROUP = 16
S5_GROUPS = S5_WIDTH // S5_GROUP
S5_STATE = 64

N_BRANCH = 4
BRANCH_WIDTH = 1024
N_EXPERTS = 16
EXPERT_FF = 1024
EC_CAPACITY = 2

ROPE_BASE = 10000.0
RMS_EPS = 1e-6
NEG_INF = -1e30
F32 = jnp.float32
BF16 = jnp.bfloat16

IN_SPLITS = (3 * NA_WIDTH, (HY_ORDER + 1) * HY_WIDTH, GLA_QK_WIDTH, GLA_QK_WIDTH, GLA_V_WIDTH, GLA_V_WIDTH,
             2 * GLA_GATE_RANK, S5_WIDTH)
IN_COLS = sum(IN_SPLITS)
IN_OFF = tuple(int(v) for v in np.concatenate([[0], np.cumsum(IN_SPLITS)]))
IN_MAIN = IN_OFF[6]

V7X_LANES = 128
V7X_SUBLANES = 8
V7X_VMEM_LIMIT_BYTES = 56 * 1024 * 1024


def _mm_kernel(x_ref, w_ref, o_ref, wb_ref):
    @pl.when(pl.program_id(1) == 0)
    def _():
        wb_ref[...] = w_ref[...].astype(BF16)

    o_ref[...] = jnp.dot(x_ref[...].astype(BF16), wb_ref[...], preferred_element_type=F32).astype(o_ref.dtype)


def _mm_tiles(M, K, N):
    tm = M if M <= 1024 else 1024
    tn = 512 if K > 1024 else 1024
    tn = min(tn, N)
    return tm, tn


def mm(x, w, lead=(), ncols=None, out_dtype=F32):
    M, K = x.shape
    nw = w.shape[-1]
    N = nw if ncols is None else ncols
    pad = (-M) % V7X_SUBLANES
    if pad:
        x = jnp.pad(x, ((0, pad), (0, 0)))
    Mp = M + pad
    tm, tn = _mm_tiles(Mp, K, N)
    assert Mp % tm == 0 and (N == nw or N % tn == 0) and w.shape[-2] == K and w.ndim == len(lead) + 2
    out = pl.pallas_call(
        _mm_kernel,
        grid=(pl.cdiv(N, tn), Mp // tm),
        in_specs=[pl.BlockSpec((tm, K), lambda j, i: (i, 0)),
                  pl.BlockSpec((None,) * len(lead) + (K, tn), lambda j, i: tuple(lead) + (0, j))],
        out_specs=pl.BlockSpec((tm, tn), lambda j, i: (i, j)),
        out_shape=jax.ShapeDtypeStruct((Mp, N), out_dtype),
        scratch_shapes=[pltpu.VMEM((K, tn), BF16)],
        compiler_params=pltpu.CompilerParams(dimension_semantics=("arbitrary", "arbitrary"),
                                             vmem_limit_bytes=V7X_VMEM_LIMIT_BYTES),
        name="mm",
    )(x, w)
    return out[:M] if pad else out


def rms_norm(x, g):
    xf = x.astype(F32)
    y = xf * lax.rsqrt(jnp.mean(xf * xf, axis=-1, keepdims=True) + RMS_EPS)
    return (y * g.astype(F32)).astype(x.dtype)


def dense_attention(q, k, v):
    scale = q.shape[-1] ** -0.5
    s = jnp.einsum('bqhd,bkhd->bhqk', q, k, preferred_element_type=F32) * scale
    p = jax.nn.softmax(s, axis=-1).astype(v.dtype)
    o = jnp.einsum('bhqk,bkhd->bqhd', p, v)
    return o.reshape(q.shape[0], q.shape[1], -1)


NA_TILE_ROWS = 8


def na_bias_table(rpb, rows):
    nt = rows // NA_TILE_ROWS
    cases = jnp.array([0, min(1, nt - 1), nt - 1])
    rq = jnp.arange(NA_TILE_ROWS)
    rk = jnp.arange(2 * NA_TILE_ROWS)
    c = jnp.arange(GRID_W)
    r = cases[:, None] * NA_TILE_ROWS + rq[None, :]
    ks = jnp.clip(cases * NA_TILE_ROWS - NA_WIN_R // 2, 0, rows - 2 * NA_TILE_ROWS)
    krow = ks[:, None] + rk[None, :]
    ws = jnp.clip(r - NA_WIN_R // 2, 0, rows - NA_WIN_R)
    row_ok = (krow[:, None, :] >= ws[:, :, None]) & (krow[:, None, :] < ws[:, :, None] + NA_WIN_R)
    row_off = jnp.clip(krow[:, None, :] - r[:, :, None] + (NA_WIN_R - 1), 0, 2 * NA_WIN_R - 2)
    col_start = jnp.clip(c - NA_WIN_C // 2, 0, GRID_W - NA_WIN_C)
    col_ok = (c[None, :] >= col_start[:, None]) & (c[None, :] < col_start[:, None] + NA_WIN_C)
    col_off = jnp.clip(c[None, :] - c[:, None] + (NA_WIN_C - 1), 0, 2 * NA_WIN_C - 2)
    hp = lax.Precision.HIGHEST
    row_hot = (row_off[..., None] == jnp.arange(2 * NA_WIN_R - 1)).astype(F32)
    col_hot = (col_off[..., None] == jnp.arange(2 * NA_WIN_C - 1)).astype(F32)
    b = jnp.einsum('hab,crka->hcrkb', rpb.astype(F32), row_hot, precision=hp)
    b = jnp.einsum('hcrkb,qpb->hcrqkp', b, col_hot, precision=hp)
    ok = row_ok[:, :, None, :, None] & col_ok[None, None, :, None, :]
    b = jnp.where(ok[None], b, NEG_INF).transpose(1, 0, 2, 3, 4, 5)
    return b.reshape(3, rpb.shape[0], NA_TILE_ROWS * GRID_W, 2 * NA_TILE_ROWS * GRID_W).astype(BF16)


def _na_kernel(q_ref, k_ref, v_ref, kc_ref, vc_ref, b_ref, o_ref, *, scale):
    q = q_ref[...].astype(BF16)
    nk = k_ref.shape[0]
    k = jnp.concatenate([k_ref[...].astype(BF16), kc_ref[...].astype(BF16)], axis=0)
    v = jnp.concatenate([v_ref[...].astype(BF16), vc_ref[...].astype(BF16)], axis=0)
    s = lax.dot_general(q, k, (((1,), (1,)), ((), ())), preferred_element_type=F32) * scale
    bias = jnp.concatenate([b_ref[...].astype(F32), jnp.zeros((s.shape[0], s.shape[1] - nk), F32)], axis=1)
    s = s + bias
    m = jnp.max(s, axis=-1, keepdims=True)
    p = jnp.exp(s - m)
    den = jnp.sum(p, axis=-1, keepdims=True)
    o = jnp.dot(p.astype(BF16), v, preferred_element_type=F32)
    o_ref[...] = (o / den).astype(o_ref.dtype)


def na_latent(zl, zc, bias_tab, B, L, Lc):
    rows = L // GRID_W
    nt = rows // NA_TILE_ROWS
    tq = NA_TILE_ROWS * GRID_W
    tk = 2 * tq
    H, dh = NA_HEADS, NA_HEAD_DIM

    def case(j):
        return jnp.where(j == 0, 0, jnp.where(j == nt - 1, 2, 1))

    def kstart(b, j):
        ks = jnp.clip(j * NA_TILE_ROWS - NA_WIN_R // 2, 0, rows - 2 * NA_TILE_ROWS)
        return pl.multiple_of(b * L + ks * GRID_W, (NA_WIN_R // 2) * GRID_W)

    def kv_spec(col0):
        return pl.BlockSpec((pl.Element(tk), pl.Element(dh)),
                            lambda b, j, h: (kstart(b, j), pl.multiple_of((col0 + h) * dh, dh)))

    return pl.pallas_call(
        functools.partial(_na_kernel, scale=dh ** -0.5),
        grid=(B, nt, H),
        in_specs=[
            pl.BlockSpec((tq, dh), lambda b, j, h: (b * nt + j, h)),
            kv_spec(H),
            kv_spec(2 * H),
            pl.BlockSpec((Lc, dh), lambda b, j, h: (b, H + h)),
            pl.BlockSpec((Lc, dh), lambda b, j, h: (b, 2 * H + h)),
            pl.BlockSpec((None, None, tq, tk), lambda b, j, h: (case(j), h, 0, 0)),
        ],
        out_specs=pl.BlockSpec((tq, dh), lambda b, j, h: (b * nt + j, h)),
        out_shape=jax.ShapeDtypeStruct((B * L, H * dh), BF16),
        compiler_params=pltpu.CompilerParams(dimension_semantics=("arbitrary",) * 3),
        name="na_latent",
    )(zl, zl, zl, zc, zc, bias_tab)


def na_context(zc, B, Lc):
    qc, kc, vc = [t.reshape(B, Lc, NA_HEADS, NA_HEAD_DIM) for t in jnp.split(zc[:, :3 * NA_WIDTH], 3, axis=-1)]
    return dense_attention(qc, kc, vc).reshape(B * Lc, NA_WIDTH)


def short_conv3(u, w, b):
    up = jnp.pad(u, ((0, 0), (1, 1), (0, 0)))
    return up[:, :-2] * w[0] + up[:, 1:-1] * w[1] + up[:, 2:] * w[2] + b


def hyena_filters(L, w1, b1, w2, b2, w3, freq):
    r = jnp.arange(2 * L)
    lag = jnp.where(r < L, r, 2 * L - r)
    t = jnp.minimum(lag, L - 1).astype(F32)
    t01 = t / max(L - 1, 1)
    bands = jnp.linspace(1e-4, HY_BANDS - 1, HY_BANDS, dtype=F32)
    ang = (2.0 * math.pi / L) * t[:, None] * bands[None, :]
    z = jnp.concatenate([t01[:, None], jnp.cos(ang), -jnp.sin(ang)], axis=-1)
    f = freq.astype(F32)
    h = jnp.sin(f * (z @ w1.astype(F32) + b1.astype(F32)))
    h = jnp.sin(f * (h @ w2.astype(F32) + b2.astype(F32)))
    w3d = w3.astype(F32).reshape(-1, 2, HY_ORDER * HY_WIDTH)
    h = jnp.where((r < L)[:, None], h @ w3d[:, 0], h @ w3d[:, 1]).reshape(2 * L, HY_ORDER, HY_WIDTH)
    deltas = jnp.abs(jnp.linspace(math.log(HY_DECAY_TARGET) / HY_SLOW_DECAY, math.log(HY_DECAY_TARGET) / HY_FAST_DECAY,
                                  HY_WIDTH, dtype=F32))
    decay = jnp.where((r != L)[:, None], jnp.exp(-t01[:, None] * deltas[None, :]), 0.0)
    kern = h * decay[:, None, :]
    return kern / jnp.sum(jnp.abs(kern), axis=0, keepdims=True)


HY_N1 = 64


def _cis(num, den):
    ang = (2.0 * math.pi / den) * (num % den).astype(F32)
    return jnp.cos(ang), jnp.sin(ang)


def hyena_dft_tables(L):
    N, N1 = 2 * L, HY_N1
    N2 = N // N1
    NB = V7X_SUBLANES
    n1 = jnp.arange(N1)[:, None, None]
    k2 = jnp.arange(N2)[None, :, None]
    n2 = jnp.arange(N2)[None, None, :]
    c, s = _cis(n1 * k2 + n2 * k2 * N1, N)
    g_full = jnp.concatenate([c, -s], axis=1)
    row = jnp.arange(2 * N2 * NB)[None, :, None]
    col = jnp.arange(N2 // 2 * NB)[None, None, :]
    blk = jnp.arange(N1 // NB)[:, None, None]
    r_im, r_k2, r_i = row // (N2 * NB), (row // NB) % N2, row % NB
    c_n2, c_i = col // NB, col % NB
    bc, bs = _cis((blk * NB + r_i) * r_k2 + c_n2 * r_k2 * N1, N)
    g_fwd = jnp.where(r_i == c_i, jnp.where(r_im == 0, bc, -bs), 0.0)
    g_inv = g_fwd.transpose(0, 2, 1) / N
    a = jnp.arange(N1)
    fc, fs = _cis(a[:, None] * a[None, :], N1)
    f_fwd = jnp.concatenate([jnp.concatenate([fc, fs], axis=1), jnp.concatenate([-fs, fc], axis=1)], axis=0)
    f_inv = jnp.concatenate([jnp.concatenate([fc, -fs], axis=1), jnp.concatenate([fs, fc], axis=1)], axis=0)
    return (g_fwd.astype(BF16), f_fwd.astype(BF16), f_inv.astype(BF16), g_inv.astype(BF16)), (g_full, f_fwd)


def _split_bf16(x):
    hi = x.astype(BF16)
    return hi, (x - hi.astype(F32)).astype(BF16)


def _dot_hi_lo(a, b):
    a_hi, a_lo = _split_bf16(a)
    b_hi, b_lo = _split_bf16(b)
    return (jnp.dot(a_hi, b_hi, preferred_element_type=F32) + jnp.dot(a_hi, b_lo, preferred_element_type=F32)
            + jnp.dot(a_lo, b_hi, preferred_element_type=F32))


def _hy_s1_kernel(x_ref, g_ref, o_ref):
    n2 = g_ref.shape[1] // 2
    for i in range(x_ref.shape[1]):
        r = _dot_hi_lo(g_ref[i], x_ref[:, i, :])
        o_ref[0, :, i, :] = r[:n2]
        o_ref[1, :, i, :] = r[n2:]


def _hy_s2_kernel(b_ref, f_ref, o_ref):
    n1 = b_ref.shape[1]
    x = _dot_hi_lo(f_ref[...], b_ref[...].reshape(2 * n1, b_ref.shape[2]))
    o_ref[0] = x[:n1]
    o_ref[1] = x[n1:]


def hyena_kernel_spectrum(kern, tables_f32):
    g_full, f_fwd = tables_f32
    N, O, C = kern.shape
    N1 = HY_N1
    N2 = N // N1
    NB = V7X_SUBLANES
    params = pltpu.CompilerParams(dimension_semantics=("arbitrary", "arbitrary"),
                                  vmem_limit_bytes=V7X_VMEM_LIMIT_BYTES)
    bsp = pl.pallas_call(
        _hy_s1_kernel,
        grid=(O, N1 // NB),
        in_specs=[pl.BlockSpec((N2, NB, C), lambda o, n: (0, n, o)),
                  pl.BlockSpec((NB, 2 * N2, N2), lambda o, n: (n, 0, 0))],
        out_specs=pl.BlockSpec((None, 2, N2, NB, C), lambda o, n: (o, 0, 0, n, 0)),
        out_shape=jax.ShapeDtypeStruct((O, 2, N2, N1, C), F32),
        compiler_params=params, name="hy_s1",
    )(kern.reshape(N2, N1, O * C), g_full)
    return pl.pallas_call(
        _hy_s2_kernel,
        grid=(O, N2),
        in_specs=[pl.BlockSpec((None, 2, None, N1, C), lambda o, k: (o, 0, k, 0, 0)),
                  pl.BlockSpec((2 * N1, 2 * N1), lambda o, k: (0, 0))],
        out_specs=pl.BlockSpec((None, 2, None, N1, C), lambda o, k: (o, 0, k, 0, 0)),
        out_shape=jax.ShapeDtypeStruct((O, 2, N2, N1, C), F32),
        compiler_params=params, name="hy_s2",
    )(bsp, f_fwd)


def _hy_p1_kernel(x_ref, g_ref, o_ref):
    h2, nb, c = x_ref.shape
    r = jnp.dot(g_ref[...], x_ref[...].reshape(h2 * nb, c).astype(BF16), preferred_element_type=F32)
    o_ref[...] = r.reshape(o_ref.shape)


def _hy_p2_kernel(b_ref, h_ref, ff_ref, fi_ref, o_ref):
    n1 = b_ref.shape[1]
    b = b_ref[...].reshape(2 * n1, b_ref.shape[2]).astype(BF16)
    x = jnp.dot(ff_ref[...], b, preferred_element_type=F32)
    xr, xi = x[:n1], x[n1:]
    hr, hi = h_ref[0], h_ref[1]
    y = jnp.concatenate([xr * hr - xi * hi, xr * hi + xi * hr], axis=0).astype(BF16)
    c = jnp.dot(fi_ref[...], y, preferred_element_type=F32)
    o_ref[0] = c[:n1]
    o_ref[1] = c[n1:]


def _hy_p3_kernel(c_ref, g_ref, y_ref, x_ref, bias_ref, o_ref):
    _, n2, nb, c = c_ref.shape
    conv = jnp.dot(g_ref[...], c_ref[...].reshape(2 * n2 * nb, c).astype(BF16), preferred_element_type=F32)
    o_ref[...] = x_ref[...] * (conv.reshape(o_ref.shape) + y_ref[...] * bias_ref[...])


def hyena_long_conv(y, y_col, xg, xg_col, hspec, order, bias, tables, B, L):
    g_fwd, f_fwd, f_inv, g_inv = tables
    N1 = HY_N1
    N2 = 2 * L // N1
    H2 = N2 // 2
    C = HY_WIDTH
    NB = V7X_SUBLANES
    params = pltpu.CompilerParams(dimension_semantics=("arbitrary", "arbitrary"),
                                  vmem_limit_bytes=V7X_VMEM_LIMIT_BYTES)
    y4 = y.reshape(B, H2, N1, y.shape[1])
    xg4 = xg.reshape(B, H2, N1, xg.shape[1])
    bsp = pl.pallas_call(
        _hy_p1_kernel,
        grid=(B, N1 // NB),
        in_specs=[pl.BlockSpec((None, H2, NB, C), lambda b, n: (b, 0, n, y_col)),
                  pl.BlockSpec((None, 2 * N2 * NB, H2 * NB), lambda b, n: (n, 0, 0))],
        out_specs=pl.BlockSpec((None, 2, N2, NB, C), lambda b, n: (b, 0, 0, n, 0)),
        out_shape=jax.ShapeDtypeStruct((B, 2, N2, N1, C), F32),
        compiler_params=params, name="hy_p1",
    )(y4, g_fwd)
    csp = pl.pallas_call(
        _hy_p2_kernel,
        grid=(B, N2),
        in_specs=[pl.BlockSpec((None, 2, None, N1, C), lambda b, k: (b, 0, k, 0, 0)),
                  pl.BlockSpec((None, 2, None, N1, C), lambda b, k: (order, 0, k, 0, 0)),
                  pl.BlockSpec((2 * N1, 2 * N1), lambda b, k: (0, 0)),
                  pl.BlockSpec((2 * N1, 2 * N1), lambda b, k: (0, 0))],
        out_specs=pl.BlockSpec((None, 2, None, N1, C), lambda b, k: (b, 0, k, 0, 0)),
        out_shape=jax.ShapeDtypeStruct((B, 2, N2, N1, C), F32),
        compiler_params=params, name="hy_p2",
    )(bsp, hspec, f_fwd, f_inv)
    out = pl.pallas_call(
        _hy_p3_kernel,
        grid=(B, N1 // NB),
        in_specs=[pl.BlockSpec((None, 2, N2, NB, C), lambda b, n: (b, 0, 0, n, 0)),
                  pl.BlockSpec((None, H2 * NB, 2 * N2 * NB), lambda b, n: (n, 0, 0)),
                  pl.BlockSpec((None, H2, NB, C), lambda b, n: (b, 0, n, y_col)),
                  pl.BlockSpec((None, H2, NB, C), lambda b, n: (b, 0, n, xg_col)),
                  pl.BlockSpec((1, C), lambda b, n: (0, 0))],
        out_specs=pl.BlockSpec((None, H2, NB, C), lambda b, n: (b, 0, n, 0)),
        out_shape=jax.ShapeDtypeStruct((B, H2, N1, C), F32),
        compiler_params=params, name="hy_p3",
    )(csp, g_inv, y4, xg4, bias.reshape(1, C))
    return out.reshape(B * L, C)


def hyena_mixer(z, conv_w, conv_b, w1, b1, w2, b2, w3, freq, bias, B, L):
    zc = short_conv3(z.astype(F32), conv_w.astype(F32), conv_b.astype(F32)).reshape(B * L, -1)
    kern = hyena_filters(L, w1, b1, w2, b2, w3, freq)
    tables, tables_f32 = hyena_dft_tables(L)
    hspec = hyena_kernel_spectrum(kern, tables_f32)
    y = zc
    for o in range(HY_ORDER):
        y = hyena_long_conv(y, 0, zc, o + 1, hspec, o, bias[o].astype(F32), tables, B, L)
    return y


def rope_tables(L, rotary):
    if not rotary:
        return jnp.ones((L, GLA_DK), F32), jnp.zeros((L, GLA_DK), F32)
    t = jnp.arange(L)
    pos = jnp.stack([(t // GRID_W).astype(F32), (t % GRID_W).astype(F32)], axis=1)
    quarter = GLA_DK // 4
    inv = ROPE_BASE ** (-jnp.arange(quarter, dtype=F32) / quarter)
    ang = pos[:, :, None] * inv[None, None, :]
    cos = jnp.concatenate([jnp.cos(ang), jnp.cos(ang)], axis=-1).reshape(L, GLA_DK)
    sin = jnp.concatenate([-jnp.sin(ang), jnp.sin(ang)], axis=-1).reshape(L, GLA_DK)
    return cos, sin


def _gla_kernel(q_ref, k_ref, v_ref, g_ref, a_ref, cos_ref, sin_ref, w2_ref, b2_ref, ng_ref, s0_ref, o_ref, st_ref,
                *, n_chunks):
    C = GLA_CHUNK
    quarter = GLA_DK // 4
    lane = lax.broadcasted_iota(jnp.int32, (C, GLA_DK), 1)
    first_half = (lane % (2 * quarter)) < quarter
    row = lax.broadcasted_iota(jnp.int32, (C, C), 0)
    col = lax.broadcasted_iota(jnp.int32, (C, C), 1)
    tri = (row >= col, row <= col)
    o_ref[...] = jnp.zeros_like(o_ref)
    st_ref[...] = s0_ref[...]

    def rope(x, cos, sin):
        swapped = jnp.where(first_half, pltpu.roll(x, GLA_DK - quarter, 1), pltpu.roll(x, quarter, 1))
        return x * cos + swapped * sin

    def chunk(c, d):
        sl = pl.ds(pl.multiple_of(c * C, C), C)
        cos, sin = cos_ref[sl, :], sin_ref[sl, :]
        q = rope(q_ref[sl, :], cos, sin) * GLA_DK ** -0.5
        k = rope(k_ref[sl, :], cos, sin)
        v = v_ref[sl, :].astype(BF16)
        a = a_ref[sl, d * GLA_GATE_RANK:(d + 1) * GLA_GATE_RANK]
        pre = jnp.dot(a.astype(BF16), w2_ref[d].astype(BF16), preferred_element_type=F32) + b2_ref[d]
        log_a = jax.nn.log_sigmoid(pre) / GLA_TAU
        mask = tri[d]
        bcum = jnp.dot(mask.astype(F32), log_a, preferred_element_type=F32, precision=lax.Precision.HIGHEST)
        blast = jnp.sum(log_a, axis=0, keepdims=True)
        q_in = (q * jnp.exp(bcum)).astype(BF16)
        k_in = (k * jnp.exp(-bcum)).astype(BF16)
        k_st = (k * jnp.exp(blast - bcum)).astype(BF16)
        att = lax.dot_general(q_in, k_in, (((1,), (1,)), ((), ())), preferred_element_type=F32)
        att = jnp.where(mask, att, 0.0).astype(BF16)
        st = st_ref[d]
        o = jnp.dot(att, v, preferred_element_type=F32)
        o = o + lax.dot_general(q_in, st.astype(BF16), (((1,), (1,)), ((), ())), preferred_element_type=F32)
        o_ref[sl, :] += o
        kv_t = lax.dot_general(v, k_st, (((0,), (0,)), ((), ())), preferred_element_type=F32)
        st_ref[d] = st * jnp.exp(blast) + kv_t

    def body(i, carry):
        chunk(i, 0)
        chunk(n_chunks - 1 - i, 1)
        return carry

    lax.fori_loop(0, n_chunks, body, 0, unroll=2)
    o = o_ref[...]
    o = o * lax.rsqrt(jnp.mean(o * o, axis=-1, keepdims=True) + RMS_EPS) * ng_ref[...]
    o_ref[...] = o * jax.nn.silu(g_ref[...])


def gla_segment(z, za, s0, w2, b2, norm_g, B, L, rotary):
    H, dk, dv = GLA_HEADS, GLA_DK, GLA_DV
    cos, sin = rope_tables(L, rotary)
    qb, kb, vb, gb = IN_OFF[2] // dk, IN_OFF[3] // dk, IN_OFF[4] // dv, IN_OFF[5] // dv
    return pl.pallas_call(
        functools.partial(_gla_kernel, n_chunks=L // GLA_CHUNK),
        grid=(B, H),
        in_specs=[pl.BlockSpec((L, dk), lambda b, h: (b, qb + h)),
                  pl.BlockSpec((L, dk), lambda b, h: (b, kb + h)),
                  pl.BlockSpec((L, dv), lambda b, h: (b, vb + h)),
                  pl.BlockSpec((L, dv), lambda b, h: (b, gb + h)),
                  pl.BlockSpec((L, 2 * GLA_GATE_RANK), lambda b, h: (b, 0)),
                  pl.BlockSpec((L, dk), lambda b, h: (0, 0)),
                  pl.BlockSpec((L, dk), lambda b, h: (0, 0)),
                  pl.BlockSpec((2, GLA_GATE_RANK, dk), lambda b, h: (0, 0, h)),
                  pl.BlockSpec((2, 1, dk), lambda b, h: (0, 0, h)),
                  pl.BlockSpec((1, dv), lambda b, h: (0, 0)),
                  pl.BlockSpec((None, None, 2, dv, dk), lambda b, h: (b, h, 0, 0, 0))],
        out_specs=[pl.BlockSpec((L, dv), lambda b, h: (b, h)),
                   pl.BlockSpec((None, None, 2, dv, dk), lambda b, h: (b, h, 0, 0, 0))],
        out_shape=[jax.ShapeDtypeStruct((B * L, H * dv), F32), jax.ShapeDtypeStruct((B, H, 2, dv, dk), F32)],
        compiler_params=pltpu.CompilerParams(dimension_semantics=("arbitrary", "arbitrary"),
                                             vmem_limit_bytes=V7X_VMEM_LIMIT_BYTES),
        name="gla_segment",
    )(z, z, z, z, za, cos, sin, w2, b2.reshape(2, 1, -1), norm_g.reshape(1, dv), s0)


def gla_mixer(zl, zc, za_l, za_c, w2, b2, norm_g, B, L, Lc):
    s0 = jnp.zeros((B, GLA_HEADS, 2, GLA_DV, GLA_DK), F32)
    out_c, s_ctx = gla_segment(zc, za_c, s0, w2, b2, norm_g, B, Lc, False)
    out_l, _ = gla_segment(zl, za_l, s_ctx, w2, b2, norm_g, B, L, True)
    return out_l, out_c


def s5_discretise(a_re, a_im, log_dt, b_re, b_im):
    A = lax.complex(a_re.astype(F32), a_im.astype(F32))
    dt = jnp.exp(log_dt.astype(F32))[:, None]
    a_bar = jnp.exp(A * dt)
    b_bar = ((a_bar - 1.0) / A)[..., None] * lax.complex(b_re.astype(F32), b_im.astype(F32))
    return a_bar, b_bar


S5_T = 16
S5_GQ = V7X_LANES // S5_GROUP
S5_NQ = S5_GROUPS // S5_GQ
S5_NS = 2 * S5_GQ * S5_STATE
S5_GP = V7X_LANES // S5_STATE
S5_NP = S5_GQ // S5_GP
S5_PU = S5_GP * S5_T * S5_GROUP
S5_PS = S5_GP * S5_STATE


def s5_operators(a_re, a_im, log_dt, b_re, b_im, c_re, c_im):
    T, GQ, NQ, NP, GP, P, I = S5_T, S5_GQ, S5_NQ, S5_NP, S5_GP, S5_STATE, S5_GROUP
    eye = jnp.eye(GP, dtype=F32)
    w_parts, v_parts, at_parts = [], [], []
    ktot = 0.0
    for d in range(2):
        a_bar, b_bar = s5_discretise(a_re[d], a_im[d], log_dt[d], b_re[d], b_im[d])
        c_mat = lax.complex(c_re[d].astype(F32), c_im[d].astype(F32))
        e = jnp.arange(T + 1, dtype=F32)
        apow = a_bar[None] ** e[:, None, None].astype(jnp.complex64)
        ex = (T - 1 - jnp.arange(T)) if d == 0 else jnp.arange(T)
        w = apow[ex][:, :, :, None] * b_bar[None]
        w = jnp.stack([w.real, w.imag], axis=0).reshape(2, T, NQ, NP, GP, P, I)
        w = jnp.einsum('rsqngpj,gh->qngsjrhp', w, eye)
        w_parts.append(w.reshape(NQ, NP, S5_PU, 2 * S5_PS))
        k = jnp.einsum('gip,tgp,gpj->tgij', c_mat, apow[:T], b_bar).real
        t_idx = jnp.arange(T)[:, None]
        s_idx = jnp.arange(T)[None, :]
        lag = (t_idx - s_idx) if d == 0 else (s_idx - t_idx)
        ktot = ktot + jnp.where((lag >= 0)[:, :, None, None, None], k[jnp.clip(lag, 0, T - 1)], 0.0)
        ey = (jnp.arange(T) + 1) if d == 0 else (T - jnp.arange(T))
        v = c_mat[None] * apow[ey][:, :, None, :]
        v = jnp.stack([v.real, -v.imag], axis=0).reshape(2, T, NQ, NP, GP, I, P)
        v = jnp.einsum('rtqngip,gh->qnrgphti', v, eye)
        v_parts.append(v.reshape(NQ, NP, 2 * S5_PS, S5_PU))
        at = apow[T].reshape(NQ, GQ * P)
        at_parts.append(jnp.concatenate([at.real, at.imag], axis=-1)[:, None, :])
    m = ktot.reshape(T, T, NQ, NP, GP, I, I)
    m = jnp.einsum('tsqngij,gh->qngsjhti', m, eye).reshape(NQ, NP, S5_PU, S5_PU)
    rhs = jnp.concatenate([w_parts[0], w_parts[1], m], axis=-1).astype(BF16)
    return rhs, jnp.stack(v_parts).astype(BF16), jnp.stack(at_parts)


def s5_lane_permutation():
    r = jnp.arange(S5_T * V7X_LANES)
    s, g, j = r // V7X_LANES, (r % V7X_LANES) // S5_GROUP, r % S5_GROUP
    c = (g // S5_GP) * S5_PU + (g % S5_GP) * (S5_T * S5_GROUP) + s * S5_GROUP + j
    return (c[:, None] == r[None, :]).astype(BF16)


def _s5_in_kernel(u_ref, perm_ref, rhs_ref, s_ref, y_ref):
    half = S5_NS // 2
    u_nat = jnp.concatenate([u_ref[:, s, :] for s in range(S5_T)], axis=-1).astype(BF16)
    u = jnp.dot(u_nat, perm_ref[...], preferred_element_type=F32).astype(BF16)
    for n in range(S5_NP):
        r = jnp.dot(u[:, n * S5_PU:(n + 1) * S5_PU], rhs_ref[n], preferred_element_type=F32)
        for d in range(2):
            s_ref[d, :, n * S5_PS:(n + 1) * S5_PS] = r[:, 2 * d * S5_PS:(2 * d + 1) * S5_PS]
            s_ref[d, :, half + n * S5_PS:half + (n + 1) * S5_PS] = r[:, (2 * d + 1) * S5_PS:(2 * d + 2) * S5_PS]
        y_ref[:, n * S5_PU:(n + 1) * S5_PU] = r[:, 4 * S5_PS:]


def _s5_scan_kernel(s_ref, at_ref, x_ref, *, n_ctx, n_lat, batch):
    d = pl.program_id(0)
    half = S5_NS // 2
    a_r = at_ref[:, :half]
    a_i = at_ref[:, half:]

    def run(b, base, n, carry):
        def body(i, st):
            xr, xi = st
            c = i + d * (n - 1 - 2 * i)
            row = base + b * n + c
            x_ref[pl.ds(row, 1), :] = jnp.concatenate([xr, xi], axis=-1)
            s = s_ref[pl.ds(row, 1), :]
            return a_r * xr - a_i * xi + s[:, :half], a_r * xi + a_i * xr + s[:, half:]
        return lax.fori_loop(0, n, body, carry)

    for b in range(batch):
        zero = jnp.zeros((1, half), F32)
        st = run(b, 0, n_ctx, (zero, zero))
        run(b, batch * n_ctx, n_lat, st)


def _s5_out_kernel(x_ref, v_ref, yin_ref, u_ref, perm_ref, dsk_ref, y_ref):
    half = S5_NS // 2
    parts = []
    for n in range(S5_NP):
        y = yin_ref[:, n * S5_PU:(n + 1) * S5_PU]
        for d in range(2):
            x = jnp.concatenate([x_ref[d, :, n * S5_PS:(n + 1) * S5_PS],
                                 x_ref[d, :, half + n * S5_PS:half + (n + 1) * S5_PS]], axis=-1)
            y = y + jnp.dot(x.astype(BF16), v_ref[d, n], preferred_element_type=F32)
        parts.append(y)
    y = jnp.concatenate(parts, axis=-1)
    y_hi = y.astype(BF16)
    y_lo = (y - y_hi.astype(F32)).astype(BF16)
    nt = (((1,), (1,)), ((), ()))
    y = (lax.dot_general(y_hi, perm_ref[...], nt, preferred_element_type=F32)
         + lax.dot_general(y_lo, perm_ref[...], nt, preferred_element_type=F32))
    for t in range(S5_T):
        y_ref[:, t, :] = y[:, t * V7X_LANES:(t + 1) * V7X_LANES] + dsk_ref[...] * u_ref[:, t, :]


def s5_core(uc, ul, ops, d_skip, B, L, Lc):
    rhs, vmat, a_t = ops
    T, NQ, NS, NP = S5_T, S5_NQ, S5_NS, S5_NP
    n_ctx, n_lat = Lc // T, L // T
    R = B * (n_ctx + n_lat)
    RT = R // 2
    tw = T * V7X_LANES
    assert RT % V7X_SUBLANES == 0
    u = jnp.concatenate([uc.reshape(B * n_ctx, T, S5_WIDTH), ul.reshape(B * n_lat, T, S5_WIDTH)], axis=0)
    perm = s5_lane_permutation()
    params = pltpu.CompilerParams(dimension_semantics=("arbitrary", "arbitrary"),
                                  vmem_limit_bytes=V7X_VMEM_LIMIT_BYTES)
    perm_spec = pl.BlockSpec((tw, tw), lambda q, i: (0, 0), pipeline_mode=pl.Buffered(1))
    s, y_in = pl.pallas_call(
        _s5_in_kernel,
        grid=(NQ, R // RT),
        in_specs=[pl.BlockSpec((RT, T, V7X_LANES), lambda q, i: (i, 0, q)),
                  perm_spec,
                  pl.BlockSpec((None, NP, S5_PU, 4 * S5_PS + S5_PU), lambda q, i: (q, 0, 0, 0))],
        out_specs=[pl.BlockSpec((2, None, RT, NS), lambda q, i: (0, q, i, 0)),
                   pl.BlockSpec((None, RT, tw), lambda q, i: (q, i, 0))],
        out_shape=[jax.ShapeDtypeStruct((2, NQ, R, NS), F32), jax.ShapeDtypeStruct((NQ, R, tw), F32)],
        compiler_params=params,
        name="s5_in",
    )(u, perm, rhs)
    x = pl.pallas_call(
        functools.partial(_s5_scan_kernel, n_ctx=n_ctx, n_lat=n_lat, batch=B),
        grid=(2, NQ),
        in_specs=[pl.BlockSpec((None, None, R, NS), lambda d, q: (d, q, 0, 0)),
                  pl.BlockSpec((None, None, 1, NS), lambda d, q: (d, q, 0, 0))],
        out_specs=pl.BlockSpec((None, None, R, NS), lambda d, q: (d, q, 0, 0)),
        out_shape=jax.ShapeDtypeStruct((2, NQ, R, NS), F32),
        compiler_params=pltpu.CompilerParams(dimension_semantics=("arbitrary", "arbitrary")),
        name="s5_scan",
    )(s, a_t)
    y = pl.pallas_call(
        _s5_out_kernel,
        grid=(NQ, R // RT),
        in_specs=[pl.BlockSpec((2, None, RT, NS), lambda q, i: (0, q, i, 0)),
                  pl.BlockSpec((2, None, NP, 2 * S5_PS, S5_PU), lambda q, i: (0, q, 0, 0, 0)),
                  pl.BlockSpec((None, RT, tw), lambda q, i: (q, i, 0)),
                  pl.BlockSpec((RT, T, V7X_LANES), lambda q, i: (i, 0, q)),
                  perm_spec,
                  pl.BlockSpec((1, V7X_LANES), lambda q, i: (0, q))],
        out_specs=pl.BlockSpec((RT, T, V7X_LANES), lambda q, i: (i, 0, q)),
        out_shape=jax.ShapeDtypeStruct((R, T, S5_WIDTH), F32),
        compiler_params=params,
        name="s5_out",
    )(x, vmat, y_in, u, perm, d_skip.astype(F32).reshape(1, S5_WIDTH))
    y = y.reshape(R * T, S5_WIDTH)
    return y[:B * Lc], y[B * Lc:]


def s5_glu(y, w, b, l):
    y = jax.nn.gelu(y)
    return (y * jax.nn.sigmoid(mm(y, w, (l,)) + b[l].astype(F32))).astype(BF16)


def _gate_merge_kernel(h_ref, b0_ref, b1_ref, b2_ref, b3_ref, wg0, wg1, wg2, wg3, wb0, wb1, wb2, wb3, bias_ref,
                       o_ref, wgb_ref, wbb_ref):
    wg_refs = (wg0, wg1, wg2, wg3)
    wb_refs = (wb0, wb1, wb2, wb3)
    br_refs = (b0_ref, b1_ref, b2_ref, b3_ref)

    @pl.when(pl.program_id(1) == 0)
    def _():
        for k in range(N_BRANCH):
            wgb_ref[k] = wg_refs[k][...].astype(BF16)
            wbb_ref[k] = wb_refs[k][...].astype(BF16)

    h = h_ref[...]
    acc = None
    for k in range(N_BRANCH):
        gate = jax.nn.sigmoid(jnp.dot(h, wgb_ref[k], preferred_element_type=F32) + bias_ref[k])
        term = gate * jnp.dot(br_refs[k][...], wbb_ref[k], preferred_element_type=F32)
        acc = term if acc is None else acc + term
    o_ref[...] = acc.astype(o_ref.dtype)


def gate_merge(h, branches, w_gate, b_gate, w_branch, l):
    M, D = h.shape
    W = branches[0].shape[1]
    tn = 256
    tm = min(M, 256)

    def wspec(k, rows):
        return pl.BlockSpec((None, None, rows, tn), lambda j, i: (l, k, 0, j), pipeline_mode=pl.Buffered(1))

    return pl.pallas_call(
        _gate_merge_kernel,
        grid=(D // tn, M // tm),
        in_specs=[pl.BlockSpec((tm, D), lambda j, i: (i, 0))]
                 + [pl.BlockSpec((tm, W), lambda j, i: (i, 0)) for _ in range(N_BRANCH)]
                 + [wspec(k, D) for k in range(N_BRANCH)]
                 + [wspec(k, W) for k in range(N_BRANCH)]
                 + [pl.BlockSpec((None, N_BRANCH, 1, tn), lambda j, i: (l, 0, 0, j))],
        out_specs=pl.BlockSpec((tm, tn), lambda j, i: (i, j)),
        out_shape=jax.ShapeDtypeStruct((M, D), BF16),
        scratch_shapes=[pltpu.VMEM((N_BRANCH, D, tn), BF16), pltpu.VMEM((N_BRANCH, W, tn), BF16)],
        compiler_params=pltpu.CompilerParams(dimension_semantics=("arbitrary", "arbitrary"),
                                             vmem_limit_bytes=V7X_VMEM_LIMIT_BYTES),
        name="gate_merge",
    )(h, *branches, *([w_gate] * N_BRANCH), *([w_branch] * N_BRANCH),
      b_gate.reshape(b_gate.shape[0], N_BRANCH, 1, D))


def _moe_gather_kernel(idx_ref, h_ref, o_ref, stage_ref, sem, *, n_tokens, n_experts):
    g = pl.program_id(0)
    cap = o_ref.shape[0]
    b = g // n_experts

    def row_copy(c):
        row = b * n_tokens + idx_ref[g * cap + c]
        return pltpu.make_async_copy(h_ref.at[pl.ds(row, 1), :], stage_ref.at[pl.ds(c, 1), :], sem)

    def start(c, carry):
        row_copy(c).start()
        return carry

    def wait(c, carry):
        row_copy(c).wait()
        return carry

    lax.fori_loop(0, cap, start, 0)
    lax.fori_loop(0, cap, wait, 0)
    o_ref[...] = stage_ref[...].astype(BF16)


def moe_gather(h, idx):
    B, N, D = h.shape
    _, E, cap = idx.shape
    return pl.pallas_call(
        functools.partial(_moe_gather_kernel, n_tokens=N, n_experts=E),
        grid_spec=pltpu.PrefetchScalarGridSpec(
            num_scalar_prefetch=1,
            grid=(B * E,),
            in_specs=[pl.BlockSpec(memory_space=pl.ANY)],
            out_specs=pl.BlockSpec((None, cap, D), lambda g, idx_ref: (g, 0, 0)),
            scratch_shapes=[pltpu.VMEM((cap, D), F32), pltpu.SemaphoreType.DMA(())],
        ),
        out_shape=jax.ShapeDtypeStruct((B * E, cap, D), BF16),
        compiler_params=pltpu.CompilerParams(dimension_semantics=("arbitrary",),
                                             vmem_limit_bytes=V7X_VMEM_LIMIT_BYTES),
        name="moe_gather",
    )(idx.reshape(-1).astype(jnp.int32), h.reshape(B * N, D)).reshape(B, E, cap, D)


def _expert_gu_kernel(x_ref, wg_ref, wu_ref, o_ref, wgb_ref, wub_ref):
    @pl.when(pl.program_id(2) == 0)
    def _():
        wgb_ref[...] = wg_ref[...].astype(BF16)
        wub_ref[...] = wu_ref[...].astype(BF16)

    x = x_ref[...]
    g = jnp.dot(x, wgb_ref[...], preferred_element_type=F32)
    u = jnp.dot(x, wub_ref[...], preferred_element_type=F32)
    o_ref[...] = (jax.nn.silu(g) * u).astype(BF16)


def expert_gu(xs, w_gu, l):
    B, E, C, D = xs.shape
    F = w_gu.shape[3] // 2
    tn = 512
    nj = F // tn
    return pl.pallas_call(
        _expert_gu_kernel,
        grid=(E, nj, B),
        in_specs=[pl.BlockSpec((None, None, C, D), lambda e, j, b: (b, e, 0, 0)),
                  pl.BlockSpec((None, None, D, tn), lambda e, j, b: (l, e, 0, j)),
                  pl.BlockSpec((None, None, D, tn), lambda e, j, b: (l, e, 0, nj + j))],
        out_specs=pl.BlockSpec((None, None, C, tn), lambda e, j, b: (b, e, 0, j)),
        out_shape=jax.ShapeDtypeStruct((B, E, C, F), BF16),
        scratch_shapes=[pltpu.VMEM((D, tn), BF16), pltpu.VMEM((D, tn), BF16)],
        compiler_params=pltpu.CompilerParams(dimension_semantics=("arbitrary",) * 3,
                                             vmem_limit_bytes=V7X_VMEM_LIMIT_BYTES),
        name="expert_gu",
    )(xs, w_gu, w_gu)


def _expert_down_kernel(idx_ref, a_ref, w_ref, gate_ref, o_ref, y_ref, *, n_experts):
    b, e = pl.program_id(0), pl.program_id(2)
    cap = a_ref.shape[0]

    @pl.when(e == 0)
    def _():
        o_ref[...] = jnp.zeros_like(o_ref)

    y = jnp.dot(a_ref[...], w_ref[...].astype(BF16), preferred_element_type=F32)
    y_ref[...] = y * gate_ref[...]
    base = (b * n_experts + e) * cap

    def add_row(r, carry):
        tok = idx_ref[base + r]
        o_ref[pl.ds(tok, 1), :] += y_ref[pl.ds(r, 1), :]
        return carry

    lax.fori_loop(0, cap, add_row, 0, unroll=4)


def expert_down(act, w_down, gate, idx, n_tokens, l):
    B, E, C, F = act.shape
    D = w_down.shape[3]
    tn = min(1024, D)
    assert D % tn == 0
    return pl.pallas_call(
        functools.partial(_expert_down_kernel, n_experts=E),
        grid_spec=pltpu.PrefetchScalarGridSpec(
            num_scalar_prefetch=1,
            grid=(B, D // tn, E),
            in_specs=[pl.BlockSpec((None, None, C, F), lambda b, j, e, idx_ref: (b, e, 0, 0)),
                      pl.BlockSpec((None, None, F, tn), lambda b, j, e, idx_ref: (l, e, 0, j)),
                      pl.BlockSpec((None, None, C, 1), lambda b, j, e, idx_ref: (b, e, 0, 0))],
            out_specs=pl.BlockSpec((None, n_tokens, tn), lambda b, j, e, idx_ref: (b, 0, j)),
            scratch_shapes=[pltpu.VMEM((C, tn), F32)],
        ),
        out_shape=jax.ShapeDtypeStruct((B, n_tokens, D), F32),
        compiler_params=pltpu.CompilerParams(dimension_semantics=("arbitrary",) * 3,
                                             vmem_limit_bytes=V7X_VMEM_LIMIT_BYTES),
        name="expert_down",
    )(idx.reshape(-1).astype(jnp.int32), act, w_down, gate[..., None])


def moe_ec(h, router_w, w_gu, w_down, l):
    B, N, D = h.shape
    cap = max(1, EC_CAPACITY * N // N_EXPERTS)
    logits = jnp.einsum('bnd,de->bne', h, router_w[l], precision=lax.Precision.HIGHEST)
    aff = jax.nn.softmax(logits.astype(F32), axis=-1)
    gate, idx = lax.top_k(jnp.swapaxes(aff, 1, 2), cap)
    return expert_down(expert_gu(moe_gather(h, idx), w_gu, l), w_down, gate, idx, N, l)


def kernel(x, c, ctx, c_ctx, ada_w, ada_b, mix_pre_g, mix_post_g, ffn_pre_g, ffn_post_g, w_in, na_rpb,
           hy_conv_w, hy_conv_b, hy_w1, hy_b1, hy_w2, hy_b2, hy_w3, hy_freq, hy_bias,
           gla_w2, gla_b2, gla_norm_g, s5_a_re, s5_a_im, s5_log_dt, s5_b_re, s5_b_im, s5_c_re, s5_c_im,
           s5_d, s5_glu_w, s5_glu_b, w_branch, w_gate, b_gate, w_out, router_w, ex_w_gu, ex_w_down):
    xl, xc = x, ctx
    B, L, Lc = x.shape[0], x.shape[1], ctx.shape[1]
    for l in range(DEPTH):
        need_ctx = l < DEPTH - 1
        cvecs = jnp.concatenate([c, c_ctx[None, :]], axis=0)
        mod = mm(jax.nn.silu(cvecs), ada_w, (l,)) + ada_b[l]
        ml = [p[:, None, :] for p in jnp.split(mod[:B], 6, axis=-1)]
        mc = jnp.split(mod[B], 6, axis=-1)

        hl = rms_norm(xl, mix_pre_g[l]) * (1.0 + ml[1]) + ml[0]
        hc = rms_norm(xc, mix_pre_g[l]) * (1.0 + mc[1]) + mc[0]
        hl2, hc2 = hl.reshape(B * L, D_MODEL).astype(BF16), hc.reshape(B * Lc, D_MODEL).astype(BF16)
        w_a, w_s5 = w_in[l, :, IN_MAIN:IN_MAIN + 2 * GLA_GATE_RANK], w_in[l, :, IN_MAIN + 2 * GLA_GATE_RANK:]
        zl, zc = mm(hl2, w_in, (l,), IN_MAIN), mm(hc2, w_in, (l,), IN_MAIN)
        zl3, zc3 = zl.reshape(B, L, IN_MAIN), zc.reshape(B, Lc, IN_MAIN)

        na_l = na_latent(zl, zc, na_bias_table(na_rpb[l], L // GRID_W), B, L, Lc)
        hy_args = (hy_conv_w[l], hy_conv_b[l], hy_w1[l], hy_b1[l], hy_w2[l], hy_b2[l], hy_w3[l], hy_freq[l], hy_bias[l])
        hy_l = hyena_mixer(zl3[..., IN_OFF[1]:IN_OFF[2]], *hy_args, B, L).astype(BF16)
        gla_l, gla_c = gla_mixer(zl, zc, mm(hl2, w_a), mm(hc2, w_a), gla_w2[l], gla_b2[l], gla_norm_g[l], B, L, Lc)
        s5_ops = s5_operators(s5_a_re[l], s5_a_im[l], s5_log_dt[l], s5_b_re[l], s5_b_im[l], s5_c_re[l], s5_c_im[l])
        s5_yc, s5_yl = s5_core(mm(hc2, w_s5), mm(hl2, w_s5), s5_ops, s5_d[l], B, L, Lc)
        s5_l = s5_glu(s5_yl, s5_glu_w, s5_glu_b, l)
        merged = gate_merge(hl2, (na_l, hy_l, gla_l.astype(BF16), s5_l), w_gate, b_gate, w_branch, l)
        yl = mm(merged, w_out, (l,)).reshape(B, L, D_MODEL)
        xl = xl + ml[2] * rms_norm(yl, mix_post_g[l])
        if need_ctx:
            na_c = na_context(zc, B, Lc).astype(BF16)
            hy_c = hyena_mixer(zc3[..., IN_OFF[1]:IN_OFF[2]], *hy_args, B, Lc).astype(BF16)
            s5_c = s5_glu(s5_yc, s5_glu_w, s5_glu_b, l)
            merged = gate_merge(hc2, (na_c, hy_c, gla_c.astype(BF16), s5_c), w_gate, b_gate, w_branch, l)
            yc = mm(merged, w_out, (l,)).reshape(B, Lc, D_MODEL)
            xc = xc + mc[2] * rms_norm(yc, mix_post_g[l])

        hl = rms_norm(xl, ffn_pre_g[l]) * (1.0 + ml[4]) + ml[3]
        xl = xl + ml[5] * rms_norm(moe_ec(hl, router_w, ex_w_gu, ex_w_down, l), ffn_post_g[l])
        if need_ctx:
            hc = rms_norm(xc, ffn_pre_g[l]) * (1.0 + mc[4]) + mc[3]
            xc = xc + mc[5] * rms_norm(moe_ec(hc, router_w, ex_w_gu, ex_w_down, l), ffn_post_g[l])
    return xl
```

```python
import functools
import math

import jax
import jax.numpy as jnp
import numpy as np
from jax import lax
from jax.experimental import pallas as pl
from jax.experimental.pallas import tpu as pltpu

D_MODEL = 4096
BATCH = 2
SEQ = 4096
DEPTH = 2
CTX_LEN = 256
GRID_W = 64

NA_HEADS = 8
NA_HEAD_DIM = 128
NA_WIDTH = NA_HEADS * NA_HEAD_DIM
NA_WIN_R = 8
NA_WIN_C = 16

HY_WIDTH = 1024
HY_ORDER = 2
HY_BANDS = 16
HY_POS_DIM = 1 + 2 * HY_BANDS
HY_FILTER_HIDDEN = 64
HY_DECAY_TARGET = 1e-2
HY_FAST_DECAY = 0.3
HY_SLOW_DECAY = 1.5

GLA_HEADS = 4
GLA_DK = 128
GLA_DV = 256
GLA_QK_WIDTH = GLA_HEADS * GLA_DK
GLA_V_WIDTH = GLA_HEADS * GLA_DV
GLA_GATE_RANK = 16
GLA_TAU = 16.0
GLA_CHUNK = 64

S5_WIDTH = 1024
S5_GROUP = 16
S5_GROUPS = S5_WIDTH // S5_GROUP
S5_STATE = 64

N_BRANCH = 4
BRANCH_WIDTH = 1024
N_EXPERTS = 16
EXPERT_FF = 1024
EC_CAPACITY = 2

ROPE_BASE = 10000.0
RMS_EPS = 1e-6
NEG_INF = -1e30
F32 = jnp.float32
BF16 = jnp.bfloat16

IN_SPLITS = (3 * NA_WIDTH, (HY_ORDER + 1) * HY_WIDTH, GLA_QK_WIDTH, GLA_QK_WIDTH, GLA_V_WIDTH, GLA_V_WIDTH,
             2 * GLA_GATE_RANK, S5_WIDTH)
IN_COLS = sum(IN_SPLITS)
IN_OFF = tuple(int(v) for v in np.concatenate([[0], np.cumsum(IN_SPLITS)]))
IN_MAIN = IN_OFF[6]

V7X_LANES = 128
V7X_SUBLANES = 8
V7X_VMEM_LIMIT_BYTES = 56 * 1024 * 1024


def _mm_kernel(x_ref, w_ref, o_ref, wb_ref):
    @pl.when(pl.program_id(1) == 0)
    def _():
        wb_ref[...] = w_ref[...].astype(BF16)

    o_ref[...] = jnp.dot(x_ref[...].astype(BF16), wb_ref[...], preferred_element_type=F32).astype(o_ref.dtype)


def _mm_tiles(M, K, N):
    tm = M if M <= 1024 else 1024
    tn = 512 if K > 1024 else 1024
    tn = min(tn, N)
    return tm, tn


def mm(x, w, lead=(), ncols=None, out_dtype=F32):
    M, K = x.shape
    nw = w.shape[-1]
    N = nw if ncols is None else ncols
    pad = (-M) % V7X_SUBLANES
    if pad:
        x = jnp.pad(x, ((0, pad), (0, 0)))
    Mp = M + pad
    tm, tn = _mm_tiles(Mp, K, N)
    assert Mp % tm == 0 and (N == nw or N % tn == 0) and w.shape[-2] == K and w.ndim == len(lead) + 2
    out = pl.pallas_call(
        _mm_kernel,
        grid=(pl.cdiv(N, tn), Mp // tm),
        in_specs=[pl.BlockSpec((tm, K), lambda j, i: (i, 0)),
                  pl.BlockSpec((None,) * len(lead) + (K, tn), lambda j, i: tuple(lead) + (0, j))],
        out_specs=pl.BlockSpec((tm, tn), lambda j, i: (i, j)),
        out_shape=jax.ShapeDtypeStruct((Mp, N), out_dtype),
        scratch_shapes=[pltpu.VMEM((K, tn), BF16)],
        compiler_params=pltpu.CompilerParams(dimension_semantics=("arbitrary", "arbitrary"),
                                             vmem_limit_bytes=V7X_VMEM_LIMIT_BYTES),
        name="mm",
    )(x, w)
    return out[:M] if pad else out


def rms_norm(x, g):
    xf = x.astype(F32)
    y = xf * lax.rsqrt(jnp.mean(xf * xf, axis=-1, keepdims=True) + RMS_EPS)
    return (y * g.astype(F32)).astype(x.dtype)


def dense_attention(q, k, v):
    scale = q.shape[-1] ** -0.5
    s = jnp.einsum('bqhd,bkhd->bhqk', q, k, preferred_element_type=F32) * scale
    p = jax.nn.softmax(s, axis=-1).astype(v.dtype)
    o = jnp.einsum('bhqk,bkhd->bqhd', p, v)
    return o.reshape(q.shape[0], q.shape[1], -1)


NA_TILE_ROWS = 8


def na_bias_table(rpb, rows):
    nt = rows // NA_TILE_ROWS
    cases = jnp.array([0, min(1, nt - 1), nt - 1])
    rq = jnp.arange(NA_TILE_ROWS)
    rk = jnp.arange(2 * NA_TILE_ROWS)
    c = jnp.arange(GRID_W)
    r = cases[:, None] * NA_TILE_ROWS + rq[None, :]
    ks = jnp.clip(cases * NA_TILE_ROWS - NA_WIN_R // 2, 0, rows - 2 * NA_TILE_ROWS)
    krow = ks[:, None] + rk[None, :]
    ws = jnp.clip(r - NA_WIN_R // 2, 0, rows - NA_WIN_R)
    row_ok = (krow[:, None, :] >= ws[:, :, None]) & (krow[:, None, :] < ws[:, :, None] + NA_WIN_R)
    row_off = jnp.clip(krow[:, None, :] - r[:, :, None] + (NA_WIN_R - 1), 0, 2 * NA_WIN_R - 2)
    col_start = jnp.clip(c - NA_WIN_C // 2, 0, GRID_W - NA_WIN_C)
    col_ok = (c[None, :] >= col_start[:, None]) & (c[None, :] < col_start[:, None] + NA_WIN_C)
    col_off = jnp.clip(c[None, :] - c[:, None] + (NA_WIN_C - 1), 0, 2 * NA_WIN_C - 2)
    hp = lax.Precision.HIGHEST
    row_hot = (row_off[..., None] == jnp.arange(2 * NA_WIN_R - 1)).astype(F32)
    col_hot = (col_off[..., None] == jnp.arange(2 * NA_WIN_C - 1)).astype(F32)
    b = jnp.einsum('hab,crka->hcrkb', rpb.astype(F32), row_hot, precision=hp)
    b = jnp.einsum('hcrkb,qpb->hcrqkp', b, col_hot, precision=hp)
    ok = row_ok[:, :, None, :, None] & col_ok[None, None, :, None, :]
    b = jnp.where(ok[None], b, NEG_INF).transpose(1, 0, 2, 3, 4, 5)
    return b.reshape(3, rpb.shape[0], NA_TILE_ROWS * GRID_W, 2 * NA_TILE_ROWS * GRID_W).astype(BF16)


def _na_kernel(q_ref, k_ref, v_ref, kc_ref, vc_ref, b_ref, o_ref, *, scale):
    q = q_ref[...].astype(BF16)
    nk = k_ref.shape[0]
    k = jnp.concatenate([k_ref[...].astype(BF16), kc_ref[...].astype(BF16)], axis=0)
    v = jnp.concatenate([v_ref[...].astype(BF16), vc_ref[...].astype(BF16)], axis=0)
    s = lax.dot_general(q, k, (((1,), (1,)), ((), ())), preferred_element_type=F32) * scale
    bias = jnp.concatenate([b_ref[...].astype(F32), jnp.zeros((s.shape[0], s.shape[1] - nk), F32)], axis=1)
    s = s + bias
    m = jnp.max(s, axis=-1, keepdims=True)
    p = jnp.exp(s - m)
    den = jnp.sum(p, axis=-1, keepdims=True)
    o = jnp.dot(p.astype(BF16), v, preferred_element_type=F32)
    o_ref[...] = (o / den).astype(o_ref.dtype)


def na_latent(zl, zc, bias_tab, B, L, Lc):
    rows = L // GRID_W
    nt = rows // NA_TILE_ROWS
    tq = NA_TILE_ROWS * GRID_W
    tk = 2 * tq
    H, dh = NA_HEADS, NA_HEAD_DIM

    def case(j):
        return jnp.where(j == 0, 0, jnp.where(j == nt - 1, 2, 1))

    def kstart(b, j):
        ks = jnp.clip(j * NA_TILE_ROWS - NA_WIN_R // 2, 0, rows - 2 * NA_TILE_ROWS)
        return pl.multiple_of(b * L + ks * GRID_W, (NA_WIN_R // 2) * GRID_W)

    def kv_spec(col0):
        return pl.BlockSpec((pl.Element(tk), pl.Element(dh)),
                            lambda b, j, h: (kstart(b, j), pl.multiple_of((col0 + h) * dh, dh)))

    return pl.pallas_call(
        functools.partial(_na_kernel, scale=dh ** -0.5),
        grid=(B, nt, H),
        in_specs=[
            pl.BlockSpec((tq, dh), lambda b, j, h: (b * nt + j, h)),
            kv_spec(H),
            kv_spec(2 * H),
            pl.BlockSpec((Lc, dh), lambda b, j, h: (b, H + h)),
            pl.BlockSpec((Lc, dh), lambda b, j, h: (b, 2 * H + h)),
            pl.BlockSpec((None, None, tq, tk), lambda b, j, h: (case(j), h, 0, 0)),
        ],
        out_specs=pl.BlockSpec((tq, dh), lambda b, j, h: (b * nt + j, h)),
        out_shape=jax.ShapeDtypeStruct((B * L, H * dh), BF16),
        compiler_params=pltpu.CompilerParams(dimension_semantics=("arbitrary",) * 3),
        name="na_latent",
    )(zl, zl, zl, zc, zc, bias_tab)


def na_context(zc, B, Lc):
    qc, kc, vc = [t.reshape(B, Lc, NA_HEADS, NA_HEAD_DIM) for t in jnp.split(zc[:, :3 * NA_WIDTH], 3, axis=-1)]
    return dense_attention(qc, kc, vc).reshape(B * Lc, NA_WIDTH)


def short_conv3(u, w, b):
    up = jnp.pad(u, ((0, 0), (1, 1), (0, 0)))
    return up[:, :-2] * w[0] + up[:, 1:-1] * w[1] + up[:, 2:] * w[2] + b


def hyena_filters(L, w1, b1, w2, b2, w3, freq):
    r = jnp.arange(2 * L)
    lag = jnp.where(r < L, r, 2 * L - r)
    t = jnp.minimum(lag, L - 1).astype(F32)
    t01 = t / max(L - 1, 1)
    bands = jnp.linspace(1e-4, HY_BANDS - 1, HY_BANDS, dtype=F32)
    ang = (2.0 * math.pi / L) * t[:, None] * bands[None, :]
    z = jnp.concatenate([t01[:, None], jnp.cos(ang), -jnp.sin(ang)], axis=-1)
    f = freq.astype(F32)
    h = jnp.sin(f * (z @ w1.astype(F32) + b1.astype(F32)))
    h = jnp.sin(f * (h @ w2.astype(F32) + b2.astype(F32)))
    w3d = w3.astype(F32).reshape(-1, 2, HY_ORDER * HY_WIDTH)
    h = jnp.where((r < L)[:, None], h @ w3d[:, 0], h @ w3d[:, 1]).reshape(2 * L, HY_ORDER, HY_WIDTH)
    deltas = jnp.abs(jnp.linspace(math.log(HY_DECAY_TARGET) / HY_SLOW_DECAY, math.log(HY_DECAY_TARGET) / HY_FAST_DECAY,
                                  HY_WIDTH, dtype=F32))
    decay = jnp.where((r != L)[:, None], jnp.exp(-t01[:, None] * deltas[None, :]), 0.0)
    kern = h * decay[:, None, :]
    return kern / jnp.sum(jnp.abs(kern), axis=0, keepdims=True)


HY_N1 = 64


def _cis(num, den):
    ang = (2.0 * math.pi / den) * (num % den).astype(F32)
    return jnp.cos(ang), jnp.sin(ang)


def hyena_dft_tables(L):
    N, N1 = 2 * L, HY_N1
    N2 = N // N1
    NB = V7X_SUBLANES
    n1 = jnp.arange(N1)[:, None, None]
    k2 = jnp.arange(N2)[None, :, None]
    n2 = jnp.arange(N2)[None, None, :]
    c, s = _cis(n1 * k2 + n2 * k2 * N1, N)
    g_full = jnp.concatenate([c, -s], axis=1)
    row = jnp.arange(2 * N2 * NB)[None, :, None]
    col = jnp.arange(N2 // 2 * NB)[None, None, :]
    blk = jnp.arange(N1 // NB)[:, None, None]
    r_im, r_k2, r_i = row // (N2 * NB), (row // NB) % N2, row % NB
    c_n2, c_i = col // NB, col % NB
    bc, bs = _cis((blk * NB + r_i) * r_k2 + c_n2 * r_k2 * N1, N)
    g_fwd = jnp.where(r_i == c_i, jnp.where(r_im == 0, bc, -bs), 0.0)
    g_inv = g_fwd.transpose(0, 2, 1) / N
    a = jnp.arange(N1)
    fc, fs = _cis(a[:, None] * a[None, :], N1)
    f_fwd = jnp.concatenate([jnp.concatenate([fc, fs], axis=1), jnp.concatenate([-fs, fc], axis=1)], axis=0)
    f_inv = jnp.concatenate([jnp.concatenate([fc, -fs], axis=1), jnp.concatenate([fs, fc], axis=1)], axis=0)
    return (g_fwd.astype(BF16), f_fwd.astype(BF16), f_inv.astype(BF16), g_inv.astype(BF16)), (g_full, f_fwd)


def _split_bf16(x):
    hi = x.astype(BF16)
    return hi, (x - hi.astype(F32)).astype(BF16)


def _dot_hi_lo(a, b):
    a_hi, a_lo = _split_bf16(a)
    b_hi, b_lo = _split_bf16(b)
    return (jnp.dot(a_hi, b_hi, preferred_element_type=F32) + jnp.dot(a_hi, b_lo, preferred_element_type=F32)
            + jnp.dot(a_lo, b_hi, preferred_element_type=F32))


def _hy_s1_kernel(x_ref, g_ref, o_ref):
    n2 = g_ref.shape[1] // 2
    for i in range(x_ref.shape[1]):
        r = _dot_hi_lo(g_ref[i], x_ref[:, i, :])
        o_ref[0, :, i, :] = r[:n2]
        o_ref[1, :, i, :] = r[n2:]


def _hy_s2_kernel(b_ref, f_ref, o_ref):
    n1 = b_ref.shape[1]
    x = _dot_hi_lo(f_ref[...], b_ref[...].reshape(2 * n1, b_ref.shape[2]))
    o_ref[0] = x[:n1]
    o_ref[1] = x[n1:]


def hyena_kernel_spectrum(kern, tables_f32):
    g_full, f_fwd = tables_f32
    N, O, C = kern.shape
    N1 = HY_N1
    N2 = N // N1
    NB = V7X_SUBLANES
    params = pltpu.CompilerParams(dimension_semantics=("arbitrary", "arbitrary"),
                                  vmem_limit_bytes=V7X_VMEM_LIMIT_BYTES)
    bsp = pl.pallas_call(
        _hy_s1_kernel,
        grid=(O, N1 // NB),
        in_specs=[pl.BlockSpec((N2, NB, C), lambda o, n: (0, n, o)),
                  pl.BlockSpec((NB, 2 * N2, N2), lambda o, n: (n, 0, 0))],
        out_specs=pl.BlockSpec((None, 2, N2, NB, C), lambda o, n: (o, 0, 0, n, 0)),
        out_shape=jax.ShapeDtypeStruct((O, 2, N2, N1, C), F32),
        compiler_params=params, name="hy_s1",
    )(kern.reshape(N2, N1, O * C), g_full)
    return pl.pallas_call(
        _hy_s2_kernel,
        grid=(O, N2),
        in_specs=[pl.BlockSpec((None, 2, None, N1, C), lambda o, k: (o, 0, k, 0, 0)),
                  pl.BlockSpec((2 * N1, 2 * N1), lambda o, k: (0, 0))],
        out_specs=pl.BlockSpec((None, 2, None, N1, C), lambda o, k: (o, 0, k, 0, 0)),
        out_shape=jax.ShapeDtypeStruct((O, 2, N2, N1, C), F32),
        compiler_params=params, name="hy_s2",
    )(bsp, f_fwd)


def _hy_p1_kernel(x_ref, g_ref, o_ref):
    h2, nb, c = x_ref.shape
    r = jnp.dot(g_ref[...], x_ref[...].reshape(h2 * nb, c).astype(BF16), preferred_element_type=F32)
    o_ref[...] = r.reshape(o_ref.shape)


def _hy_p2_kernel(b_ref, h_ref, ff_ref, fi_ref, o_ref):
    _, kb, n1, c_dim = b_ref.shape
    for k in range(kb):
        b = b_ref[:, k].reshape(2 * n1, c_dim).astype(BF16)
        x = jnp.dot(ff_ref[...], b, preferred_element_type=F32)
        xr, xi = x[:n1], x[n1:]
        hr, hi = h_ref[0, k], h_ref[1, k]
        y = jnp.concatenate([xr * hr - xi * hi, xr * hi + xi * hr], axis=0).astype(BF16)
        c = jnp.dot(fi_ref[...], y, preferred_element_type=F32)
        o_ref[0, k] = c[:n1]
        o_ref[1, k] = c[n1:]


def _hy_p3_kernel(c_ref, g_ref, y_ref, x_ref, bias_ref, o_ref):
    _, n2, nb, c = c_ref.shape
    conv = jnp.dot(g_ref[...], c_ref[...].reshape(2 * n2 * nb, c).astype(BF16), preferred_element_type=F32)
    o_ref[...] = x_ref[...] * (conv.reshape(o_ref.shape) + y_ref[...] * bias_ref[...])


def hyena_long_conv(y, y_col, xg, xg_col, hspec, order, bias, tables, B, L):
    g_fwd, f_fwd, f_inv, g_inv = tables
    N1 = HY_N1
    N2 = 2 * L // N1
    H2 = N2 // 2
    C = HY_WIDTH
    NB = V7X_SUBLANES
    KB = 2
    params = pltpu.CompilerParams(dimension_semantics=("arbitrary", "arbitrary"),
                                  vmem_limit_bytes=V7X_VMEM_LIMIT_BYTES)
    y4 = y.reshape(B, H2, N1, y.shape[1])
    xg4 = xg.reshape(B, H2, N1, xg.shape[1])
    bsp = pl.pallas_call(
        _hy_p1_kernel,
        grid=(B, N1 // NB),
        in_specs=[pl.BlockSpec((None, H2, NB, C), lambda b, n: (b, 0, n, y_col)),
                  pl.BlockSpec((None, 2 * N2 * NB, H2 * NB), lambda b, n: (n, 0, 0))],
        out_specs=pl.BlockSpec((None, 2, N2, NB, C), lambda b, n: (b, 0, 0, n, 0)),
        out_shape=jax.ShapeDtypeStruct((B, 2, N2, N1, C), F32),
        compiler_params=params, name="hy_p1",
    )(y4, g_fwd)
    csp = pl.pallas_call(
        _hy_p2_kernel,
        grid=(N2 // KB, B),
        in_specs=[pl.BlockSpec((None, 2, KB, N1, C), lambda k, b: (b, 0, k, 0, 0)),
                  pl.BlockSpec((None, 2, KB, N1, C), lambda k, b: (order, 0, k, 0, 0)),
                  pl.BlockSpec((2 * N1, 2 * N1), lambda k, b: (0, 0)),
                  pl.BlockSpec((2 * N1, 2 * N1), lambda k, b: (0, 0))],
        out_specs=pl.BlockSpec((None, 2, KB, N1, C), lambda k, b: (b, 0, k, 0, 0)),
        out_shape=jax.ShapeDtypeStruct((B, 2, N2, N1, C), F32),
        compiler_params=params, name="hy_p2",
    )(bsp, hspec, f_fwd, f_inv)
    out = pl.pallas_call(
        _hy_p3_kernel,
        grid=(B, N1 // NB),
        in_specs=[pl.BlockSpec((None, 2, N2, NB, C), lambda b, n: (b, 0, 0, n, 0)),
                  pl.BlockSpec((None, H2 * NB, 2 * N2 * NB), lambda b, n: (n, 0, 0)),
                  pl.BlockSpec((None, H2, NB, C), lambda b, n: (b, 0, n, y_col)),
                  pl.BlockSpec((None, H2, NB, C), lambda b, n: (b, 0, n, xg_col)),
                  pl.BlockSpec((1, C), lambda b, n: (0, 0))],
        out_specs=pl.BlockSpec((None, H2, NB, C), lambda b, n: (b, 0, n, 0)),
        out_shape=jax.ShapeDtypeStruct((B, H2, N1, C), F32),
        compiler_params=params, name="hy_p3",
    )(csp, g_inv, y4, xg4, bias.reshape(1, C))
    return out.reshape(B * L, C)


def hyena_mixer(z, conv_w, conv_b, w1, b1, w2, b2, w3, freq, bias, B, L):
    zc = short_conv3(z.astype(F32), conv_w.astype(F32), conv_b.astype(F32)).reshape(B * L, -1)
    kern = hyena_filters(L, w1, b1, w2, b2, w3, freq)
    tables, tables_f32 = hyena_dft_tables(L)
    hspec = hyena_kernel_spectrum(kern, tables_f32)
    y = zc
    for o in range(HY_ORDER):
        y = hyena_long_conv(y, 0, zc, o + 1, hspec, o, bias[o].astype(F32), tables, B, L)
    return y


def rope_tables(L, rotary):
    if not rotary:
        return jnp.ones((L, GLA_DK), F32), jnp.zeros((L, GLA_DK), F32)
    t = jnp.arange(L)
    pos = jnp.stack([(t // GRID_W).astype(F32), (t % GRID_W).astype(F32)], axis=1)
    quarter = GLA_DK // 4
    inv = ROPE_BASE ** (-jnp.arange(quarter, dtype=F32) / quarter)
    ang = pos[:, :, None] * inv[None, None, :]
    cos = jnp.concatenate([jnp.cos(ang), jnp.cos(ang)], axis=-1).reshape(L, GLA_DK)
    sin = jnp.concatenate([-jnp.sin(ang), jnp.sin(ang)], axis=-1).reshape(L, GLA_DK)
    return cos, sin


def _gla_kernel(q_ref, k_ref, v_ref, g_ref, a_ref, cos_ref, sin_ref, w2_ref, b2_ref, ng_ref, s0_ref, o_ref, st_ref,
                *, n_chunks):
    C = GLA_CHUNK
    quarter = GLA_DK // 4
    lane = lax.broadcasted_iota(jnp.int32, (C, GLA_DK), 1)
    first_half = (lane % (2 * quarter)) < quarter
    row = lax.broadcasted_iota(jnp.int32, (C, C), 0)
    col = lax.broadcasted_iota(jnp.int32, (C, C), 1)
    tri = (row >= col, row <= col)
    o_ref[...] = jnp.zeros_like(o_ref)
    st_ref[...] = s0_ref[...]

    def rope(x, cos, sin):
        swapped = jnp.where(first_half, pltpu.roll(x, GLA_DK - quarter, 1), pltpu.roll(x, quarter, 1))
        return x * cos + swapped * sin

    def chunk(c, d):
        sl = pl.ds(pl.multiple_of(c * C, C), C)
        cos, sin = cos_ref[sl, :], sin_ref[sl, :]
        q = rope(q_ref[sl, :], cos, sin) * GLA_DK ** -0.5
        k = rope(k_ref[sl, :], cos, sin)
        v = v_ref[sl, :].astype(BF16)
        a = a_ref[sl, d * GLA_GATE_RANK:(d + 1) * GLA_GATE_RANK]
        pre = jnp.dot(a.astype(BF16), w2_ref[d].astype(BF16), preferred_element_type=F32) + b2_ref[d]
        log_a = jax.nn.log_sigmoid(pre) / GLA_TAU
        mask = tri[d]
        bcum = jnp.dot(mask.astype(F32), log_a, preferred_element_type=F32, precision=lax.Precision.HIGHEST)
        blast = jnp.sum(log_a, axis=0, keepdims=True)
        q_in = (q * jnp.exp(bcum)).astype(BF16)
        k_in = (k * jnp.exp(-bcum)).astype(BF16)
        k_st = (k * jnp.exp(blast - bcum)).astype(BF16)
        att = lax.dot_general(q_in, k_in, (((1,), (1,)), ((), ())), preferred_element_type=F32)
        att = jnp.where(mask, att, 0.0).astype(BF16)
        st = st_ref[d]
        o = jnp.dot(att, v, preferred_element_type=F32)
        o = o + lax.dot_general(q_in, st.astype(BF16), (((1,), (1,)), ((), ())), preferred_element_type=F32)
        o_ref[sl, :] += o
        kv_t = lax.dot_general(v, k_st, (((0,), (0,)), ((), ())), preferred_element_type=F32)
        st_ref[d] = st * jnp.exp(blast) + kv_t

    def body(i, carry):
        chunk(i, 0)
        chunk(n_chunks - 1 - i, 1)
        return carry

    lax.fori_loop(0, n_chunks, body, 0, unroll=2)
    o = o_ref[...]
    o = o * lax.rsqrt(jnp.mean(o * o, axis=-1, keepdims=True) + RMS_EPS) * ng_ref[...]
    o_ref[...] = o * jax.nn.silu(g_ref[...])


def gla_segment(z, za, s0, w2, b2, norm_g, B, L, rotary):
    H, dk, dv = GLA_HEADS, GLA_DK, GLA_DV
    cos, sin = rope_tables(L, rotary)
    qb, kb, vb, gb = IN_OFF[2] // dk, IN_OFF[3] // dk, IN_OFF[4] // dv, IN_OFF[5] // dv
    return pl.pallas_call(
        functools.partial(_gla_kernel, n_chunks=L // GLA_CHUNK),
        grid=(B, H),
        in_specs=[pl.BlockSpec((L, dk), lambda b, h: (b, qb + h)),
                  pl.BlockSpec((L, dk), lambda b, h: (b, kb + h)),
                  pl.BlockSpec((L, dv), lambda b, h: (b, vb + h)),
                  pl.BlockSpec((L, dv), lambda b, h: (b, gb + h)),
                  pl.BlockSpec((L, 2 * GLA_GATE_RANK), lambda b, h: (b, 0)),
                  pl.BlockSpec((L, dk), lambda b, h: (0, 0)),
                  pl.BlockSpec((L, dk), lambda b, h: (0, 0)),
                  pl.BlockSpec((2, GLA_GATE_RANK, dk), lambda b, h: (0, 0, h)),
                  pl.BlockSpec((2, 1, dk), lambda b, h: (0, 0, h)),
                  pl.BlockSpec((1, dv), lambda b, h: (0, 0)),
                  pl.BlockSpec((None, None, 2, dv, dk), lambda b, h: (b, h, 0, 0, 0))],
        out_specs=[pl.BlockSpec((L, dv), lambda b, h: (b, h)),
                   pl.BlockSpec((None, None, 2, dv, dk), lambda b, h: (b, h, 0, 0, 0))],
        out_shape=[jax.ShapeDtypeStruct((B * L, H * dv), F32), jax.ShapeDtypeStruct((B, H, 2, dv, dk), F32)],
        compiler_params=pltpu.CompilerParams(dimension_semantics=("arbitrary", "arbitrary"),
                                             vmem_limit_bytes=V7X_VMEM_LIMIT_BYTES),
        name="gla_segment",
    )(z, z, z, z, za, cos, sin, w2, b2.reshape(2, 1, -1), norm_g.reshape(1, dv), s0)


def gla_mixer(zl, zc, za_l, za_c, w2, b2, norm_g, B, L, Lc):
    s0 = jnp.zeros((B, GLA_HEADS, 2, GLA_DV, GLA_DK), F32)
    out_c, s_ctx = gla_segment(zc, za_c, s0, w2, b2, norm_g, B, Lc, False)
    out_l, _ = gla_segment(zl, za_l, s_ctx, w2, b2, norm_g, B, L, True)
    return out_l, out_c


def _cmul(a, b):
    return a[0] * b[0] - a[1] * b[1], a[0] * b[1] + a[1] * b[0]


S5_T = 16
S5_GQ = V7X_LANES // S5_GROUP
S5_NQ = S5_GROUPS // S5_GQ
S5_NS = 2 * S5_GQ * S5_STATE
S5_GP = V7X_LANES // S5_STATE
S5_NP = S5_GQ // S5_GP
S5_PU = S5_GP * S5_T * S5_GROUP
S5_PS = S5_GP * S5_STATE


def s5_operators(a_re, a_im, log_dt, b_re, b_im, c_re, c_im):
    T, GQ, NQ, NP, GP, P, I, G = S5_T, S5_GQ, S5_NQ, S5_NP, S5_GP, S5_STATE, S5_GROUP, S5_GROUPS
    hp = lax.Precision.HIGHEST
    TI = T * I
    lane = jnp.arange(TI)
    step_hot = (lane[None, :] // I == jnp.arange(T)[:, None]).astype(F32)
    chan_hot = (lane[None, :] % I == jnp.arange(I)[:, None]).astype(F32)
    steps = jnp.arange(T)
    same_chan = lane[:, None] % I == lane[None, :] % I
    dstep = lane[None, :] // I - lane[:, None] // I
    shift = [(same_chan[None] & (dstep[None] == steps[:, None, None])).astype(F32),
             (same_chan[None] & (dstep[None] + 2 * (lane[:, None] // I)[None] == steps[:, None, None])).astype(F32)]

    def pair_blocks(blocks, axis_rows, axis_cols):
        rows = []
        for g in range(GP):
            rows.append(jnp.concatenate([blocks[g] if h == g else jnp.zeros_like(blocks[g]) for h in range(GP)],
                                        axis=axis_cols))
        return jnp.concatenate(rows, axis=axis_rows)

    def split_groups(x):
        x = x.reshape(NQ, NP, GP, *x.shape[1:])
        return [x[:, :, g] for g in range(GP)]

    w_parts, v_parts, at_parts = [], [], []
    m_sum = 0.0
    for d in range(2):
        ar, ai = a_re[d].astype(F32), a_im[d].astype(F32)
        dt = jnp.exp(log_dt[d].astype(F32))[:, None]

        def power_rows(e):
            mag = jnp.exp((ar * dt)[:, None, :] * e[None, :, None])
            ph = (ai * dt)[:, None, :] * e[None, :, None]
            return mag * jnp.cos(ph), mag * jnp.sin(ph)

        def power_cols(e):
            pr, pi = power_rows(e)
            return (jnp.einsum('gtp,tc->gpc', pr, step_hot, precision=hp),
                    jnp.einsum('gtp,tc->gpc', pi, step_hot, precision=hp))

        def rows_power(e):
            return tuple(jnp.repeat(x, I, axis=1) for x in power_rows(e))

        a1 = power_rows(jnp.ones((1,), F32))
        zr, zi = a1[0][:, 0] - 1.0, a1[1][:, 0]
        den = ar * ar + ai * ai
        qr, qi = (zr * ar + zi * ai) / den, (zi * ar - zr * ai) / den
        bt = (b_re[d].astype(F32).transpose(0, 2, 1), b_im[d].astype(F32).transpose(0, 2, 1))
        bbar = _cmul((qr[:, None, :], qi[:, None, :]), bt)
        b_rows = tuple(jnp.tile(x, (1, T, 1)) for x in bbar)
        cm = (c_re[d].astype(F32).transpose(0, 2, 1), c_im[d].astype(F32).transpose(0, 2, 1))
        c_cols = tuple(jnp.einsum('gpi,ic->gpc', x, chan_hot, precision=hp) for x in cm)

        ex = (T - 1 - steps) if d == 0 else steps
        w = _cmul(rows_power(ex.astype(F32)), b_rows)
        wg = [split_groups(x) for x in w]
        w_parts.append(jnp.concatenate([pair_blocks(wg[0], 2, 3), pair_blocks(wg[1], 2, 3)], axis=3))
        ca = _cmul(power_cols(steps.astype(F32)), c_cols)
        kc = (jnp.einsum('gjp,gpc->gjc', bbar[0], ca[0], precision=hp)
              - jnp.einsum('gjp,gpc->gjc', bbar[1], ca[1], precision=hp))
        m_sum = m_sum + jnp.einsum('gjk,skc->gsjc', kc, shift[d], precision=hp).reshape(G, TI, TI)
        ey = (steps + 1) if d == 0 else (T - steps)
        v = _cmul(power_cols(ey.astype(F32)), c_cols)
        vg = [split_groups(x) for x in (v[0], -v[1])]
        v_parts.append(jnp.concatenate([pair_blocks(vg[0], 2, 3), pair_blocks(vg[1], 2, 3)], axis=2))
        at = power_rows(jnp.full((1,), float(T), F32))
        at_parts.append(jnp.concatenate([at[0].reshape(NQ, GQ * P), at[1].reshape(NQ, GQ * P)], axis=-1)[:, None, :])
    m_pair = pair_blocks(split_groups(m_sum), 2, 3)
    rhs = jnp.concatenate([w_parts[0], w_parts[1], m_pair], axis=-1).astype(BF16)
    return rhs, jnp.stack(v_parts).astype(BF16), jnp.stack(at_parts)


def s5_lane_permutation():
    r = jnp.arange(S5_T * V7X_LANES)
    s, g, j = r // V7X_LANES, (r % V7X_LANES) // S5_GROUP, r % S5_GROUP
    c = (g // S5_GP) * S5_PU + (g % S5_GP) * (S5_T * S5_GROUP) + s * S5_GROUP + j
    return (c[:, None] == r[None, :]).astype(BF16)


def _s5_in_kernel(u_ref, perm_ref, rhs_ref, s_ref, y_ref):
    half = S5_NS // 2
    u_nat = jnp.concatenate([u_ref[:, s, :] for s in range(S5_T)], axis=-1).astype(BF16)
    u = jnp.dot(u_nat, perm_ref[...], preferred_element_type=F32).astype(BF16)
    for n in range(S5_NP):
        r = jnp.dot(u[:, n * S5_PU:(n + 1) * S5_PU], rhs_ref[n], preferred_element_type=F32)
        for d in range(2):
            s_ref[d, :, n * S5_PS:(n + 1) * S5_PS] = r[:, 2 * d * S5_PS:(2 * d + 1) * S5_PS]
            s_ref[d, :, half + n * S5_PS:half + (n + 1) * S5_PS] = r[:, (2 * d + 1) * S5_PS:(2 * d + 2) * S5_PS]
        y_ref[:, n * S5_PU:(n + 1) * S5_PU] = r[:, 4 * S5_PS:]


def _s5_scan_kernel(s_ref, at_ref, x_ref, *, n_ctx, n_lat, batch):
    d = pl.program_id(0)
    half = S5_NS // 2
    a_r = at_ref[:, :half]
    a_i = at_ref[:, half:]

    def run(b, base, n, carry):
        def body(i, st):
            xr, xi = st
            c = i + d * (n - 1 - 2 * i)
            row = base + b * n + c
            x_ref[pl.ds(row, 1), :] = jnp.concatenate([xr, xi], axis=-1)
            s = s_ref[pl.ds(row, 1), :]
            return a_r * xr - a_i * xi + s[:, :half], a_r * xi + a_i * xr + s[:, half:]
        return lax.fori_loop(0, n, body, carry)

    for b in range(batch):
        zero = jnp.zeros((1, half), F32)
        st = run(b, 0, n_ctx, (zero, zero))
        run(b, batch * n_ctx, n_lat, st)


def _s5_out_kernel(x_ref, v_ref, yin_ref, u_ref, perm_ref, dsk_ref, y_ref):
    half = S5_NS // 2
    parts = []
    for n in range(S5_NP):
        y = yin_ref[:, n * S5_PU:(n + 1) * S5_PU]
        for d in range(2):
            x = jnp.concatenate([x_ref[d, :, n * S5_PS:(n + 1) * S5_PS],
                                 x_ref[d, :, half + n * S5_PS:half + (n + 1) * S5_PS]], axis=-1)
            y = y + jnp.dot(x.astype(BF16), v_ref[d, n], preferred_element_type=F32)
        parts.append(y)
    y = jnp.concatenate(parts, axis=-1)
    y_hi = y.astype(BF16)
    y_lo = (y - y_hi.astype(F32)).astype(BF16)
    nt = (((1,), (1,)), ((), ()))
    y = (lax.dot_general(y_hi, perm_ref[...], nt, preferred_element_type=F32)
         + lax.dot_general(y_lo, perm_ref[...], nt, preferred_element_type=F32))
    for t in range(S5_T):
        y_ref[:, t, :] = y[:, t * V7X_LANES:(t + 1) * V7X_LANES] + dsk_ref[...] * u_ref[:, t, :]


def s5_core(uc, ul, ops, d_skip, B, L, Lc):
    rhs, vmat, a_t = ops
    T, NQ, NS, NP = S5_T, S5_NQ, S5_NS, S5_NP
    n_ctx, n_lat = Lc // T, L // T
    R = B * (n_ctx + n_lat)
    RT = R // 2
    tw = T * V7X_LANES
    assert RT % V7X_SUBLANES == 0
    u = jnp.concatenate([uc.reshape(B * n_ctx, T, S5_WIDTH), ul.reshape(B * n_lat, T, S5_WIDTH)], axis=0)
    perm = s5_lane_permutation()
    params = pltpu.CompilerParams(dimension_semantics=("arbitrary", "arbitrary"),
                                  vmem_limit_bytes=V7X_VMEM_LIMIT_BYTES)
    perm_spec = pl.BlockSpec((tw, tw), lambda q, i: (0, 0), pipeline_mode=pl.Buffered(1))
    s, y_in = pl.pallas_call(
        _s5_in_kernel,
        grid=(NQ, R // RT),
        in_specs=[pl.BlockSpec((RT, T, V7X_LANES), lambda q, i: (i, 0, q)),
                  perm_spec,
                  pl.BlockSpec((None, NP, S5_PU, 4 * S5_PS + S5_PU), lambda q, i: (q, 0, 0, 0))],
        out_specs=[pl.BlockSpec((2, None, RT, NS), lambda q, i: (0, q, i, 0)),
                   pl.BlockSpec((None, RT, tw), lambda q, i: (q, i, 0))],
        out_shape=[jax.ShapeDtypeStruct((2, NQ, R, NS), F32), jax.ShapeDtypeStruct((NQ, R, tw), F32)],
        compiler_params=params,
        name="s5_in",
    )(u, perm, rhs)
    x = pl.pallas_call(
        functools.partial(_s5_scan_kernel, n_ctx=n_ctx, n_lat=n_lat, batch=B),
        grid=(2, NQ),
        in_specs=[pl.BlockSpec((None, None, R, NS), lambda d, q: (d, q, 0, 0)),
                  pl.BlockSpec((None, None, 1, NS), lambda d, q: (d, q, 0, 0))],
        out_specs=pl.BlockSpec((None, None, R, NS), lambda d, q: (d, q, 0, 0)),
        out_shape=jax.ShapeDtypeStruct((2, NQ, R, NS), F32),
        compiler_params=pltpu.CompilerParams(dimension_semantics=("arbitrary", "arbitrary")),
        name="s5_scan",
    )(s, a_t)
    y = pl.pallas_call(
        _s5_out_kernel,
        grid=(NQ, R // RT),
        in_specs=[pl.BlockSpec((2, None, RT, NS), lambda q, i: (0, q, i, 0)),
                  pl.BlockSpec((2, None, NP, 2 * S5_PS, S5_PU), lambda q, i: (0, q, 0, 0, 0)),
                  pl.BlockSpec((None, RT, tw), lambda q, i: (q, i, 0)),
                  pl.BlockSpec((RT, T, V7X_LANES), lambda q, i: (i, 0, q)),
                  perm_spec,
                  pl.BlockSpec((1, V7X_LANES), lambda q, i: (0, q))],
        out_specs=pl.BlockSpec((RT, T, V7X_LANES), lambda q, i: (i, 0, q)),
        out_shape=jax.ShapeDtypeStruct((R, T, S5_WIDTH), F32),
        compiler_params=params,
        name="s5_out",
    )(x, vmat, y_in, u, perm, d_skip.astype(F32).reshape(1, S5_WIDTH))
    y = y.reshape(R * T, S5_WIDTH)
    return y[:B * Lc], y[B * Lc:]


def s5_glu(y, w, b, l):
    y = jax.nn.gelu(y)
    return (y * jax.nn.sigmoid(mm(y, w, (l,)) + b[l].astype(F32))).astype(BF16)


def _gate_merge_kernel(h_ref, b0_ref, b1_ref, b2_ref, b3_ref, wg0, wg1, wg2, wg3, wb0, wb1, wb2, wb3, bias_ref,
                       o_ref, wgb_ref, wbb_ref):
    wg_refs = (wg0, wg1, wg2, wg3)
    wb_refs = (wb0, wb1, wb2, wb3)
    br_refs = (b0_ref, b1_ref, b2_ref, b3_ref)

    @pl.when(pl.program_id(1) == 0)
    def _():
        for k in range(N_BRANCH):
            wgb_ref[k] = wg_refs[k][...].astype(BF16)
            wbb_ref[k] = wb_refs[k][...].astype(BF16)

    h = h_ref[...]
    acc = None
    for k in range(N_BRANCH):
        gate = jax.nn.sigmoid(jnp.dot(h, wgb_ref[k], preferred_element_type=F32) + bias_ref[k])
        term = gate * jnp.dot(br_refs[k][...], wbb_ref[k], preferred_element_type=F32)
        acc = term if acc is None else acc + term
    o_ref[...] = acc.astype(o_ref.dtype)


def gate_merge(h, branches, w_gate, b_gate, w_branch, l):
    M, D = h.shape
    W = branches[0].shape[1]
    tn = 256
    tm = min(M, 256)

    def wspec(k, rows):
        return pl.BlockSpec((None, None, rows, tn), lambda j, i: (l, k, 0, j), pipeline_mode=pl.Buffered(1))

    return pl.pallas_call(
        _gate_merge_kernel,
        grid=(D // tn, M // tm),
        in_specs=[pl.BlockSpec((tm, D), lambda j, i: (i, 0))]
                 + [pl.BlockSpec((tm, W), lambda j, i: (i, 0)) for _ in range(N_BRANCH)]
                 + [wspec(k, D) for k in range(N_BRANCH)]
                 + [wspec(k, W) for k in range(N_BRANCH)]
                 + [pl.BlockSpec((None, N_BRANCH, 1, tn), lambda j, i: (l, 0, 0, j))],
        out_specs=pl.BlockSpec((tm, tn), lambda j, i: (i, j)),
        out_shape=jax.ShapeDtypeStruct((M, D), BF16),
        scratch_shapes=[pltpu.VMEM((N_BRANCH, D, tn), BF16), pltpu.VMEM((N_BRANCH, W, tn), BF16)],
        compiler_params=pltpu.CompilerParams(dimension_semantics=("arbitrary", "arbitrary"),
                                             vmem_limit_bytes=V7X_VMEM_LIMIT_BYTES),
        name="gate_merge",
    )(h, *branches, *([w_gate] * N_BRANCH), *([w_branch] * N_BRANCH),
      b_gate.reshape(b_gate.shape[0], N_BRANCH, 1, D))


def _moe_gather_kernel(idx_ref, h_ref, o_ref, stage_ref, sem, *, n_tokens, n_experts):
    g = pl.program_id(0)
    cap = o_ref.shape[0]
    b = g // n_experts

    def row_copy(c):
        row = b * n_tokens + idx_ref[g * cap + c]
        return pltpu.make_async_copy(h_ref.at[pl.ds(row, 1), :], stage_ref.at[pl.ds(c, 1), :], sem)

    def start(c, carry):
        row_copy(c).start()
        return carry

    def wait(c, carry):
        row_copy(c).wait()
        return carry

    lax.fori_loop(0, cap, start, 0)
    lax.fori_loop(0, cap, wait, 0)
    o_ref[...] = stage_ref[...].astype(BF16)


def moe_gather(h, idx):
    B, N, D = h.shape
    _, E, cap = idx.shape
    return pl.pallas_call(
        functools.partial(_moe_gather_kernel, n_tokens=N, n_experts=E),
        grid_spec=pltpu.PrefetchScalarGridSpec(
            num_scalar_prefetch=1,
            grid=(B * E,),
            in_specs=[pl.BlockSpec(memory_space=pl.ANY)],
            out_specs=pl.BlockSpec((None, cap, D), lambda g, idx_ref: (g, 0, 0)),
            scratch_shapes=[pltpu.VMEM((cap, D), F32), pltpu.SemaphoreType.DMA(())],
        ),
        out_shape=jax.ShapeDtypeStruct((B * E, cap, D), BF16),
        compiler_params=pltpu.CompilerParams(dimension_semantics=("arbitrary",),
                                             vmem_limit_bytes=V7X_VMEM_LIMIT_BYTES),
        name="moe_gather",
    )(idx.reshape(-1).astype(jnp.int32), h.reshape(B * N, D)).reshape(B, E, cap, D)


def _expert_gu_kernel(x_ref, wg_ref, wu_ref, o_ref):
    nb, cap, d = x_ref.shape
    x = x_ref[...].reshape(nb * cap, d)
    g = jnp.dot(x, wg_ref[...].astype(BF16), preferred_element_type=F32)
    u = jnp.dot(x, wu_ref[...].astype(BF16), preferred_element_type=F32)
    o_ref[...] = (jax.nn.silu(g) * u).astype(BF16).reshape(o_ref.shape)


def expert_gu(xs, w_gu, l):
    B, E, C, D = xs.shape
    F = w_gu.shape[3] // 2
    tn = 256
    nj = F // tn
    return pl.pallas_call(
        _expert_gu_kernel,
        grid=(E, nj),
        in_specs=[pl.BlockSpec((B, None, C, D), lambda e, j: (0, e, 0, 0)),
                  pl.BlockSpec((None, None, D, tn), lambda e, j: (l, e, 0, j)),
                  pl.BlockSpec((None, None, D, tn), lambda e, j: (l, e, 0, nj + j))],
        out_specs=pl.BlockSpec((B, None, C, tn), lambda e, j: (0, e, 0, j)),
        out_shape=jax.ShapeDtypeStruct((B, E, C, F), BF16),
        compiler_params=pltpu.CompilerParams(dimension_semantics=("arbitrary",) * 2,
                                             vmem_limit_bytes=V7X_VMEM_LIMIT_BYTES),
        name="expert_gu",
    )(xs, w_gu, w_gu)


def _expert_down_kernel(idx_ref, a_ref, w_ref, gate_ref, o_ref, y_ref, *, n_experts):
    b, e = pl.program_id(0), pl.program_id(2)
    cap = a_ref.shape[0]

    @pl.when(e == 0)
    def _():
        o_ref[...] = jnp.zeros_like(o_ref)

    y = jnp.dot(a_ref[...], w_ref[...].astype(BF16), preferred_element_type=F32)
    y_ref[...] = y * gate_ref[...]
    base = (b * n_experts + e) * cap

    def add_row(r, carry):
        tok = idx_ref[base + r]
        o_ref[pl.ds(tok, 1), :] += y_ref[pl.ds(r, 1), :]
        return carry

    lax.fori_loop(0, cap, add_row, 0, unroll=8)


def expert_down(act, w_down, gate, idx, n_tokens, l):
    B, E, C, F = act.shape
    D = w_down.shape[3]
    tn = min(1024, D)
    assert D % tn == 0
    return pl.pallas_call(
        functools.partial(_expert_down_kernel, n_experts=E),
        grid_spec=pltpu.PrefetchScalarGridSpec(
            num_scalar_prefetch=1,
            grid=(B, D // tn, E),
            in_specs=[pl.BlockSpec((None, None, C, F), lambda b, j, e, idx_ref: (b, e, 0, 0)),
                      pl.BlockSpec((None, None, F, tn), lambda b, j, e, idx_ref: (l, e, 0, j)),
                      pl.BlockSpec((None, None, C, 1), lambda b, j, e, idx_ref: (b, e, 0, 0))],
            out_specs=pl.BlockSpec((None, n_tokens, tn), lambda b, j, e, idx_ref: (b, 0, j)),
            scratch_shapes=[pltpu.VMEM((C, tn), F32)],
        ),
        out_shape=jax.ShapeDtypeStruct((B, n_tokens, D), F32),
        compiler_params=pltpu.CompilerParams(dimension_semantics=("arbitrary",) * 3,
                                             vmem_limit_bytes=V7X_VMEM_LIMIT_BYTES),
        name="expert_down",
    )(idx.reshape(-1).astype(jnp.int32), act, w_down, gate[..., None])


def moe_ec(h, router_w, w_gu, w_down, l):
    B, N, D = h.shape
    cap = max(1, EC_CAPACITY * N // N_EXPERTS)
    logits = jnp.einsum('bnd,de->bne', h, router_w[l], precision=lax.Precision.HIGHEST)
    aff = jax.nn.softmax(logits.astype(F32), axis=-1)
    gate, idx = lax.top_k(jnp.swapaxes(aff, 1, 2), cap)
    return expert_down(expert_gu(moe_gather(h, idx), w_gu, l), w_down, gate, idx, N, l)


def kernel(x, c, ctx, c_ctx, ada_w, ada_b, mix_pre_g, mix_post_g, ffn_pre_g, ffn_post_g, w_in, na_rpb,
           hy_conv_w, hy_conv_b, hy_w1, hy_b1, hy_w2, hy_b2, hy_w3, hy_freq, hy_bias,
           gla_w2, gla_b2, gla_norm_g, s5_a_re, s5_a_im, s5_log_dt, s5_b_re, s5_b_im, s5_c_re, s5_c_im,
           s5_d, s5_glu_w, s5_glu_b, w_branch, w_gate, b_gate, w_out, router_w, ex_w_gu, ex_w_down):
    xl, xc = x, ctx
    B, L, Lc = x.shape[0], x.shape[1], ctx.shape[1]
    for l in range(DEPTH):
        need_ctx = l < DEPTH - 1
        cvecs = jnp.concatenate([c, c_ctx[None, :]], axis=0)
        mod = mm(jax.nn.silu(cvecs), ada_w, (l,)) + ada_b[l]
        ml = [p[:, None, :] for p in jnp.split(mod[:B], 6, axis=-1)]
        mc = jnp.split(mod[B], 6, axis=-1)

        hl = rms_norm(xl, mix_pre_g[l]) * (1.0 + ml[1]) + ml[0]
        hc = rms_norm(xc, mix_pre_g[l]) * (1.0 + mc[1]) + mc[0]
        hl2, hc2 = hl.reshape(B * L, D_MODEL).astype(BF16), hc.reshape(B * Lc, D_MODEL).astype(BF16)
        w_a, w_s5 = w_in[l, :, IN_MAIN:IN_MAIN + 2 * GLA_GATE_RANK], w_in[l, :, IN_MAIN + 2 * GLA_GATE_RANK:]
        zl, zc = mm(hl2, w_in, (l,), IN_MAIN), mm(hc2, w_in, (l,), IN_MAIN)
        zl3, zc3 = zl.reshape(B, L, IN_MAIN), zc.reshape(B, Lc, IN_MAIN)

        na_l = na_latent(zl, zc, na_bias_table(na_rpb[l], L // GRID_W), B, L, Lc)
        hy_args = (hy_conv_w[l], hy_conv_b[l], hy_w1[l], hy_b1[l], hy_w2[l], hy_b2[l], hy_w3[l], hy_freq[l], hy_bias[l])
        hy_l = hyena_mixer(zl3[..., IN_OFF[1]:IN_OFF[2]], *hy_args, B, L).astype(BF16)
        gla_l, gla_c = gla_mixer(zl, zc, mm(hl2, w_a), mm(hc2, w_a), gla_w2[l], gla_b2[l], gla_norm_g[l], B, L, Lc)
        s5_ops = s5_operators(s5_a_re[l], s5_a_im[l], s5_log_dt[l], s5_b_re[l], s5_b_im[l], s5_c_re[l], s5_c_im[l])
        s5_yc, s5_yl = s5_core(mm(hc2, w_s5), mm(hl2, w_s5), s5_ops, s5_d[l], B, L, Lc)
        s5_l = s5_glu(s5_yl, s5_glu_w, s5_glu_b, l)
        merged = gate_merge(hl2, (na_l, hy_l, gla_l.astype(BF16), s5_l), w_gate, b_gate, w_branch, l)
        yl = mm(merged, w_out, (l,)).reshape(B, L, D_MODEL)
        xl = xl + ml[2] * rms_norm(yl, mix_post_g[l])
        if need_ctx:
            na_c = na_context(zc, B, Lc).astype(BF16)
            hy_c = hyena_mixer(zc3[..., IN_OFF[1]:IN_OFF[2]], *hy_args, B, Lc).astype(BF16)
            s5_c = s5_glu(s5_yc, s5_glu_w, s5_glu_b, l)
            merged = gate_merge(hc2, (na_c, hy_c, gla_c.astype(BF16), s5_c), w_gate, b_gate, w_branch, l)
            yc = mm(merged, w_out, (l,)).reshape(B, Lc, D_MODEL)
            xc = xc + mc[2] * rms_norm(yc, mix_post_g[l])

        hl = rms_norm(xl, ffn_pre_g[l]) * (1.0 + ml[4]) + ml[3]
        xl = xl + ml[5] * rms_norm(moe_ec(hl, router_w, ex_w_gu, ex_w_down, l), ffn_post_g[l])
        if need_ctx:
            hc = rms_norm(xc, ffn_pre_g[l]) * (1.0 + mc[4]) + mc[3]
            xc = xc + mc[5] * rms_norm(moe_ec(hc, router_w, ex_w_gu, ex_w_down, l), ffn_post_g[l])
    return xl
```

```python
import functools
import math

import jax
import jax.numpy as jnp
import numpy as np
from jax import lax
from jax.experimental import pallas as pl
from jax.experimental.pallas import tpu as pltpu

D_MODEL = 4096
BATCH = 2
SEQ = 4096
DEPTH = 2
CTX_LEN = 256
GRID_W = 64

NA_HEADS = 8
NA_HEAD_DIM = 128
NA_WIDTH = NA_HEADS * NA_HEAD_DIM
NA_WIN_R = 8
NA_WIN_C = 16

HY_WIDTH = 1024
HY_ORDER = 2
HY_BANDS = 16
HY_POS_DIM = 1 + 2 * HY_BANDS
HY_FILTER_HIDDEN = 64
HY_DECAY_TARGET = 1e-2
HY_FAST_DECAY = 0.3
HY_SLOW_DECAY = 1.5

GLA_HEADS = 4
GLA_DK = 128
GLA_DV = 256
GLA_QK_WIDTH = GLA_HEADS * GLA_DK
GLA_V_WIDTH = GLA_HEADS * GLA_DV
GLA_GATE_RANK = 16
GLA_TAU = 16.0
GLA_CHUNK = 64

S5_WIDTH = 1024
S5_GROUP = 16
S5_GROUPS = S5_WIDTH // S5_GROUP
S5_STATE = 64

N_BRANCH = 4
BRANCH_WIDTH = 1024
N_EXPERTS = 16
EXPERT_FF = 1024
EC_CAPACITY = 2

ROPE_BASE = 10000.0
RMS_EPS = 1e-6
NEG_INF = -1e30
F32 = jnp.float32
BF16 = jnp.bfloat16

IN_SPLITS = (3 * NA_WIDTH, (HY_ORDER + 1) * HY_WIDTH, GLA_QK_WIDTH, GLA_QK_WIDTH, GLA_V_WIDTH, GLA_V_WIDTH,
             2 * GLA_GATE_RANK, S5_WIDTH)
IN_COLS = sum(IN_SPLITS)
IN_OFF = tuple(int(v) for v in np.concatenate([[0], np.cumsum(IN_SPLITS)]))
IN_MAIN = IN_OFF[6]

V7X_LANES = 128
V7X_SUBLANES = 8
V7X_VMEM_LIMIT_BYTES = 56 * 1024 * 1024


def _mm_kernel(x_ref, w_ref, o_ref, wb_ref):
    @pl.when(pl.program_id(1) == 0)
    def _():
        wb_ref[...] = w_ref[...].astype(BF16)

    o_ref[...] = jnp.dot(x_ref[...].astype(BF16), wb_ref[...], preferred_element_type=F32).astype(o_ref.dtype)


def _mm_tiles(M, K, N):
    tm = M if M <= 1024 else 1024
    tn = 512 if K > 1024 else 1024
    tn = min(tn, N)
    return tm, tn


def mm(x, w, lead=(), ncols=None, out_dtype=F32):
    M, K = x.shape
    nw = w.shape[-1]
    N = nw if ncols is None else ncols
    pad = (-M) % V7X_SUBLANES
    if pad:
        x = jnp.pad(x, ((0, pad), (0, 0)))
    Mp = M + pad
    tm, tn = _mm_tiles(Mp, K, N)
    assert Mp % tm == 0 and (N == nw or N % tn == 0) and w.shape[-2] == K and w.ndim == len(lead) + 2
    out = pl.pallas_call(
        _mm_kernel,
        grid=(pl.cdiv(N, tn), Mp // tm),
        in_specs=[pl.BlockSpec((tm, K), lambda j, i: (i, 0)),
                  pl.BlockSpec((None,) * len(lead) + (K, tn), lambda j, i: tuple(lead) + (0, j))],
        out_specs=pl.BlockSpec((tm, tn), lambda j, i: (i, j)),
        out_shape=jax.ShapeDtypeStruct((Mp, N), out_dtype),
        scratch_shapes=[pltpu.VMEM((K, tn), BF16)],
        compiler_params=pltpu.CompilerParams(dimension_semantics=("arbitrary", "arbitrary"),
                                             vmem_limit_bytes=V7X_VMEM_LIMIT_BYTES),
        name="mm",
    )(x, w)
    return out[:M] if pad else out


def rms_norm(x, g):
    xf = x.astype(F32)
    y = xf * lax.rsqrt(jnp.mean(xf * xf, axis=-1, keepdims=True) + RMS_EPS)
    return (y * g.astype(F32)).astype(x.dtype)


def dense_attention(q, k, v):
    scale = q.shape[-1] ** -0.5
    s = jnp.einsum('bqhd,bkhd->bhqk', q, k, preferred_element_type=F32) * scale
    p = jax.nn.softmax(s, axis=-1).astype(v.dtype)
    o = jnp.einsum('bhqk,bkhd->bqhd', p, v)
    return o.reshape(q.shape[0], q.shape[1], -1)


NA_TILE_ROWS = 8


def na_bias_table(rpb, rows):
    nt = rows // NA_TILE_ROWS
    cases = jnp.array([0, min(1, nt - 1), nt - 1])
    rq = jnp.arange(NA_TILE_ROWS)
    rk = jnp.arange(2 * NA_TILE_ROWS)
    c = jnp.arange(GRID_W)
    r = cases[:, None] * NA_TILE_ROWS + rq[None, :]
    ks = jnp.clip(cases * NA_TILE_ROWS - NA_WIN_R // 2, 0, rows - 2 * NA_TILE_ROWS)
    krow = ks[:, None] + rk[None, :]
    ws = jnp.clip(r - NA_WIN_R // 2, 0, rows - NA_WIN_R)
    row_ok = (krow[:, None, :] >= ws[:, :, None]) & (krow[:, None, :] < ws[:, :, None] + NA_WIN_R)
    row_off = jnp.clip(krow[:, None, :] - r[:, :, None] + (NA_WIN_R - 1), 0, 2 * NA_WIN_R - 2)
    col_start = jnp.clip(c - NA_WIN_C // 2, 0, GRID_W - NA_WIN_C)
    col_ok = (c[None, :] >= col_start[:, None]) & (c[None, :] < col_start[:, None] + NA_WIN_C)
    col_off = jnp.clip(c[None, :] - c[:, None] + (NA_WIN_C - 1), 0, 2 * NA_WIN_C - 2)
    hp = lax.Precision.HIGHEST
    row_hot = (row_off[..., None] == jnp.arange(2 * NA_WIN_R - 1)).astype(F32)
    col_hot = (col_off[..., None] == jnp.arange(2 * NA_WIN_C - 1)).astype(F32)
    b = jnp.einsum('hab,crka->hcrkb', rpb.astype(F32), row_hot, precision=hp)
    b = jnp.einsum('hcrkb,qpb->hcrqkp', b, col_hot, precision=hp)
    ok = row_ok[:, :, None, :, None] & col_ok[None, None, :, None, :]
    b = jnp.where(ok[None], b, NEG_INF).transpose(1, 0, 2, 3, 4, 5)
    return b.reshape(3, rpb.shape[0], NA_TILE_ROWS * GRID_W, 2 * NA_TILE_ROWS * GRID_W).astype(BF16)


def _na_kernel(q_ref, k_ref, v_ref, kc_ref, vc_ref, b_ref, o_ref, *, scale):
    q = q_ref[...].astype(BF16)
    nk = k_ref.shape[0]
    k = jnp.concatenate([k_ref[...].astype(BF16), kc_ref[...].astype(BF16)], axis=0)
    v = jnp.concatenate([v_ref[...].astype(BF16), vc_ref[...].astype(BF16)], axis=0)
    s = lax.dot_general(q, k, (((1,), (1,)), ((), ())), preferred_element_type=F32) * scale
    bias = jnp.concatenate([b_ref[...].astype(F32), jnp.zeros((s.shape[0], s.shape[1] - nk), F32)], axis=1)
    s = s + bias
    m = jnp.max(s, axis=-1, keepdims=True)
    p = jnp.exp(s - m)
    den = jnp.sum(p, axis=-1, keepdims=True)
    o = jnp.dot(p.astype(BF16), v, preferred_element_type=F32)
    o_ref[...] = (o / den).astype(o_ref.dtype)


def na_latent(zl, zc, bias_tab, B, L, Lc):
    rows = L // GRID_W
    nt = rows // NA_TILE_ROWS
    tq = NA_TILE_ROWS * GRID_W
    tk = 2 * tq
    H, dh = NA_HEADS, NA_HEAD_DIM

    def case(j):
        return jnp.where(j == 0, 0, jnp.where(j == nt - 1, 2, 1))

    def kstart(b, j):
        ks = jnp.clip(j * NA_TILE_ROWS - NA_WIN_R // 2, 0, rows - 2 * NA_TILE_ROWS)
        return pl.multiple_of(b * L + ks * GRID_W, (NA_WIN_R // 2) * GRID_W)

    def kv_spec(col0):
        return pl.BlockSpec((pl.Element(tk), pl.Element(dh)),
                            lambda b, j, h: (kstart(b, j), pl.multiple_of((col0 + h) * dh, dh)))

    return pl.pallas_call(
        functools.partial(_na_kernel, scale=dh ** -0.5),
        grid=(B, nt, H),
        in_specs=[
            pl.BlockSpec((tq, dh), lambda b, j, h: (b * nt + j, h)),
            kv_spec(H),
            kv_spec(2 * H),
            pl.BlockSpec((Lc, dh), lambda b, j, h: (b, H + h)),
            pl.BlockSpec((Lc, dh), lambda b, j, h: (b, 2 * H + h)),
            pl.BlockSpec((None, None, tq, tk), lambda b, j, h: (case(j), h, 0, 0)),
        ],
        out_specs=pl.BlockSpec((tq, dh), lambda b, j, h: (b * nt + j, h)),
        out_shape=jax.ShapeDtypeStruct((B * L, H * dh), BF16),
        compiler_params=pltpu.CompilerParams(dimension_semantics=("arbitrary",) * 3),
        name="na_latent",
    )(zl, zl, zl, zc, zc, bias_tab)


def na_context(zc, B, Lc):
    qc, kc, vc = [t.reshape(B, Lc, NA_HEADS, NA_HEAD_DIM) for t in jnp.split(zc[:, :3 * NA_WIDTH], 3, axis=-1)]
    return dense_attention(qc, kc, vc).reshape(B * Lc, NA_WIDTH)


def short_conv3(u, w, b):
    up = jnp.pad(u, ((0, 0), (1, 1), (0, 0)))
    return up[:, :-2] * w[0] + up[:, 1:-1] * w[1] + up[:, 2:] * w[2] + b


def hyena_filters(L, w1, b1, w2, b2, w3, freq):
    r = jnp.arange(2 * L)
    lag = jnp.where(r < L, r, 2 * L - r)
    t = jnp.minimum(lag, L - 1).astype(F32)
    t01 = t / max(L - 1, 1)
    bands = jnp.linspace(1e-4, HY_BANDS - 1, HY_BANDS, dtype=F32)
    ang = (2.0 * math.pi / L) * t[:, None] * bands[None, :]
    z = jnp.concatenate([t01[:, None], jnp.cos(ang), -jnp.sin(ang)], axis=-1)
    f = freq.astype(F32)
    h = jnp.sin(f * (z @ w1.astype(F32) + b1.astype(F32)))
    h = jnp.sin(f * (h @ w2.astype(F32) + b2.astype(F32)))
    w3d = w3.astype(F32).reshape(-1, 2, HY_ORDER * HY_WIDTH)
    h = jnp.where((r < L)[:, None], h @ w3d[:, 0], h @ w3d[:, 1]).reshape(2 * L, HY_ORDER, HY_WIDTH)
    deltas = jnp.abs(jnp.linspace(math.log(HY_DECAY_TARGET) / HY_SLOW_DECAY, math.log(HY_DECAY_TARGET) / HY_FAST_DECAY,
                                  HY_WIDTH, dtype=F32))
    decay = jnp.where((r != L)[:, None], jnp.exp(-t01[:, None] * deltas[None, :]), 0.0)
    kern = h * decay[:, None, :]
    return kern / jnp.sum(jnp.abs(kern), axis=0, keepdims=True)


HY_N1 = 64
HY_KB = 4


def _cis(num, den):
    ang = (2.0 * math.pi / den) * (num % den).astype(F32)
    return jnp.cos(ang), jnp.sin(ang)


def hyena_dft_tables(L):
    N, N1 = 2 * L, HY_N1
    N2 = N // N1
    NB = V7X_SUBLANES
    n1 = jnp.arange(N1)[:, None, None]
    k2 = jnp.arange(N2)[None, :, None]
    n2 = jnp.arange(N2)[None, None, :]
    c, s = _cis(n1 * k2 + n2 * k2 * N1, N)
    g_full = jnp.concatenate([c, -s], axis=1)
    row = jnp.arange(2 * N2 * NB)[None, :, None]
    col = jnp.arange(N2 // 2 * NB)[None, None, :]
    blk = jnp.arange(N1 // NB)[:, None, None]
    r_im, r_k2, r_i = row // (N2 * NB), (row // NB) % N2, row % NB
    c_n2, c_i = col // NB, col % NB
    bc, bs = _cis((blk * NB + r_i) * r_k2 + c_n2 * r_k2 * N1, N)
    g_fwd = jnp.where(r_i == c_i, jnp.where(r_im == 0, bc, -bs), 0.0)
    g_inv = g_fwd.transpose(0, 2, 1) / N
    a = jnp.arange(N1)
    fc, fs = _cis(a[:, None] * a[None, :], N1)
    f_fwd = jnp.concatenate([jnp.concatenate([fc, fs], axis=1), jnp.concatenate([-fs, fc], axis=1)], axis=0)
    f_inv = jnp.concatenate([jnp.concatenate([fc, -fs], axis=1), jnp.concatenate([fs, fc], axis=1)], axis=0)
    return (g_fwd.astype(BF16), f_fwd.astype(BF16), f_inv.astype(BF16), g_inv.astype(BF16)), (g_full, f_fwd)


def _split_bf16(x):
    hi = x.astype(BF16)
    return hi, (x - hi.astype(F32)).astype(BF16)


def _dot_hi_lo(a, b):
    a_hi, a_lo = _split_bf16(a)
    b_hi, b_lo = _split_bf16(b)
    return (jnp.dot(a_hi, b_hi, preferred_element_type=F32) + jnp.dot(a_hi, b_lo, preferred_element_type=F32)
            + jnp.dot(a_lo, b_hi, preferred_element_type=F32))


def _hy_s1_kernel(x_ref, g_ref, o_ref):
    n2 = g_ref.shape[1] // 2
    for i in range(x_ref.shape[1]):
        r = _dot_hi_lo(g_ref[i], x_ref[:, i, :])
        o_ref[0, :, i, :] = r[:n2]
        o_ref[1, :, i, :] = r[n2:]


def _hy_s2_kernel(b_ref, f_ref, o_ref):
    _, kb, n1, c_dim = b_ref.shape
    for k in range(kb):
        x = _dot_hi_lo(f_ref[...], b_ref[:, k].reshape(2 * n1, c_dim))
        o_ref[0, k] = x[:n1]
        o_ref[1, k] = x[n1:]


def hyena_kernel_spectrum(kern, tables_f32):
    g_full, f_fwd = tables_f32
    N, O, C = kern.shape
    N1 = HY_N1
    N2 = N // N1
    NB = V7X_SUBLANES
    KB = HY_KB
    params = pltpu.CompilerParams(dimension_semantics=("arbitrary", "arbitrary"),
                                  vmem_limit_bytes=V7X_VMEM_LIMIT_BYTES)
    bsp = pl.pallas_call(
        _hy_s1_kernel,
        grid=(O, N1 // NB),
        in_specs=[pl.BlockSpec((N2, NB, C), lambda o, n: (0, n, o)),
                  pl.BlockSpec((NB, 2 * N2, N2), lambda o, n: (n, 0, 0))],
        out_specs=pl.BlockSpec((None, 2, N2, NB, C), lambda o, n: (o, 0, 0, n, 0)),
        out_shape=jax.ShapeDtypeStruct((O, 2, N2, N1, C), F32),
        compiler_params=params, name="hy_s1",
    )(kern.reshape(N2, N1, O * C), g_full)
    return pl.pallas_call(
        _hy_s2_kernel,
        grid=(O, N2 // KB),
        in_specs=[pl.BlockSpec((None, 2, KB, N1, C), lambda o, k: (o, 0, k, 0, 0)),
                  pl.BlockSpec((2 * N1, 2 * N1), lambda o, k: (0, 0))],
        out_specs=pl.BlockSpec((None, 2, KB, N1, C), lambda o, k: (o, 0, k, 0, 0)),
        out_shape=jax.ShapeDtypeStruct((O, 2, N2, N1, C), F32),
        compiler_params=params, name="hy_s2",
    )(bsp, f_fwd)


def _hy_p1_kernel(x_ref, g_ref, o_ref):
    h2, nb, c = x_ref.shape
    r = jnp.dot(g_ref[...], x_ref[...].reshape(h2 * nb, c).astype(BF16), preferred_element_type=F32)
    o_ref[...] = r.reshape(o_ref.shape)


def _hy_p2_kernel(b_ref, h_ref, ff_ref, fi_ref, o_ref):
    _, kb, n1, c_dim = b_ref.shape
    for k in range(kb):
        b = b_ref[:, k].reshape(2 * n1, c_dim).astype(BF16)
        x = jnp.dot(ff_ref[...], b, preferred_element_type=F32)
        xr, xi = x[:n1], x[n1:]
        hr, hi = h_ref[0, k], h_ref[1, k]
        y = jnp.concatenate([xr * hr - xi * hi, xr * hi + xi * hr], axis=0).astype(BF16)
        c = jnp.dot(fi_ref[...], y, preferred_element_type=F32)
        o_ref[0, k] = c[:n1]
        o_ref[1, k] = c[n1:]


def _hy_p3_kernel(c_ref, g_ref, y_ref, x_ref, bias_ref, o_ref):
    _, n2, nb, c = c_ref.shape
    conv = jnp.dot(g_ref[...], c_ref[...].reshape(2 * n2 * nb, c).astype(BF16), preferred_element_type=F32)
    o_ref[...] = x_ref[...] * (conv.reshape(o_ref.shape) + y_ref[...] * bias_ref[...])


def hyena_long_conv(y, y_col, xg, xg_col, hspec, order, bias, tables, B, L):
    g_fwd, f_fwd, f_inv, g_inv = tables
    N1 = HY_N1
    N2 = 2 * L // N1
    H2 = N2 // 2
    C = HY_WIDTH
    NB = V7X_SUBLANES
    KB = HY_KB
    params = pltpu.CompilerParams(dimension_semantics=("arbitrary", "arbitrary"),
                                  vmem_limit_bytes=V7X_VMEM_LIMIT_BYTES)
    y4 = y.reshape(B, H2, N1, y.shape[1])
    xg4 = xg.reshape(B, H2, N1, xg.shape[1])
    bsp = pl.pallas_call(
        _hy_p1_kernel,
        grid=(B, N1 // NB),
        in_specs=[pl.BlockSpec((None, H2, NB, C), lambda b, n: (b, 0, n, y_col)),
                  pl.BlockSpec((None, 2 * N2 * NB, H2 * NB), lambda b, n: (n, 0, 0))],
        out_specs=pl.BlockSpec((None, 2, N2, NB, C), lambda b, n: (b, 0, 0, n, 0)),
        out_shape=jax.ShapeDtypeStruct((B, 2, N2, N1, C), F32),
        compiler_params=params, name="hy_p1",
    )(y4, g_fwd)
    csp = pl.pallas_call(
        _hy_p2_kernel,
        grid=(N2 // KB, B),
        in_specs=[pl.BlockSpec((None, 2, KB, N1, C), lambda k, b: (b, 0, k, 0, 0)),
                  pl.BlockSpec((None, 2, KB, N1, C), lambda k, b: (order, 0, k, 0, 0)),
                  pl.BlockSpec((2 * N1, 2 * N1), lambda k, b: (0, 0)),
                  pl.BlockSpec((2 * N1, 2 * N1), lambda k, b: (0, 0))],
        out_specs=pl.BlockSpec((None, 2, KB, N1, C), lambda k, b: (b, 0, k, 0, 0)),
        out_shape=jax.ShapeDtypeStruct((B, 2, N2, N1, C), F32),
        compiler_params=params, name="hy_p2",
    )(bsp, hspec, f_fwd, f_inv)
    out = pl.pallas_call(
        _hy_p3_kernel,
        grid=(B, N1 // NB),
        in_specs=[pl.BlockSpec((None, 2, N2, NB, C), lambda b, n: (b, 0, 0, n, 0)),
                  pl.BlockSpec((None, H2 * NB, 2 * N2 * NB), lambda b, n: (n, 0, 0)),
                  pl.BlockSpec((None, H2, NB, C), lambda b, n: (b, 0, n, y_col)),
                  pl.BlockSpec((None, H2, NB, C), lambda b, n: (b, 0, n, xg_col)),
                  pl.BlockSpec((1, C), lambda b, n: (0, 0))],
        out_specs=pl.BlockSpec((None, H2, NB, C), lambda b, n: (b, 0, n, 0)),
        out_shape=jax.ShapeDtypeStruct((B, H2, N1, C), F32),
        compiler_params=params, name="hy_p3",
    )(csp, g_inv, y4, xg4, bias.reshape(1, C))
    return out.reshape(B * L, C)


def hyena_mixer(z, conv_w, conv_b, w1, b1, w2, b2, w3, freq, bias, B, L):
    zc = short_conv3(z.astype(F32), conv_w.astype(F32), conv_b.astype(F32)).reshape(B * L, -1)
    kern = hyena_filters(L, w1, b1, w2, b2, w3, freq)
    tables, tables_f32 = hyena_dft_tables(L)
    hspec = hyena_kernel_spectrum(kern, tables_f32)
    y = zc
    for o in range(HY_ORDER):
        y = hyena_long_conv(y, 0, zc, o + 1, hspec, o, bias[o].astype(F32), tables, B, L)
    return y


def rope_tables(L, rotary):
    if not rotary:
        return jnp.ones((L, GLA_DK), F32), jnp.zeros((L, GLA_DK), F32)
    t = jnp.arange(L)
    pos = jnp.stack([(t // GRID_W).astype(F32), (t % GRID_W).astype(F32)], axis=1)
    quarter = GLA_DK // 4
    inv = ROPE_BASE ** (-jnp.arange(quarter, dtype=F32) / quarter)
    ang = pos[:, :, None] * inv[None, None, :]
    cos = jnp.concatenate([jnp.cos(ang), jnp.cos(ang)], axis=-1).reshape(L, GLA_DK)
    sin = jnp.concatenate([-jnp.sin(ang), jnp.sin(ang)], axis=-1).reshape(L, GLA_DK)
    return cos, sin


def _gla_kernel(q_ref, k_ref, v_ref, g_ref, a_ref, cos_ref, sin_ref, w2_ref, b2_ref, ng_ref, s0_ref, o_ref, st_ref,
                *, n_chunks):
    C = GLA_CHUNK
    quarter = GLA_DK // 4
    lane = lax.broadcasted_iota(jnp.int32, (C, GLA_DK), 1)
    first_half = (lane % (2 * quarter)) < quarter
    row = lax.broadcasted_iota(jnp.int32, (C, C), 0)
    col = lax.broadcasted_iota(jnp.int32, (C, C), 1)
    tri = (row >= col, row <= col)
    o_ref[...] = jnp.zeros_like(o_ref)
    st_ref[...] = s0_ref[...]

    def rope(x, cos, sin):
        swapped = jnp.where(first_half, pltpu.roll(x, GLA_DK - quarter, 1), pltpu.roll(x, quarter, 1))
        return x * cos + swapped * sin

    def chunk(c, d):
        sl = pl.ds(pl.multiple_of(c * C, C), C)
        cos, sin = cos_ref[sl, :], sin_ref[sl, :]
        q = rope(q_ref[sl, :], cos, sin) * GLA_DK ** -0.5
        k = rope(k_ref[sl, :], cos, sin)
        v = v_ref[sl, :].astype(BF16)
        a = a_ref[sl, d * GLA_GATE_RANK:(d + 1) * GLA_GATE_RANK]
        pre = jnp.dot(a.astype(BF16), w2_ref[d].astype(BF16), preferred_element_type=F32) + b2_ref[d]
        log_a = jax.nn.log_sigmoid(pre) / GLA_TAU
        mask = tri[d]
        bcum = jnp.dot(mask.astype(F32), log_a, preferred_element_type=F32, precision=lax.Precision.HIGHEST)
        blast = jnp.sum(log_a, axis=0, keepdims=True)
        q_in = (q * jnp.exp(bcum)).astype(BF16)
        k_in = (k * jnp.exp(-bcum)).astype(BF16)
        k_st = (k * jnp.exp(blast - bcum)).astype(BF16)
        att = lax.dot_general(q_in, k_in, (((1,), (1,)), ((), ())), preferred_element_type=F32)
        att = jnp.where(mask, att, 0.0).astype(BF16)
        st = st_ref[d]
        o = jnp.dot(att, v, preferred_element_type=F32)
        o = o + lax.dot_general(q_in, st.astype(BF16), (((1,), (1,)), ((), ())), preferred_element_type=F32)
        o_ref[sl, :] += o
        kv_t = lax.dot_general(v, k_st, (((0,), (0,)), ((), ())), preferred_element_type=F32)
        st_ref[d] = st * jnp.exp(blast) + kv_t

    def body(i, carry):
        chunk(i, 0)
        chunk(n_chunks - 1 - i, 1)
        return carry

    lax.fori_loop(0, n_chunks, body, 0, unroll=2)
    o = o_ref[...]
    o = o * lax.rsqrt(jnp.mean(o * o, axis=-1, keepdims=True) + RMS_EPS) * ng_ref[...]
    o_ref[...] = o * jax.nn.silu(g_ref[...])


def gla_segment(z, za, s0, w2, b2, norm_g, B, L, rotary):
    H, dk, dv = GLA_HEADS, GLA_DK, GLA_DV
    cos, sin = rope_tables(L, rotary)
    qb, kb, vb, gb = IN_OFF[2] // dk, IN_OFF[3] // dk, IN_OFF[4] // dv, IN_OFF[5] // dv
    return pl.pallas_call(
        functools.partial(_gla_kernel, n_chunks=L // GLA_CHUNK),
        grid=(B, H),
        in_specs=[pl.BlockSpec((L, dk), lambda b, h: (b, qb + h)),
                  pl.BlockSpec((L, dk), lambda b, h: (b, kb + h)),
                  pl.BlockSpec((L, dv), lambda b, h: (b, vb + h)),
                  pl.BlockSpec((L, dv), lambda b, h: (b, gb + h)),
                  pl.BlockSpec((L, 2 * GLA_GATE_RANK), lambda b, h: (b, 0)),
                  pl.BlockSpec((L, dk), lambda b, h: (0, 0)),
                  pl.BlockSpec((L, dk), lambda b, h: (0, 0)),
                  pl.BlockSpec((2, GLA_GATE_RANK, dk), lambda b, h: (0, 0, h)),
                  pl.BlockSpec((2, 1, dk), lambda b, h: (0, 0, h)),
                  pl.BlockSpec((1, dv), lambda b, h: (0, 0)),
                  pl.BlockSpec((None, None, 2, dv, dk), lambda b, h: (b, h, 0, 0, 0))],
        out_specs=[pl.BlockSpec((L, dv), lambda b, h: (b, h)),
                   pl.BlockSpec((None, None, 2, dv, dk), lambda b, h: (b, h, 0, 0, 0))],
        out_shape=[jax.ShapeDtypeStruct((B * L, H * dv), F32), jax.ShapeDtypeStruct((B, H, 2, dv, dk), F32)],
        compiler_params=pltpu.CompilerParams(dimension_semantics=("arbitrary", "arbitrary"),
                                             vmem_limit_bytes=V7X_VMEM_LIMIT_BYTES),
        name="gla_segment",
    )(z, z, z, z, za, cos, sin, w2, b2.reshape(2, 1, -1), norm_g.reshape(1, dv), s0)


def gla_mixer(zl, zc, za_l, za_c, w2, b2, norm_g, B, L, Lc):
    s0 = jnp.zeros((B, GLA_HEADS, 2, GLA_DV, GLA_DK), F32)
    out_c, s_ctx = gla_segment(zc, za_c, s0, w2, b2, norm_g, B, Lc, False)
    out_l, _ = gla_segment(zl, za_l, s_ctx, w2, b2, norm_g, B, L, True)
    return out_l, out_c


def _cmul(a, b):
    return a[0] * b[0] - a[1] * b[1], a[0] * b[1] + a[1] * b[0]


S5_T = 16
S5_GQ = V7X_LANES // S5_GROUP
S5_NQ = S5_GROUPS // S5_GQ
S5_NS = 2 * S5_GQ * S5_STATE
S5_GP = V7X_LANES // S5_STATE
S5_NP = S5_GQ // S5_GP
S5_PU = S5_GP * S5_T * S5_GROUP
S5_PS = S5_GP * S5_STATE


def s5_operators(a_re, a_im, log_dt, b_re, b_im, c_re, c_im):
    T, GQ, NQ, NP, GP, P, I, G = S5_T, S5_GQ, S5_NQ, S5_NP, S5_GP, S5_STATE, S5_GROUP, S5_GROUPS
    hp = lax.Precision.HIGHEST
    TI = T * I
    lane = jnp.arange(TI)
    step_hot = (lane[None, :] // I == jnp.arange(T)[:, None]).astype(F32)
    chan_hot = (lane[None, :] % I == jnp.arange(I)[:, None]).astype(F32)
    steps = jnp.arange(T)
    same_chan = lane[:, None] % I == lane[None, :] % I
    dstep = lane[None, :] // I - lane[:, None] // I
    shift = [(same_chan[None] & (dstep[None] == steps[:, None, None])).astype(F32),
             (same_chan[None] & (dstep[None] + 2 * (lane[:, None] // I)[None] == steps[:, None, None])).astype(F32)]

    def pair_blocks(blocks, axis_rows, axis_cols):
        rows = []
        for g in range(GP):
            rows.append(jnp.concatenate([blocks[g] if h == g else jnp.zeros_like(blocks[g]) for h in range(GP)],
                                        axis=axis_cols))
        return jnp.concatenate(rows, axis=axis_rows)

    def split_groups(x):
        x = x.reshape(NQ, NP, GP, *x.shape[1:])
        return [x[:, :, g] for g in range(GP)]

    w_parts, v_parts, at_parts = [], [], []
    m_sum = 0.0
    for d in range(2):
        ar, ai = a_re[d].astype(F32), a_im[d].astype(F32)
        dt = jnp.exp(log_dt[d].astype(F32))[:, None]

        def power_rows(e):
            mag = jnp.exp((ar * dt)[:, None, :] * e[None, :, None])
            ph = (ai * dt)[:, None, :] * e[None, :, None]
            return mag * jnp.cos(ph), mag * jnp.sin(ph)

        def power_cols(e):
            pr, pi = power_rows(e)
            return (jnp.einsum('gtp,tc->gpc', pr, step_hot, precision=hp),
                    jnp.einsum('gtp,tc->gpc', pi, step_hot, precision=hp))

        def rows_power(e):
            return tuple(jnp.repeat(x, I, axis=1) for x in power_rows(e))

        a1 = power_rows(jnp.ones((1,), F32))
        zr, zi = a1[0][:, 0] - 1.0, a1[1][:, 0]
        den = ar * ar + ai * ai
        qr, qi = (zr * ar + zi * ai) / den, (zi * ar - zr * ai) / den
        bt = (b_re[d].astype(F32).transpose(0, 2, 1), b_im[d].astype(F32).transpose(0, 2, 1))
        bbar = _cmul((qr[:, None, :], qi[:, None, :]), bt)
        b_rows = tuple(jnp.tile(x, (1, T, 1)) for x in bbar)
        cm = (c_re[d].astype(F32).transpose(0, 2, 1), c_im[d].astype(F32).transpose(0, 2, 1))
        c_cols = tuple(jnp.einsum('gpi,ic->gpc', x, chan_hot, precision=hp) for x in cm)

        ex = (T - 1 - steps) if d == 0 else steps
        w = _cmul(rows_power(ex.astype(F32)), b_rows)
        wg = [split_groups(x) for x in w]
        w_parts.append(jnp.concatenate([pair_blocks(wg[0], 2, 3), pair_blocks(wg[1], 2, 3)], axis=3))
        ca = _cmul(power_cols(steps.astype(F32)), c_cols)
        kc = (jnp.einsum('gjp,gpc->gjc', bbar[0], ca[0], precision=hp)
              - jnp.einsum('gjp,gpc->gjc', bbar[1], ca[1], precision=hp))
        m_sum = m_sum + jnp.einsum('gjk,skc->gsjc', kc, shift[d], precision=hp).reshape(G, TI, TI)
        ey = (steps + 1) if d == 0 else (T - steps)
        v = _cmul(power_cols(ey.astype(F32)), c_cols)
        vg = [split_groups(x) for x in (v[0], -v[1])]
        v_parts.append(jnp.concatenate([pair_blocks(vg[0], 2, 3), pair_blocks(vg[1], 2, 3)], axis=2))
        at = power_rows(jnp.full((1,), float(T), F32))
        at_parts.append(jnp.concatenate([at[0].reshape(NQ, GQ * P), at[1].reshape(NQ, GQ * P)], axis=-1)[:, None, :])
    m_pair = pair_blocks(split_groups(m_sum), 2, 3)
    rhs = jnp.concatenate([w_parts[0], w_parts[1], m_pair], axis=-1).astype(BF16)
    return rhs, jnp.stack(v_parts).astype(BF16), jnp.stack(at_parts)


def s5_lane_permutation():
    r = jnp.arange(S5_T * V7X_LANES)
    s, g, j = r // V7X_LANES, (r % V7X_LANES) // S5_GROUP, r % S5_GROUP
    c = (g // S5_GP) * S5_PU + (g % S5_GP) * (S5_T * S5_GROUP) + s * S5_GROUP + j
    return (c[:, None] == r[None, :]).astype(BF16)


def _s5_in_kernel(u_ref, perm_ref, rhs_ref, s_ref, y_ref):
    half = S5_NS // 2
    u_nat = jnp.concatenate([u_ref[:, s, :] for s in range(S5_T)], axis=-1).astype(BF16)
    u = jnp.dot(u_nat, perm_ref[...], preferred_element_type=F32).astype(BF16)
    for n in range(S5_NP):
        r = jnp.dot(u[:, n * S5_PU:(n + 1) * S5_PU], rhs_ref[n], preferred_element_type=F32)
        for d in range(2):
            s_ref[d, :, n * S5_PS:(n + 1) * S5_PS] = r[:, 2 * d * S5_PS:(2 * d + 1) * S5_PS]
            s_ref[d, :, half + n * S5_PS:half + (n + 1) * S5_PS] = r[:, (2 * d + 1) * S5_PS:(2 * d + 2) * S5_PS]
        y_ref[:, n * S5_PU:(n + 1) * S5_PU] = r[:, 4 * S5_PS:]


def _s5_scan_kernel(s_ref, at_ref, x_ref, *, n_ctx, n_lat, batch):
    d = pl.program_id(0)
    half = S5_NS // 2
    a_r = at_ref[:, :half]
    a_i = at_ref[:, half:]

    def run(b, base, n, carry):
        def body(i, st):
            xr, xi = st
            c = i + d * (n - 1 - 2 * i)
            row = base + b * n + c
            x_ref[pl.ds(row, 1), :] = jnp.concatenate([xr, xi], axis=-1)
            s = s_ref[pl.ds(row, 1), :]
            return a_r * xr - a_i * xi + s[:, :half], a_r * xi + a_i * xr + s[:, half:]
        return lax.fori_loop(0, n, body, carry)

    for b in range(batch):
        zero = jnp.zeros((1, half), F32)
        st = run(b, 0, n_ctx, (zero, zero))
        run(b, batch * n_ctx, n_lat, st)


def _s5_out_kernel(x_ref, v_ref, yin_ref, u_ref, perm_ref, dsk_ref, y_ref):
    half = S5_NS // 2
    parts = []
    for n in range(S5_NP):
        y = yin_ref[:, n * S5_PU:(n + 1) * S5_PU]
        for d in range(2):
            x = jnp.concatenate([x_ref[d, :, n * S5_PS:(n + 1) * S5_PS],
                                 x_ref[d, :, half + n * S5_PS:half + (n + 1) * S5_PS]], axis=-1)
            y = y + jnp.dot(x.astype(BF16), v_ref[d, n], preferred_element_type=F32)
        parts.append(y)
    y = jnp.concatenate(parts, axis=-1)
    y_hi = y.astype(BF16)
    y_lo = (y - y_hi.astype(F32)).astype(BF16)
    nt = (((1,), (1,)), ((), ()))
    y = (lax.dot_general(y_hi, perm_ref[...], nt, preferred_element_type=F32)
         + lax.dot_general(y_lo, perm_ref[...], nt, preferred_element_type=F32))
    for t in range(S5_T):
        y_ref[:, t, :] = y[:, t * V7X_LANES:(t + 1) * V7X_LANES] + dsk_ref[...] * u_ref[:, t, :]


def s5_core(uc, ul, ops, d_skip, B, L, Lc):
    rhs, vmat, a_t = ops
    T, NQ, NS, NP = S5_T, S5_NQ, S5_NS, S5_NP
    n_ctx, n_lat = Lc // T, L // T
    R = B * (n_ctx + n_lat)
    RT = R // 2
    tw = T * V7X_LANES
    assert RT % V7X_SUBLANES == 0
    u = jnp.concatenate([uc.reshape(B * n_ctx, T, S5_WIDTH), ul.reshape(B * n_lat, T, S5_WIDTH)], axis=0)
    perm = s5_lane_permutation()
    params = pltpu.CompilerParams(dimension_semantics=("arbitrary", "arbitrary"),
                                  vmem_limit_bytes=V7X_VMEM_LIMIT_BYTES)
    perm_spec = pl.BlockSpec((tw, tw), lambda q, i: (0, 0), pipeline_mode=pl.Buffered(1))
    s, y_in = pl.pallas_call(
        _s5_in_kernel,
        grid=(NQ, R // RT),
        in_specs=[pl.BlockSpec((RT, T, V7X_LANES), lambda q, i: (i, 0, q)),
                  perm_spec,
                  pl.BlockSpec((None, NP, S5_PU, 4 * S5_PS + S5_PU), lambda q, i: (q, 0, 0, 0))],
        out_specs=[pl.BlockSpec((2, None, RT, NS), lambda q, i: (0, q, i, 0)),
                   pl.BlockSpec((None, RT, tw), lambda q, i: (q, i, 0))],
        out_shape=[jax.ShapeDtypeStruct((2, NQ, R, NS), F32), jax.ShapeDtypeStruct((NQ, R, tw), F32)],
        compiler_params=params,
        name="s5_in",
    )(u, perm, rhs)
    x = pl.pallas_call(
        functools.partial(_s5_scan_kernel, n_ctx=n_ctx, n_lat=n_lat, batch=B),
        grid=(2, NQ),
        in_specs=[pl.BlockSpec((None, None, R, NS), lambda d, q: (d, q, 0, 0)),
                  pl.BlockSpec((None, None, 1, NS), lambda d, q: (d, q, 0, 0))],
        out_specs=pl.BlockSpec((None, None, R, NS), lambda d, q: (d, q, 0, 0)),
        out_shape=jax.ShapeDtypeStruct((2, NQ, R, NS), F32),
        compiler_params=pltpu.CompilerParams(dimension_semantics=("arbitrary", "arbitrary")),
        name="s5_scan",
    )(s, a_t)
    y = pl.pallas_call(
        _s5_out_kernel,
        grid=(NQ, R // RT),
        in_specs=[pl.BlockSpec((2, None, RT, NS), lambda q, i: (0, q, i, 0)),
                  pl.BlockSpec((2, None, NP, 2 * S5_PS, S5_PU), lambda q, i: (0, q, 0, 0, 0)),
                  pl.BlockSpec((None, RT, tw), lambda q, i: (q, i, 0)),
                  pl.BlockSpec((RT, T, V7X_LANES), lambda q, i: (i, 0, q)),
                  perm_spec,
                  pl.BlockSpec((1, V7X_LANES), lambda q, i: (0, q))],
        out_specs=pl.BlockSpec((RT, T, V7X_LANES), lambda q, i: (i, 0, q)),
        out_shape=jax.ShapeDtypeStruct((R, T, S5_WIDTH), F32),
        compiler_params=params,
        name="s5_out",
    )(x, vmat, y_in, u, perm, d_skip.astype(F32).reshape(1, S5_WIDTH))
    y = y.reshape(R * T, S5_WIDTH)
    return y[:B * Lc], y[B * Lc:]


def s5_glu(y, w, b, l):
    y = jax.nn.gelu(y)
    return (y * jax.nn.sigmoid(mm(y, w, (l,)) + b[l].astype(F32))).astype(BF16)


def _gate_merge_kernel(h_ref, b0_ref, b1_ref, b2_ref, b3_ref, wg0, wg1, wg2, wg3, wb0, wb1, wb2, wb3, bias_ref,
                       o_ref, wgb_ref, wbb_ref):
    wg_refs = (wg0, wg1, wg2, wg3)
    wb_refs = (wb0, wb1, wb2, wb3)
    br_refs = (b0_ref, b1_ref, b2_ref, b3_ref)

    @pl.when(pl.program_id(1) == 0)
    def _():
        for k in range(N_BRANCH):
            wgb_ref[k] = wg_refs[k][...].astype(BF16)
            wbb_ref[k] = wb_refs[k][...].astype(BF16)

    h = h_ref[...]
    acc = None
    for k in range(N_BRANCH):
        gate = jax.nn.sigmoid(jnp.dot(h, wgb_ref[k], preferred_element_type=F32) + bias_ref[k])
        term = gate * jnp.dot(br_refs[k][...], wbb_ref[k], preferred_element_type=F32)
        acc = term if acc is None else acc + term
    o_ref[...] = acc.astype(o_ref.dtype)


def gate_merge(h, branches, w_gate, b_gate, w_branch, l):
    M, D = h.shape
    W = branches[0].shape[1]
    tn = 256
    tm = min(M, 512)

    def wspec(k, rows):
        return pl.BlockSpec((None, None, rows, tn), lambda j, i: (l, k, 0, j), pipeline_mode=pl.Buffered(1))

    return pl.pallas_call(
        _gate_merge_kernel,
        grid=(D // tn, M // tm),
        in_specs=[pl.BlockSpec((tm, D), lambda j, i: (i, 0))]
                 + [pl.BlockSpec((tm, W), lambda j, i: (i, 0)) for _ in range(N_BRANCH)]
                 + [wspec(k, D) for k in range(N_BRANCH)]
                 + [wspec(k, W) for k in range(N_BRANCH)]
                 + [pl.BlockSpec((None, N_BRANCH, 1, tn), lambda j, i: (l, 0, 0, j))],
        out_specs=pl.BlockSpec((tm, tn), lambda j, i: (i, j)),
        out_shape=jax.ShapeDtypeStruct((M, D), BF16),
        scratch_shapes=[pltpu.VMEM((N_BRANCH, D, tn), BF16), pltpu.VMEM((N_BRANCH, W, tn), BF16)],
        compiler_params=pltpu.CompilerParams(dimension_semantics=("arbitrary", "arbitrary"),
                                             vmem_limit_bytes=V7X_VMEM_LIMIT_BYTES),
        name="gate_merge",
    )(h, *branches, *([w_gate] * N_BRANCH), *([w_branch] * N_BRANCH),
      b_gate.reshape(b_gate.shape[0], N_BRANCH, 1, D))


def _moe_gather_kernel(idx_ref, h_ref, o_ref, stage_ref, sem, *, n_tokens, n_experts):
    g = pl.program_id(0)
    cap = o_ref.shape[0]
    b = g // n_experts

    def row_copy(c):
        row = b * n_tokens + idx_ref[g * cap + c]
        return pltpu.make_async_copy(h_ref.at[pl.ds(row, 1), :], stage_ref.at[pl.ds(c, 1), :], sem)

    def start(c, carry):
        row_copy(c).start()
        return carry

    def wait(c, carry):
        row_copy(c).wait()
        return carry

    lax.fori_loop(0, cap, start, 0, unroll=8)
    lax.fori_loop(0, cap, wait, 0, unroll=8)
    o_ref[...] = stage_ref[...].astype(BF16)


def moe_gather(h, idx):
    B, N, D = h.shape
    _, E, cap = idx.shape
    return pl.pallas_call(
        functools.partial(_moe_gather_kernel, n_tokens=N, n_experts=E),
        grid_spec=pltpu.PrefetchScalarGridSpec(
            num_scalar_prefetch=1,
            grid=(B * E,),
            in_specs=[pl.BlockSpec(memory_space=pl.ANY)],
            out_specs=pl.BlockSpec((None, cap, D), lambda g, idx_ref: (g, 0, 0)),
            scratch_shapes=[pltpu.VMEM((cap, D), F32), pltpu.SemaphoreType.DMA(())],
        ),
        out_shape=jax.ShapeDtypeStruct((B * E, cap, D), BF16),
        compiler_params=pltpu.CompilerParams(dimension_semantics=("arbitrary",),
                                             vmem_limit_bytes=V7X_VMEM_LIMIT_BYTES),
        name="moe_gather",
    )(idx.reshape(-1).astype(jnp.int32), h.reshape(B * N, D)).reshape(B, E, cap, D)


def _expert_gu_kernel(x_ref, wg_ref, wu_ref, o_ref):
    nb, cap, d = x_ref.shape
    x = x_ref[...].reshape(nb * cap, d)
    g = jnp.dot(x, wg_ref[...].astype(BF16), preferred_element_type=F32)
    u = jnp.dot(x, wu_ref[...].astype(BF16), preferred_element_type=F32)
    o_ref[...] = (jax.nn.silu(g) * u).astype(BF16).reshape(o_ref.shape)


def expert_gu(xs, w_gu, l):
    B, E, C, D = xs.shape
    F = w_gu.shape[3] // 2
    tn = 256
    nj = F // tn
    return pl.pallas_call(
        _expert_gu_kernel,
        grid=(E, nj),
        in_specs=[pl.BlockSpec((B, None, C, D), lambda e, j: (0, e, 0, 0)),
                  pl.BlockSpec((None, None, D, tn), lambda e, j: (l, e, 0, j)),
                  pl.BlockSpec((None, None, D, tn), lambda e, j: (l, e, 0, nj + j))],
        out_specs=pl.BlockSpec((B, None, C, tn), lambda e, j: (0, e, 0, j)),
        out_shape=jax.ShapeDtypeStruct((B, E, C, F), BF16),
        compiler_params=pltpu.CompilerParams(dimension_semantics=("arbitrary",) * 2,
                                             vmem_limit_bytes=V7X_VMEM_LIMIT_BYTES),
        name="expert_gu",
    )(xs, w_gu, w_gu)


def _expert_down_kernel(idx_ref, a_ref, w_ref, gate_ref, o_ref, y_ref, *, n_experts):
    b, e = pl.program_id(0), pl.program_id(2)
    cap = a_ref.shape[0]

    @pl.when(e == 0)
    def _():
        o_ref[...] = jnp.zeros_like(o_ref)

    y = jnp.dot(a_ref[...], w_ref[...].astype(BF16), preferred_element_type=F32)
    y_ref[...] = y * gate_ref[...]
    base = (b * n_experts + e) * cap

    def add_row(r, carry):
        tok = idx_ref[base + r]
        o_ref[pl.ds(tok, 1), :] += y_ref[pl.ds(r, 1), :]
        return carry

    lax.fori_loop(0, cap, add_row, 0, unroll=8)


def expert_down(act, w_down, gate, idx, n_tokens, l):
    B, E, C, F = act.shape
    D = w_down.shape[3]
    tn = min(1024, D)
    assert D % tn == 0
    return pl.pallas_call(
        functools.partial(_expert_down_kernel, n_experts=E),
        grid_spec=pltpu.PrefetchScalarGridSpec(
            num_scalar_prefetch=1,
            grid=(B, D // tn, E),
            in_specs=[pl.BlockSpec((None, None, C, F), lambda b, j, e, idx_ref: (b, e, 0, 0)),
                      pl.BlockSpec((None, None, F, tn), lambda b, j, e, idx_ref: (l, e, 0, j)),
                      pl.BlockSpec((None, None, C, 1), lambda b, j, e, idx_ref: (b, e, 0, 0))],
            out_specs=pl.BlockSpec((None, n_tokens, tn), lambda b, j, e, idx_ref: (b, 0, j)),
            scratch_shapes=[pltpu.VMEM((C, tn), F32)],
        ),
        out_shape=jax.ShapeDtypeStruct((B, n_tokens, D), F32),
        compiler_params=pltpu.CompilerParams(dimension_semantics=("arbitrary",) * 3,
                                             vmem_limit_bytes=V7X_VMEM_LIMIT_BYTES),
        name="expert_down",
    )(idx.reshape(-1).astype(jnp.int32), act, w_down, gate[..., None])


def moe_ec(h, router_w, w_gu, w_down, l):
    B, N, D = h.shape
    cap = max(1, EC_CAPACITY * N // N_EXPERTS)
    logits = jnp.einsum('bnd,de->bne', h, router_w[l], precision=lax.Precision.HIGHEST)
    aff = jax.nn.softmax(logits.astype(F32), axis=-1)
    gate, idx = lax.top_k(jnp.swapaxes(aff, 1, 2), cap)
    return expert_down(expert_gu(moe_gather(h, idx), w_gu, l), w_down, gate, idx, N, l)


def kernel(x, c, ctx, c_ctx, ada_w, ada_b, mix_pre_g, mix_post_g, ffn_pre_g, ffn_post_g, w_in, na_rpb,
           hy_conv_w, hy_conv_b, hy_w1, hy_b1, hy_w2, hy_b2, hy_w3, hy_freq, hy_bias,
           gla_w2, gla_b2, gla_norm_g, s5_a_re, s5_a_im, s5_log_dt, s5_b_re, s5_b_im, s5_c_re, s5_c_im,
           s5_d, s5_glu_w, s5_glu_b, w_branch, w_gate, b_gate, w_out, router_w, ex_w_gu, ex_w_down):
    xl, xc = x, ctx
    B, L, Lc = x.shape[0], x.shape[1], ctx.shape[1]
    for l in range(DEPTH):
        need_ctx = l < DEPTH - 1
        cvecs = jnp.concatenate([c, c_ctx[None, :]], axis=0)
        mod = mm(jax.nn.silu(cvecs), ada_w, (l,)) + ada_b[l]
        ml = [p[:, None, :] for p in jnp.split(mod[:B], 6, axis=-1)]
        mc = jnp.split(mod[B], 6, axis=-1)

        hl = rms_norm(xl, mix_pre_g[l]) * (1.0 + ml[1]) + ml[0]
        hc = rms_norm(xc, mix_pre_g[l]) * (1.0 + mc[1]) + mc[0]
        hl2, hc2 = hl.reshape(B * L, D_MODEL).astype(BF16), hc.reshape(B * Lc, D_MODEL).astype(BF16)
        w_a, w_s5 = w_in[l, :, IN_MAIN:IN_MAIN + 2 * GLA_GATE_RANK], w_in[l, :, IN_MAIN + 2 * GLA_GATE_RANK:]
        zl, zc = mm(hl2, w_in, (l,), IN_MAIN), mm(hc2, w_in, (l,), IN_MAIN)
        zl3, zc3 = zl.reshape(B, L, IN_MAIN), zc.reshape(B, Lc, IN_MAIN)

        na_l = na_latent(zl, zc, na_bias_table(na_rpb[l], L // GRID_W), B, L, Lc)
        hy_args = (hy_conv_w[l], hy_conv_b[l], hy_w1[l], hy_b1[l], hy_w2[l], hy_b2[l], hy_w3[l], hy_freq[l], hy_bias[l])
        hy_l = hyena_mixer(zl3[..., IN_OFF[1]:IN_OFF[2]], *hy_args, B, L).astype(BF16)
        gla_l, gla_c = gla_mixer(zl, zc, mm(hl2, w_a), mm(hc2, w_a), gla_w2[l], gla_b2[l], gla_norm_g[l], B, L, Lc)
        s5_ops = s5_operators(s5_a_re[l], s5_a_im[l], s5_log_dt[l], s5_b_re[l], s5_b_im[l], s5_c_re[l], s5_c_im[l])
        s5_yc, s5_yl = s5_core(mm(hc2, w_s5), mm(hl2, w_s5), s5_ops, s5_d[l], B, L, Lc)
        s5_l = s5_glu(s5_yl, s5_glu_w, s5_glu_b, l)
        merged = gate_merge(hl2, (na_l, hy_l, gla_l.astype(BF16), s5_l), w_gate, b_gate, w_branch, l)
        yl = mm(merged, w_out, (l,)).reshape(B, L, D_MODEL)
        xl = xl + ml[2] * rms_norm(yl, mix_post_g[l])
        if need_ctx:
            na_c = na_context(zc, B, Lc).astype(BF16)
            hy_c = hyena_mixer(zc3[..., IN_OFF[1]:IN_OFF[2]], *hy_args, B, Lc).astype(BF16)
            s5_c = s5_glu(s5_yc, s5_glu_w, s5_glu_b, l)
            merged = gate_merge(hc2, (na_c, hy_c, gla_c.astype(BF16), s5_c), w_gate, b_gate, w_branch, l)
            yc = mm(merged, w_out, (l,)).reshape(B, Lc, D_MODEL)
            xc = xc + mc[2] * rms_norm(yc, mix_post_g[l])

        hl = rms_norm(xl, ffn_pre_g[l]) * (1.0 + ml[4]) + ml[3]
        xl = xl + ml[5] * rms_norm(moe_ec(hl, router_w, ex_w_gu, ex_w_down, l), ffn_post_g[l])
        if need_ctx:
            hc = rms_norm(xc, ffn_pre_g[l]) * (1.0 + mc[4]) + mc[3]
            xc = xc + mc[5] * rms_norm(moe_ec(hc, router_w, ex_w_gu, ex_w_down, l), ffn_post_g[l])
    return xl
```

```python
import functools
import math

import jax
import jax.numpy as jnp
import numpy as np
from jax import lax
from jax.experimental import pallas as pl
from jax.experimental.pallas import tpu as pltpu

D_MODEL = 4096
BATCH = 2
SEQ = 4096
DEPTH = 2
CTX_LEN = 256
GRID_W = 64

NA_HEADS = 8
NA_HEAD_DIM = 128
NA_WIDTH = NA_HEADS * NA_HEAD_DIM
NA_WIN_R = 8
NA_WIN_C = 16

HY_WIDTH = 1024
HY_ORDER = 2
HY_BANDS = 16
HY_POS_DIM = 1 + 2 * HY_BANDS
HY_FILTER_HIDDEN = 64
HY_DECAY_TARGET = 1e-2
HY_FAST_DECAY = 0.3
HY_SLOW_DECAY = 1.5

GLA_HEADS = 4
GLA_DK = 128
GLA_DV = 256
GLA_QK_WIDTH = GLA_HEADS * GLA_DK
GLA_V_WIDTH = GLA_HEADS * GLA_DV
GLA_GATE_RANK = 16
GLA_TAU = 16.0
GLA_CHUNK = 64

S5_WIDTH = 1024
S5_GROUP = 16
S5_GROUPS = S5_WIDTH // S5_GROUP
S5_STATE = 64

N_BRANCH = 4
BRANCH_WIDTH = 1024
N_EXPERTS = 16
EXPERT_FF = 1024
EC_CAPACITY = 2

ROPE_BASE = 10000.0
RMS_EPS = 1e-6
NEG_INF = -1e30
F32 = jnp.float32
BF16 = jnp.bfloat16

IN_SPLITS = (3 * NA_WIDTH, (HY_ORDER + 1) * HY_WIDTH, GLA_QK_WIDTH, GLA_QK_WIDTH, GLA_V_WIDTH, GLA_V_WIDTH,
             2 * GLA_GATE_RANK, S5_WIDTH)
IN_COLS = sum(IN_SPLITS)
IN_OFF = tuple(int(v) for v in np.concatenate([[0], np.cumsum(IN_SPLITS)]))
IN_MAIN = IN_OFF[6]

V7X_LANES = 128
V7X_SUBLANES = 8
V7X_VMEM_LIMIT_BYTES = 56 * 1024 * 1024


def _mm_kernel(x_ref, w_ref, o_ref, wb_ref):
    @pl.when(pl.program_id(1) == 0)
    def _():
        wb_ref[...] = w_ref[...].astype(BF16)

    o_ref[...] = jnp.dot(x_ref[...].astype(BF16), wb_ref[...], preferred_element_type=F32).astype(o_ref.dtype)


def _mm_tiles(M, K, N):
    tm = M if M <= 1024 else 1024
    tn = 512 if K > 1024 else 1024
    tn = min(tn, N)
    return tm, tn


def mm(x, w, lead=(), ncols=None, out_dtype=F32):
    M, K = x.shape
    nw = w.shape[-1]
    N = nw if ncols is None else ncols
    pad = (-M) % V7X_SUBLANES
    if pad:
        x = jnp.pad(x, ((0, pad), (0, 0)))
    Mp = M + pad
    tm, tn = _mm_tiles(Mp, K, N)
    assert Mp % tm == 0 and (N == nw or N % tn == 0) and w.shape[-2] == K and w.ndim == len(lead) + 2
    out = pl.pallas_call(
        _mm_kernel,
        grid=(pl.cdiv(N, tn), Mp // tm),
        in_specs=[pl.BlockSpec((tm, K), lambda j, i: (i, 0)),
                  pl.BlockSpec((None,) * len(lead) + (K, tn), lambda j, i: tuple(lead) + (0, j))],
        out_specs=pl.BlockSpec((tm, tn), lambda j, i: (i, j)),
        out_shape=jax.ShapeDtypeStruct((Mp, N), out_dtype),
        scratch_shapes=[pltpu.VMEM((K, tn), BF16)],
        compiler_params=pltpu.CompilerParams(dimension_semantics=("arbitrary", "arbitrary"),
                                             vmem_limit_bytes=V7X_VMEM_LIMIT_BYTES),
        name="mm",
    )(x, w)
    return out[:M] if pad else out


def rms_norm(x, g):
    xf = x.astype(F32)
    y = xf * lax.rsqrt(jnp.mean(xf * xf, axis=-1, keepdims=True) + RMS_EPS)
    return (y * g.astype(F32)).astype(x.dtype)


def dense_attention(q, k, v):
    scale = q.shape[-1] ** -0.5
    s = jnp.einsum('bqhd,bkhd->bhqk', q, k, preferred_element_type=F32) * scale
    p = jax.nn.softmax(s, axis=-1).astype(v.dtype)
    o = jnp.einsum('bhqk,bkhd->bqhd', p, v)
    return o.reshape(q.shape[0], q.shape[1], -1)


NA_TILE_ROWS = 8


def na_bias_table(rpb, rows):
    H = rpb.shape[0]
    W, TR, NA_ROFF, NA_COFF = GRID_W, NA_TILE_ROWS, 2 * NA_WIN_R - 1, 2 * NA_WIN_C - 1
    nt = rows // TR
    cases = jnp.array([0, min(1, nt - 1), nt - 1])[:, None, None]
    ks = jnp.clip(cases * TR - NA_WIN_R // 2, 0, rows - 2 * TR)

    def row_terms(r, k):
        r_abs, k_abs = cases * TR + r, ks + k
        ws = jnp.clip(r_abs - NA_WIN_R // 2, 0, rows - NA_WIN_R)
        return jnp.clip(k_abs - r_abs + (NA_WIN_R - 1), 0, NA_ROFF - 1), (k_abs >= ws) & (k_abs < ws + NA_WIN_R)

    c = jnp.arange(W)
    col_off = jnp.clip(c[None, :] - c[:, None] + (NA_WIN_C - 1), 0, NA_COFF - 1)
    col_hot = (col_off[..., None] == jnp.arange(NA_COFF)).astype(F32)
    tcol = jnp.einsum('hab,qpb->hqap', rpb.astype(F32), col_hot, precision=lax.Precision.HIGHEST)
    tcat = tcol.reshape(H, W, NA_ROFF * W).astype(BF16)
    x = jnp.arange(NA_ROFF * W)[None, :, None]
    y = jnp.arange(2 * TR * W)[None, None, :]
    sel = []
    for r in range(TR):
        a_of, _ = row_terms(r, y // W)
        sel.append(((x // W == a_of) & (x % W == y % W)).astype(BF16))
    sel = jnp.stack(sel, axis=1)
    b = jnp.einsum('hqx,crxy->chrqy', tcat, sel, preferred_element_type=F32)
    b = b.reshape(3, H, TR * W, 2 * TR * W)
    i = jnp.arange(TR * W)[None, :, None]
    _, row_ok = row_terms(i // W, y // W)
    col_start = jnp.clip(i % W - NA_WIN_C // 2, 0, W - NA_WIN_C)
    ok = row_ok & (y % W >= col_start) & (y % W < col_start + NA_WIN_C)
    return jnp.where(ok[:, None], b, NEG_INF).astype(BF16)


def _na_kernel(q_ref, k_ref, v_ref, kc_ref, vc_ref, b_ref, o_ref, *, scale):
    q = q_ref[...].astype(BF16)
    nk = k_ref.shape[0]
    k = jnp.concatenate([k_ref[...].astype(BF16), kc_ref[...].astype(BF16)], axis=0)
    v = jnp.concatenate([v_ref[...].astype(BF16), vc_ref[...].astype(BF16)], axis=0)
    s = lax.dot_general(q, k, (((1,), (1,)), ((), ())), preferred_element_type=F32) * scale
    bias = jnp.concatenate([b_ref[...].astype(F32), jnp.zeros((s.shape[0], s.shape[1] - nk), F32)], axis=1)
    s = s + bias
    m = jnp.max(s, axis=-1, keepdims=True)
    p = jnp.exp(s - m)
    den = jnp.sum(p, axis=-1, keepdims=True)
    o = jnp.dot(p.astype(BF16), v, preferred_element_type=F32)
    o_ref[...] = (o / den).astype(o_ref.dtype)


def na_latent(zl, zc, bias_tab, B, L, Lc):
    rows = L // GRID_W
    nt = rows // NA_TILE_ROWS
    tq = NA_TILE_ROWS * GRID_W
    tk = 2 * tq
    H, dh = NA_HEADS, NA_HEAD_DIM

    def case(j):
        return jnp.where(j == 0, 0, jnp.where(j == nt - 1, 2, 1))

    def kstart(b, j):
        ks = jnp.clip(j * NA_TILE_ROWS - NA_WIN_R // 2, 0, rows - 2 * NA_TILE_ROWS)
        return pl.multiple_of(b * L + ks * GRID_W, (NA_WIN_R // 2) * GRID_W)

    def kv_spec(col0):
        return pl.BlockSpec((pl.Element(tk), pl.Element(dh)),
                            lambda b, j, h: (kstart(b, j), pl.multiple_of((col0 + h) * dh, dh)))

    return pl.pallas_call(
        functools.partial(_na_kernel, scale=dh ** -0.5),
        grid=(B, nt, H),
        in_specs=[
            pl.BlockSpec((tq, dh), lambda b, j, h: (b * nt + j, h)),
            kv_spec(H),
            kv_spec(2 * H),
            pl.BlockSpec((Lc, dh), lambda b, j, h: (b, H + h)),
            pl.BlockSpec((Lc, dh), lambda b, j, h: (b, 2 * H + h)),
            pl.BlockSpec((None, None, tq, tk), lambda b, j, h: (case(j), h, 0, 0)),
        ],
        out_specs=pl.BlockSpec((tq, dh), lambda b, j, h: (b * nt + j, h)),
        out_shape=jax.ShapeDtypeStruct((B * L, H * dh), BF16),
        compiler_params=pltpu.CompilerParams(dimension_semantics=("arbitrary",) * 3),
        name="na_latent",
    )(zl, zl, zl, zc, zc, bias_tab)


def na_context(zc, B, Lc):
    qc, kc, vc = [t.reshape(B, Lc, NA_HEADS, NA_HEAD_DIM) for t in jnp.split(zc[:, :3 * NA_WIDTH], 3, axis=-1)]
    return dense_attention(qc, kc, vc).reshape(B * Lc, NA_WIDTH)


def short_conv3(u, w, b):
    up = jnp.pad(u, ((0, 0), (1, 1), (0, 0)))
    return up[:, :-2] * w[0] + up[:, 1:-1] * w[1] + up[:, 2:] * w[2] + b


def hyena_filters(L, w1, b1, w2, b2, w3, freq):
    r = jnp.arange(2 * L)
    lag = jnp.where(r < L, r, 2 * L - r)
    t = jnp.minimum(lag, L - 1).astype(F32)
    t01 = t / max(L - 1, 1)
    bands = jnp.linspace(1e-4, HY_BANDS - 1, HY_BANDS, dtype=F32)
    ang = (2.0 * math.pi / L) * t[:, None] * bands[None, :]
    z = jnp.concatenate([t01[:, None], jnp.cos(ang), -jnp.sin(ang)], axis=-1)
    f = freq.astype(F32)
    h = jnp.sin(f * (z @ w1.astype(F32) + b1.astype(F32)))
    h = jnp.sin(f * (h @ w2.astype(F32) + b2.astype(F32)))
    w3d = w3.astype(F32).reshape(-1, 2, HY_ORDER * HY_WIDTH)
    h = jnp.where((r < L)[:, None], h @ w3d[:, 0], h @ w3d[:, 1]).reshape(2 * L, HY_ORDER, HY_WIDTH)
    deltas = jnp.abs(jnp.linspace(math.log(HY_DECAY_TARGET) / HY_SLOW_DECAY, math.log(HY_DECAY_TARGET) / HY_FAST_DECAY,
                                  HY_WIDTH, dtype=F32))
    decay = jnp.where((r != L)[:, None], jnp.exp(-t01[:, None] * deltas[None, :]), 0.0)
    kern = h * decay[:, None, :]
    return kern / jnp.sum(jnp.abs(kern), axis=0, keepdims=True)


HY_N1 = 64
HY_KB = 4


def _cis(num, den):
    ang = (2.0 * math.pi / den) * (num % den).astype(F32)
    return jnp.cos(ang), jnp.sin(ang)


def hyena_dft_tables(L):
    N, N1 = 2 * L, HY_N1
    N2 = N // N1
    NB = V7X_SUBLANES
    n1 = jnp.arange(N1)[:, None, None]
    k2 = jnp.arange(N2)[None, :, None]
    n2 = jnp.arange(N2)[None, None, :]
    c, s = _cis(n1 * k2 + n2 * k2 * N1, N)
    g_full = jnp.concatenate([c, -s], axis=1)
    row = jnp.arange(2 * N2 * NB)[None, :, None]
    col = jnp.arange(N2 // 2 * NB)[None, None, :]
    blk = jnp.arange(N1 // NB)[:, None, None]
    r_im, r_k2, r_i = row // (N2 * NB), (row // NB) % N2, row % NB
    c_n2, c_i = col // NB, col % NB
    bc, bs = _cis((blk * NB + r_i) * r_k2 + c_n2 * r_k2 * N1, N)
    g_fwd = jnp.where(r_i == c_i, jnp.where(r_im == 0, bc, -bs), 0.0)
    g_inv = g_fwd.transpose(0, 2, 1) / N
    a = jnp.arange(N1)
    fc, fs = _cis(a[:, None] * a[None, :], N1)
    f_fwd = jnp.concatenate([jnp.concatenate([fc, fs], axis=1), jnp.concatenate([-fs, fc], axis=1)], axis=0)
    f_inv = jnp.concatenate([jnp.concatenate([fc, -fs], axis=1), jnp.concatenate([fs, fc], axis=1)], axis=0)
    return (g_fwd.astype(BF16), f_fwd.astype(BF16), f_inv.astype(BF16), g_inv.astype(BF16)), (g_full, f_fwd)


def _split_bf16(x):
    hi = x.astype(BF16)
    return hi, (x - hi.astype(F32)).astype(BF16)


def _dot_hi_lo(a, b):
    a_hi, a_lo = _split_bf16(a)
    b_hi, b_lo = _split_bf16(b)
    return (jnp.dot(a_hi, b_hi, preferred_element_type=F32) + jnp.dot(a_hi, b_lo, preferred_element_type=F32)
            + jnp.dot(a_lo, b_hi, preferred_element_type=F32))


def _hy_s1_kernel(x_ref, g_ref, o_ref):
    n2 = g_ref.shape[1] // 2
    for i in range(x_ref.shape[1]):
        r = _dot_hi_lo(g_ref[i], x_ref[:, i, :])
        o_ref[0, :, i, :] = r[:n2]
        o_ref[1, :, i, :] = r[n2:]


def _hy_s2_kernel(b_ref, f_ref, o_ref):
    _, kb, n1, c_dim = b_ref.shape
    for k in range(kb):
        x = _dot_hi_lo(f_ref[...], b_ref[:, k].reshape(2 * n1, c_dim))
        o_ref[0, k] = x[:n1]
        o_ref[1, k] = x[n1:]


def hyena_kernel_spectrum(kern, tables_f32):
    g_full, f_fwd = tables_f32
    N, O, C = kern.shape
    N1 = HY_N1
    N2 = N // N1
    NB = V7X_SUBLANES
    KB = HY_KB
    params = pltpu.CompilerParams(dimension_semantics=("arbitrary", "arbitrary"),
                                  vmem_limit_bytes=V7X_VMEM_LIMIT_BYTES)
    bsp = pl.pallas_call(
        _hy_s1_kernel,
        grid=(O, N1 // NB),
        in_specs=[pl.BlockSpec((N2, NB, C), lambda o, n: (0, n, o)),
                  pl.BlockSpec((NB, 2 * N2, N2), lambda o, n: (n, 0, 0))],
        out_specs=pl.BlockSpec((None, 2, N2, NB, C), lambda o, n: (o, 0, 0, n, 0)),
        out_shape=jax.ShapeDtypeStruct((O, 2, N2, N1, C), F32),
        compiler_params=params, name="hy_s1",
    )(kern.reshape(N2, N1, O * C), g_full)
    return pl.pallas_call(
        _hy_s2_kernel,
        grid=(O, N2 // KB),
        in_specs=[pl.BlockSpec((None, 2, KB, N1, C), lambda o, k: (o, 0, k, 0, 0)),
                  pl.BlockSpec((2 * N1, 2 * N1), lambda o, k: (0, 0))],
        out_specs=pl.BlockSpec((None, 2, KB, N1, C), lambda o, k: (o, 0, k, 0, 0)),
        out_shape=jax.ShapeDtypeStruct((O, 2, N2, N1, C), F32),
        compiler_params=params, name="hy_s2",
    )(bsp, f_fwd)


def _hy_p1_kernel(x_ref, g_ref, o_ref):
    h2, nb, c = x_ref.shape
    r = jnp.dot(g_ref[...], x_ref[...].reshape(h2 * nb, c).astype(BF16), preferred_element_type=F32)
    o_ref[...] = r.reshape(o_ref.shape)


def _hy_p2_kernel(b_ref, h_ref, ff_ref, fi_ref, o_ref):
    _, kb, n1, c_dim = b_ref.shape
    for k in range(kb):
        b = b_ref[:, k].reshape(2 * n1, c_dim).astype(BF16)
        x = jnp.dot(ff_ref[...], b, preferred_element_type=F32)
        xr, xi = x[:n1], x[n1:]
        hr, hi = h_ref[0, k], h_ref[1, k]
        y = jnp.concatenate([xr * hr - xi * hi, xr * hi + xi * hr], axis=0).astype(BF16)
        c = jnp.dot(fi_ref[...], y, preferred_element_type=F32)
        o_ref[0, k] = c[:n1]
        o_ref[1, k] = c[n1:]


def _hy_p3_kernel(c_ref, g_ref, y_ref, x_ref, bias_ref, o_ref):
    _, n2, nb, c = c_ref.shape
    conv = jnp.dot(g_ref[...], c_ref[...].reshape(2 * n2 * nb, c).astype(BF16), preferred_element_type=F32)
    o_ref[...] = x_ref[...] * (conv.reshape(o_ref.shape) + y_ref[...] * bias_ref[...])


def hyena_long_conv(y, y_col, xg, xg_col, hspec, order, bias, tables, B, L):
    g_fwd, f_fwd, f_inv, g_inv = tables
    N1 = HY_N1
    N2 = 2 * L // N1
    H2 = N2 // 2
    C = HY_WIDTH
    NB = V7X_SUBLANES
    KB = HY_KB
    params = pltpu.CompilerParams(dimension_semantics=("arbitrary", "arbitrary"),
                                  vmem_limit_bytes=V7X_VMEM_LIMIT_BYTES)
    y4 = y.reshape(B, H2, N1, y.shape[1])
    xg4 = xg.reshape(B, H2, N1, xg.shape[1])
    bsp = pl.pallas_call(
        _hy_p1_kernel,
        grid=(B, N1 // NB),
        in_specs=[pl.BlockSpec((None, H2, NB, C), lambda b, n: (b, 0, n, y_col)),
                  pl.BlockSpec((None, 2 * N2 * NB, H2 * NB), lambda b, n: (n, 0, 0))],
        out_specs=pl.BlockSpec((None, 2, N2, NB, C), lambda b, n: (b, 0, 0, n, 0)),
        out_shape=jax.ShapeDtypeStruct((B, 2, N2, N1, C), F32),
        compiler_params=params, name="hy_p1",
    )(y4, g_fwd)
    csp = pl.pallas_call(
        _hy_p2_kernel,
        grid=(N2 // KB, B),
        in_specs=[pl.BlockSpec((None, 2, KB, N1, C), lambda k, b: (b, 0, k, 0, 0)),
                  pl.BlockSpec((None, 2, KB, N1, C), lambda k, b: (order, 0, k, 0, 0)),
                  pl.BlockSpec((2 * N1, 2 * N1), lambda k, b: (0, 0)),
                  pl.BlockSpec((2 * N1, 2 * N1), lambda k, b: (0, 0))],
        out_specs=pl.BlockSpec((None, 2, KB, N1, C), lambda k, b: (b, 0, k, 0, 0)),
        out_shape=jax.ShapeDtypeStruct((B, 2, N2, N1, C), F32),
        compiler_params=params, name="hy_p2",
    )(bsp, hspec, f_fwd, f_inv)
    out = pl.pallas_call(
        _hy_p3_kernel,
        grid=(B, N1 // NB),
        in_specs=[pl.BlockSpec((None, 2, N2, NB, C), lambda b, n: (b, 0, 0, n, 0)),
                  pl.BlockSpec((None, H2 * NB, 2 * N2 * NB), lambda b, n: (n, 0, 0)),
                  pl.BlockSpec((None, H2, NB, C), lambda b, n: (b, 0, n, y_col)),
                  pl.BlockSpec((None, H2, NB, C), lambda b, n: (b, 0, n, xg_col)),
                  pl.BlockSpec((1, C), lambda b, n: (0, 0))],
        out_specs=pl.BlockSpec((None, H2, NB, C), lambda b, n: (b, 0, n, 0)),
        out_shape=jax.ShapeDtypeStruct((B, H2, N1, C), F32),
        compiler_params=params, name="hy_p3",
    )(csp, g_inv, y4, xg4, bias.reshape(1, C))
    return out.reshape(B * L, C)


def hyena_mixer(z, conv_w, conv_b, w1, b1, w2, b2, w3, freq, bias, B, L):
    zc = short_conv3(z.astype(F32), conv_w.astype(F32), conv_b.astype(F32)).reshape(B * L, -1)
    kern = hyena_filters(L, w1, b1, w2, b2, w3, freq)
    tables, tables_f32 = hyena_dft_tables(L)
    hspec = hyena_kernel_spectrum(kern, tables_f32)
    y = zc
    for o in range(HY_ORDER):
        y = hyena_long_conv(y, 0, zc, o + 1, hspec, o, bias[o].astype(F32), tables, B, L)
    return y


def rope_tables(L, rotary):
    if not rotary:
        return jnp.ones((L, GLA_DK), F32), jnp.zeros((L, GLA_DK), F32)
    t = jnp.arange(L)
    pos = jnp.stack([(t // GRID_W).astype(F32), (t % GRID_W).astype(F32)], axis=1)
    quarter = GLA_DK // 4
    inv = ROPE_BASE ** (-jnp.arange(quarter, dtype=F32) / quarter)
    ang = pos[:, :, None] * inv[None, None, :]
    cos = jnp.concatenate([jnp.cos(ang), jnp.cos(ang)], axis=-1).reshape(L, GLA_DK)
    sin = jnp.concatenate([-jnp.sin(ang), jnp.sin(ang)], axis=-1).reshape(L, GLA_DK)
    return cos, sin


def _gla_kernel(q_ref, k_ref, v_ref, g_ref, a_ref, cos_ref, sin_ref, w2_ref, b2_ref, ng_ref, s0_ref, o_ref, st_ref,
                *, n_chunks):
    C = GLA_CHUNK
    quarter = GLA_DK // 4
    lane = lax.broadcasted_iota(jnp.int32, (C, GLA_DK), 1)
    first_half = (lane % (2 * quarter)) < quarter
    row = lax.broadcasted_iota(jnp.int32, (C, C), 0)
    col = lax.broadcasted_iota(jnp.int32, (C, C), 1)
    tri = (row >= col, row <= col)
    o_ref[...] = jnp.zeros_like(o_ref)
    st_ref[...] = s0_ref[...]

    def rope(x, cos, sin):
        swapped = jnp.where(first_half, pltpu.roll(x, GLA_DK - quarter, 1), pltpu.roll(x, quarter, 1))
        return x * cos + swapped * sin

    def chunk(c, d):
        sl = pl.ds(pl.multiple_of(c * C, C), C)
        cos, sin = cos_ref[sl, :], sin_ref[sl, :]
        q = rope(q_ref[sl, :], cos, sin) * GLA_DK ** -0.5
        k = rope(k_ref[sl, :], cos, sin)
        v = v_ref[sl, :].astype(BF16)
        a = a_ref[sl, d * GLA_GATE_RANK:(d + 1) * GLA_GATE_RANK]
        pre = jnp.dot(a.astype(BF16), w2_ref[d].astype(BF16), preferred_element_type=F32) + b2_ref[d]
        log_a = jax.nn.log_sigmoid(pre) / GLA_TAU
        mask = tri[d]
        bcum = jnp.dot(mask.astype(F32), log_a, preferred_element_type=F32, precision=lax.Precision.HIGHEST)
        blast = jnp.sum(log_a, axis=0, keepdims=True)
        q_in = (q * jnp.exp(bcum)).astype(BF16)
        k_in = (k * jnp.exp(-bcum)).astype(BF16)
        k_st = (k * jnp.exp(blast - bcum)).astype(BF16)
        att = lax.dot_general(q_in, k_in, (((1,), (1,)), ((), ())), preferred_element_type=F32)
        att = jnp.where(mask, att, 0.0).astype(BF16)
        st = st_ref[d]
        o = jnp.dot(att, v, preferred_element_type=F32)
        o = o + lax.dot_general(q_in, st.astype(BF16), (((1,), (1,)), ((), ())), preferred_element_type=F32)
        o_ref[sl, :] += o
        kv_t = lax.dot_general(v, k_st, (((0,), (0,)), ((), ())), preferred_element_type=F32)
        st_ref[d] = st * jnp.exp(blast) + kv_t

    def body(i, carry):
        chunk(i, 0)
        chunk(n_chunks - 1 - i, 1)
        return carry

    lax.fori_loop(0, n_chunks, body, 0, unroll=2)
    o = o_ref[...]
    o = o * lax.rsqrt(jnp.mean(o * o, axis=-1, keepdims=True) + RMS_EPS) * ng_ref[...]
    o_ref[...] = o * jax.nn.silu(g_ref[...])


def gla_segment(z, za, s0, w2, b2, norm_g, B, L, rotary):
    H, dk, dv = GLA_HEADS, GLA_DK, GLA_DV
    cos, sin = rope_tables(L, rotary)
    qb, kb, vb, gb = IN_OFF[2] // dk, IN_OFF[3] // dk, IN_OFF[4] // dv, IN_OFF[5] // dv
    return pl.pallas_call(
        functools.partial(_gla_kernel, n_chunks=L // GLA_CHUNK),
        grid=(B, H),
        in_specs=[pl.BlockSpec((L, dk), lambda b, h: (b, qb + h)),
                  pl.BlockSpec((L, dk), lambda b, h: (b, kb + h)),
                  pl.BlockSpec((L, dv), lambda b, h: (b, vb + h)),
                  pl.BlockSpec((L, dv), lambda b, h: (b, gb + h)),
                  pl.BlockSpec((L, 2 * GLA_GATE_RANK), lambda b, h: (b, 0)),
                  pl.BlockSpec((L, dk), lambda b, h: (0, 0)),
                  pl.BlockSpec((L, dk), lambda b, h: (0, 0)),
                  pl.BlockSpec((2, GLA_GATE_RANK, dk), lambda b, h: (0, 0, h)),
                  pl.BlockSpec((2, 1, dk), lambda b, h: (0, 0, h)),
                  pl.BlockSpec((1, dv), lambda b, h: (0, 0)),
                  pl.BlockSpec((None, None, 2, dv, dk), lambda b, h: (b, h, 0, 0, 0))],
        out_specs=[pl.BlockSpec((L, dv), lambda b, h: (b, h)),
                   pl.BlockSpec((None, None, 2, dv, dk), lambda b, h: (b, h, 0, 0, 0))],
        out_shape=[jax.ShapeDtypeStruct((B * L, H * dv), F32), jax.ShapeDtypeStruct((B, H, 2, dv, dk), F32)],
        compiler_params=pltpu.CompilerParams(dimension_semantics=("arbitrary", "arbitrary"),
                                             vmem_limit_bytes=V7X_VMEM_LIMIT_BYTES),
        name="gla_segment",
    )(z, z, z, z, za, cos, sin, w2, b2.reshape(2, 1, -1), norm_g.reshape(1, dv), s0)


def gla_mixer(zl, zc, za_l, za_c, w2, b2, norm_g, B, L, Lc):
    s0 = jnp.zeros((B, GLA_HEADS, 2, GLA_DV, GLA_DK), F32)
    out_c, s_ctx = gla_segment(zc, za_c, s0, w2, b2, norm_g, B, Lc, False)
    out_l, _ = gla_segment(zl, za_l, s_ctx, w2, b2, norm_g, B, L, True)
    return out_l, out_c


def _cmul(a, b):
    return a[0] * b[0] - a[1] * b[1], a[0] * b[1] + a[1] * b[0]


S5_T = 16
S5_GQ = V7X_LANES // S5_GROUP
S5_NQ = S5_GROUPS // S5_GQ
S5_NS = 2 * S5_GQ * S5_STATE
S5_GP = V7X_LANES // S5_STATE
S5_NP = S5_GQ // S5_GP
S5_PU = S5_GP * S5_T * S5_GROUP
S5_PS = S5_GP * S5_STATE


def s5_operators(a_re, a_im, log_dt, b_re, b_im, c_re, c_im):
    T, GQ, NQ, NP, GP, P, I, G = S5_T, S5_GQ, S5_NQ, S5_NP, S5_GP, S5_STATE, S5_GROUP, S5_GROUPS
    hp = lax.Precision.HIGHEST
    TI = T * I
    lane = jnp.arange(TI)
    step_hot = (lane[None, :] // I == jnp.arange(T)[:, None]).astype(F32)
    chan_hot = (lane[None, :] % I == jnp.arange(I)[:, None]).astype(F32)
    steps = jnp.arange(T)
    same_chan = lane[:, None] % I == lane[None, :] % I
    dstep = lane[None, :] // I - lane[:, None] // I
    shift = [(same_chan[None] & (dstep[None] == steps[:, None, None])).astype(F32),
             (same_chan[None] & (dstep[None] + 2 * (lane[:, None] // I)[None] == steps[:, None, None])).astype(F32)]

    def pair_blocks(blocks, axis_rows, axis_cols):
        rows = []
        for g in range(GP):
            rows.append(jnp.concatenate([blocks[g] if h == g else jnp.zeros_like(blocks[g]) for h in range(GP)],
                                        axis=axis_cols))
        return jnp.concatenate(rows, axis=axis_rows)

    def split_groups(x):
        x = x.reshape(NQ, NP, GP, *x.shape[1:])
        return [x[:, :, g] for g in range(GP)]

    w_parts, v_parts, at_parts = [], [], []
    m_sum = 0.0
    for d in range(2):
        ar, ai = a_re[d].astype(F32), a_im[d].astype(F32)
        dt = jnp.exp(log_dt[d].astype(F32))[:, None]

        def power_rows(e):
            mag = jnp.exp((ar * dt)[:, None, :] * e[None, :, None])
            ph = (ai * dt)[:, None, :] * e[None, :, None]
            return mag * jnp.cos(ph), mag * jnp.sin(ph)

        def power_cols(e):
            pr, pi = power_rows(e)
            return (jnp.einsum('gtp,tc->gpc', pr, step_hot, precision=hp),
                    jnp.einsum('gtp,tc->gpc', pi, step_hot, precision=hp))

        def rows_power(e):
            return tuple(jnp.repeat(x, I, axis=1) for x in power_rows(e))

        a1 = power_rows(jnp.ones((1,), F32))
        zr, zi = a1[0][:, 0] - 1.0, a1[1][:, 0]
        den = ar * ar + ai * ai
        qr, qi = (zr * ar + zi * ai) / den, (zi * ar - zr * ai) / den
        bt = (b_re[d].astype(F32).transpose(0, 2, 1), b_im[d].astype(F32).transpose(0, 2, 1))
        bbar = _cmul((qr[:, None, :], qi[:, None, :]), bt)
        b_rows = tuple(jnp.tile(x, (1, T, 1)) for x in bbar)
        cm = (c_re[d].astype(F32).transpose(0, 2, 1), c_im[d].astype(F32).transpose(0, 2, 1))
        c_cols = tuple(jnp.einsum('gpi,ic->gpc', x, chan_hot, precision=hp) for x in cm)

        ex = (T - 1 - steps) if d == 0 else steps
        w = _cmul(rows_power(ex.astype(F32)), b_rows)
        wg = [split_groups(x) for x in w]
        w_parts.append(jnp.concatenate([pair_blocks(wg[0], 2, 3), pair_blocks(wg[1], 2, 3)], axis=3))
        ca = _cmul(power_cols(steps.astype(F32)), c_cols)
        kc = (jnp.einsum('gjp,gpc->gjc', bbar[0], ca[0], precision=hp)
              - jnp.einsum('gjp,gpc->gjc', bbar[1], ca[1], precision=hp))
        m_sum = m_sum + jnp.einsum('gjk,skc->gsjc', kc, shift[d], precision=hp).reshape(G, TI, TI)
        ey = (steps + 1) if d == 0 else (T - steps)
        v = _cmul(power_cols(ey.astype(F32)), c_cols)
        vg = [split_groups(x) for x in (v[0], -v[1])]
        v_parts.append(jnp.concatenate([pair_blocks(vg[0], 2, 3), pair_blocks(vg[1], 2, 3)], axis=2))
        at = power_rows(jnp.full((1,), float(T), F32))
        at_parts.append(jnp.concatenate([at[0].reshape(NQ, GQ * P), at[1].reshape(NQ, GQ * P)], axis=-1)[:, None, :])
    m_pair = pair_blocks(split_groups(m_sum), 2, 3)
    rhs = jnp.concatenate([w_parts[0], w_parts[1], m_pair], axis=-1).astype(BF16)
    return rhs, jnp.stack(v_parts).astype(BF16), jnp.stack(at_parts)


def s5_lane_permutation():
    r = jnp.arange(S5_T * V7X_LANES)
    s, g, j = r // V7X_LANES, (r % V7X_LANES) // S5_GROUP, r % S5_GROUP
    c = (g // S5_GP) * S5_PU + (g % S5_GP) * (S5_T * S5_GROUP) + s * S5_GROUP + j
    return (c[:, None] == r[None, :]).astype(BF16)


def _s5_in_kernel(u_ref, perm_ref, rhs_ref, s_ref, y_ref):
    half = S5_NS // 2
    u_nat = jnp.concatenate([u_ref[:, s, :] for s in range(S5_T)], axis=-1).astype(BF16)
    u = jnp.dot(u_nat, perm_ref[...], preferred_element_type=F32).astype(BF16)
    for n in range(S5_NP):
        r = jnp.dot(u[:, n * S5_PU:(n + 1) * S5_PU], rhs_ref[n], preferred_element_type=F32)
        for d in range(2):
            s_ref[d, :, n * S5_PS:(n + 1) * S5_PS] = r[:, 2 * d * S5_PS:(2 * d + 1) * S5_PS]
            s_ref[d, :, half + n * S5_PS:half + (n + 1) * S5_PS] = r[:, (2 * d + 1) * S5_PS:(2 * d + 2) * S5_PS]
        y_ref[:, n * S5_PU:(n + 1) * S5_PU] = r[:, 4 * S5_PS:]


def _s5_scan_kernel(s_ref, at_ref, x_ref, *, n_ctx, n_lat, batch):
    d = pl.program_id(0)
    half = S5_NS // 2
    a_r = at_ref[:, :half]
    a_i = at_ref[:, half:]

    def run(b, base, n, carry):
        def body(i, st):
            xr, xi = st
            c = i + d * (n - 1 - 2 * i)
            row = base + b * n + c
            x_ref[pl.ds(row, 1), :] = jnp.concatenate([xr, xi], axis=-1)
            s = s_ref[pl.ds(row, 1), :]
            return a_r * xr - a_i * xi + s[:, :half], a_r * xi + a_i * xr + s[:, half:]
        return lax.fori_loop(0, n, body, carry)

    for b in range(batch):
        zero = jnp.zeros((1, half), F32)
        st = run(b, 0, n_ctx, (zero, zero))
        run(b, batch * n_ctx, n_lat, st)


def _s5_out_kernel(x_ref, v_ref, yin_ref, u_ref, perm_ref, dsk_ref, y_ref):
    half = S5_NS // 2
    parts = []
    for n in range(S5_NP):
        y = yin_ref[:, n * S5_PU:(n + 1) * S5_PU]
        for d in range(2):
            x = jnp.concatenate([x_ref[d, :, n * S5_PS:(n + 1) * S5_PS],
                                 x_ref[d, :, half + n * S5_PS:half + (n + 1) * S5_PS]], axis=-1)
            y = y + jnp.dot(x.astype(BF16), v_ref[d, n], preferred_element_type=F32)
        parts.append(y)
    y = jnp.concatenate(parts, axis=-1)
    y_hi = y.astype(BF16)
    y_lo = (y - y_hi.astype(F32)).astype(BF16)
    nt = (((1,), (1,)), ((), ()))
    y = (lax.dot_general(y_hi, perm_ref[...], nt, preferred_element_type=F32)
         + lax.dot_general(y_lo, perm_ref[...], nt, preferred_element_type=F32))
    for t in range(S5_T):
        y_ref[:, t, :] = y[:, t * V7X_LANES:(t + 1) * V7X_LANES] + dsk_ref[...] * u_ref[:, t, :]


def s5_core(uc, ul, ops, d_skip, B, L, Lc):
    rhs, vmat, a_t = ops
    T, NQ, NS, NP = S5_T, S5_NQ, S5_NS, S5_NP
    n_ctx, n_lat = Lc // T, L // T
    R = B * (n_ctx + n_lat)
    RT = R // 2
    tw = T * V7X_LANES
    assert RT % V7X_SUBLANES == 0
    u = jnp.concatenate([uc.reshape(B * n_ctx, T, S5_WIDTH), ul.reshape(B * n_lat, T, S5_WIDTH)], axis=0)
    perm = s5_lane_permutation()
    params = pltpu.CompilerParams(dimension_semantics=("arbitrary", "arbitrary"),
                                  vmem_limit_bytes=V7X_VMEM_LIMIT_BYTES)
    perm_spec = pl.BlockSpec((tw, tw), lambda q, i: (0, 0), pipeline_mode=pl.Buffered(1))
    s, y_in = pl.pallas_call(
        _s5_in_kernel,
        grid=(NQ, R // RT),
        in_specs=[pl.BlockSpec((RT, T, V7X_LANES), lambda q, i: (i, 0, q)),
                  perm_spec,
                  pl.BlockSpec((None, NP, S5_PU, 4 * S5_PS + S5_PU), lambda q, i: (q, 0, 0, 0))],
        out_specs=[pl.BlockSpec((2, None, RT, NS), lambda q, i: (0, q, i, 0)),
                   pl.BlockSpec((None, RT, tw), lambda q, i: (q, i, 0))],
        out_shape=[jax.ShapeDtypeStruct((2, NQ, R, NS), F32), jax.ShapeDtypeStruct((NQ, R, tw), F32)],
        compiler_params=params,
        name="s5_in",
    )(u, perm, rhs)
    x = pl.pallas_call(
        functools.partial(_s5_scan_kernel, n_ctx=n_ctx, n_lat=n_lat, batch=B),
        grid=(2, NQ),
        in_specs=[pl.BlockSpec((None, None, R, NS), lambda d, q: (d, q, 0, 0)),
                  pl.BlockSpec((None, None, 1, NS), lambda d, q: (d, q, 0, 0))],
        out_specs=pl.BlockSpec((None, None, R, NS), lambda d, q: (d, q, 0, 0)),
        out_shape=jax.ShapeDtypeStruct((2, NQ, R, NS), F32),
        compiler_params=pltpu.CompilerParams(dimension_semantics=("arbitrary", "arbitrary")),
        name="s5_scan",
    )(s, a_t)
    y = pl.pallas_call(
        _s5_out_kernel,
        grid=(NQ, R // RT),
        in_specs=[pl.BlockSpec((2, None, RT, NS), lambda q, i: (0, q, i, 0)),
                  pl.BlockSpec((2, None, NP, 2 * S5_PS, S5_PU), lambda q, i: (0, q, 0, 0, 0)),
                  pl.BlockSpec((None, RT, tw), lambda q, i: (q, i, 0)),
                  pl.BlockSpec((RT, T, V7X_LANES), lambda q, i: (i, 0, q)),
                  perm_spec,
                  pl.BlockSpec((1, V7X_LANES), lambda q, i: (0, q))],
        out_specs=pl.BlockSpec((RT, T, V7X_LANES), lambda q, i: (i, 0, q)),
        out_shape=jax.ShapeDtypeStruct((R, T, S5_WIDTH), F32),
        compiler_params=params,
        name="s5_out",
    )(x, vmat, y_in, u, perm, d_skip.astype(F32).reshape(1, S5_WIDTH))
    y = y.reshape(R * T, S5_WIDTH)
    return y[:B * Lc], y[B * Lc:]


def s5_glu(y, w, b, l):
    y = jax.nn.gelu(y)
    return (y * jax.nn.sigmoid(mm(y, w, (l,)) + b[l].astype(F32))).astype(BF16)


def _gate_merge_kernel(h_ref, b0_ref, b1_ref, b2_ref, b3_ref, wg0, wg1, wg2, wg3, wb0, wb1, wb2, wb3, bias_ref,
                       o_ref, wgb_ref, wbb_ref):
    wg_refs = (wg0, wg1, wg2, wg3)
    wb_refs = (wb0, wb1, wb2, wb3)
    br_refs = (b0_ref, b1_ref, b2_ref, b3_ref)

    @pl.when(pl.program_id(1) == 0)
    def _():
        for k in range(N_BRANCH):
            wgb_ref[k] = wg_refs[k][...].astype(BF16)
            wbb_ref[k] = wb_refs[k][...].astype(BF16)

    h = h_ref[...]
    acc = None
    for k in range(N_BRANCH):
        gate = jax.nn.sigmoid(jnp.dot(h, wgb_ref[k], preferred_element_type=F32) + bias_ref[k])
        term = gate * jnp.dot(br_refs[k][...], wbb_ref[k], preferred_element_type=F32)
        acc = term if acc is None else acc + term
    o_ref[...] = acc.astype(o_ref.dtype)


def gate_merge(h, branches, w_gate, b_gate, w_branch, l):
    M, D = h.shape
    W = branches[0].shape[1]
    tn = 256
    tm = min(M, 512)

    def wspec(k, rows):
        return pl.BlockSpec((None, None, rows, tn), lambda j, i: (l, k, 0, j), pipeline_mode=pl.Buffered(1))

    return pl.pallas_call(
        _gate_merge_kernel,
        grid=(D // tn, M // tm),
        in_specs=[pl.BlockSpec((tm, D), lambda j, i: (i, 0))]
                 + [pl.BlockSpec((tm, W), lambda j, i: (i, 0)) for _ in range(N_BRANCH)]
                 + [wspec(k, D) for k in range(N_BRANCH)]
                 + [wspec(k, W) for k in range(N_BRANCH)]
                 + [pl.BlockSpec((None, N_BRANCH, 1, tn), lambda j, i: (l, 0, 0, j))],
        out_specs=pl.BlockSpec((tm, tn), lambda j, i: (i, j)),
        out_shape=jax.ShapeDtypeStruct((M, D), BF16),
        scratch_shapes=[pltpu.VMEM((N_BRANCH, D, tn), BF16), pltpu.VMEM((N_BRANCH, W, tn), BF16)],
        compiler_params=pltpu.CompilerParams(dimension_semantics=("arbitrary", "arbitrary"),
                                             vmem_limit_bytes=V7X_VMEM_LIMIT_BYTES),
        name="gate_merge",
    )(h, *branches, *([w_gate] * N_BRANCH), *([w_branch] * N_BRANCH),
      b_gate.reshape(b_gate.shape[0], N_BRANCH, 1, D))


def _moe_gather_kernel(idx_ref, h_ref, o_ref, stage_ref, sem, *, n_tokens, n_experts):
    g = pl.program_id(0)
    cap = o_ref.shape[0]
    b = g // n_experts

    def row_copy(c):
        row = b * n_tokens + idx_ref[g * cap + c]
        return pltpu.make_async_copy(h_ref.at[pl.ds(row, 1), :], stage_ref.at[pl.ds(c, 1), :], sem)

    def start(c, carry):
        row_copy(c).start()
        return carry

    def wait(c, carry):
        row_copy(c).wait()
        return carry

    lax.fori_loop(0, cap, start, 0, unroll=8)
    lax.fori_loop(0, cap, wait, 0, unroll=8)
    o_ref[...] = stage_ref[...].astype(BF16)


def moe_gather(h, idx):
    B, N, D = h.shape
    _, E, cap = idx.shape
    return pl.pallas_call(
        functools.partial(_moe_gather_kernel, n_tokens=N, n_experts=E),
        grid_spec=pltpu.PrefetchScalarGridSpec(
            num_scalar_prefetch=1,
            grid=(B * E,),
            in_specs=[pl.BlockSpec(memory_space=pl.ANY)],
            out_specs=pl.BlockSpec((None, cap, D), lambda g, idx_ref: (g, 0, 0)),
            scratch_shapes=[pltpu.VMEM((cap, D), F32), pltpu.SemaphoreType.DMA(())],
        ),
        out_shape=jax.ShapeDtypeStruct((B * E, cap, D), BF16),
        compiler_params=pltpu.CompilerParams(dimension_semantics=("arbitrary",),
                                             vmem_limit_bytes=V7X_VMEM_LIMIT_BYTES),
        name="moe_gather",
    )(idx.reshape(-1).astype(jnp.int32), h.reshape(B * N, D)).reshape(B, E, cap, D)


def _expert_gu_kernel(x_ref, wg_ref, wu_ref, o_ref):
    nb, cap, d = x_ref.shape
    x = x_ref[...].reshape(nb * cap, d)
    g = jnp.dot(x, wg_ref[...].astype(BF16), preferred_element_type=F32)
    u = jnp.dot(x, wu_ref[...].astype(BF16), preferred_element_type=F32)
    o_ref[...] = (jax.nn.silu(g) * u).astype(BF16).reshape(o_ref.shape)


def expert_gu(xs, w_gu, l):
    B, E, C, D = xs.shape
    F = w_gu.shape[3] // 2
    tn = 256
    nj = F // tn
    return pl.pallas_call(
        _expert_gu_kernel,
        grid=(E, nj),
        in_specs=[pl.BlockSpec((B, None, C, D), lambda e, j: (0, e, 0, 0)),
                  pl.BlockSpec((None, None, D, tn), lambda e, j: (l, e, 0, j)),
                  pl.BlockSpec((None, None, D, tn), lambda e, j: (l, e, 0, nj + j))],
        out_specs=pl.BlockSpec((B, None, C, tn), lambda e, j: (0, e, 0, j)),
        out_shape=jax.ShapeDtypeStruct((B, E, C, F), BF16),
        compiler_params=pltpu.CompilerParams(dimension_semantics=("arbitrary",) * 2,
                                             vmem_limit_bytes=V7X_VMEM_LIMIT_BYTES),
        name="expert_gu",
    )(xs, w_gu, w_gu)


def _expert_down_kernel(idx_ref, a_ref, w_ref, gate_ref, o_ref, y_ref, *, n_experts):
    b, e = pl.program_id(0), pl.program_id(2)
    cap = a_ref.shape[0]

    @pl.when(e == 0)
    def _():
        o_ref[...] = jnp.zeros_like(o_ref)

    y = jnp.dot(a_ref[...], w_ref[...].astype(BF16), preferred_element_type=F32)
    y_ref[...] = y * gate_ref[...]
    base = (b * n_experts + e) * cap

    def add_row(r, carry):
        tok = idx_ref[base + r]
        o_ref[pl.ds(tok, 1), :] += y_ref[pl.ds(r, 1), :]
        return carry

    lax.fori_loop(0, cap, add_row, 0, unroll=8)


def expert_down(act, w_down, gate, idx, n_tokens, l):
    B, E, C, F = act.shape
    D = w_down.shape[3]
    tn = min(1024, D)
    assert D % tn == 0
    return pl.pallas_call(
        functools.partial(_expert_down_kernel, n_experts=E),
        grid_spec=pltpu.PrefetchScalarGridSpec(
            num_scalar_prefetch=1,
            grid=(B, D // tn, E),
            in_specs=[pl.BlockSpec((None, None, C, F), lambda b, j, e, idx_ref: (b, e, 0, 0)),
                      pl.BlockSpec((None, None, F, tn), lambda b, j, e, idx_ref: (l, e, 0, j)),
                      pl.BlockSpec((None, None, C, 1), lambda b, j, e, idx_ref: (b, e, 0, 0))],
            out_specs=pl.BlockSpec((None, n_tokens, tn), lambda b, j, e, idx_ref: (b, 0, j)),
            scratch_shapes=[pltpu.VMEM((C, tn), F32)],
        ),
        out_shape=jax.ShapeDtypeStruct((B, n_tokens, D), F32),
        compiler_params=pltpu.CompilerParams(dimension_semantics=("arbitrary",) * 3,
                                             vmem_limit_bytes=V7X_VMEM_LIMIT_BYTES),
        name="expert_down",
    )(idx.reshape(-1).astype(jnp.int32), act, w_down, gate[..., None])


def moe_ec(h, router_w, w_gu, w_down, l):
    B, N, D = h.shape
    cap = max(1, EC_CAPACITY * N // N_EXPERTS)
    logits = jnp.einsum('bnd,de->bne', h, router_w[l], precision=lax.Precision.HIGHEST)
    aff = jax.nn.softmax(logits.astype(F32), axis=-1)
    gate, idx = lax.top_k(jnp.swapaxes(aff, 1, 2), cap)
    return expert_down(expert_gu(moe_gather(h, idx), w_gu, l), w_down, gate, idx, N, l)


def kernel(x, c, ctx, c_ctx, ada_w, ada_b, mix_pre_g, mix_post_g, ffn_pre_g, ffn_post_g, w_in, na_rpb,
           hy_conv_w, hy_conv_b, hy_w1, hy_b1, hy_w2, hy_b2, hy_w3, hy_freq, hy_bias,
           gla_w2, gla_b2, gla_norm_g, s5_a_re, s5_a_im, s5_log_dt, s5_b_re, s5_b_im, s5_c_re, s5_c_im,
           s5_d, s5_glu_w, s5_glu_b, w_branch, w_gate, b_gate, w_out, router_w, ex_w_gu, ex_w_down):
    xl, xc = x, ctx
    B, L, Lc = x.shape[0], x.shape[1], ctx.shape[1]
    for l in range(DEPTH):
        need_ctx = l < DEPTH - 1
        cvecs = jnp.concatenate([c, c_ctx[None, :]], axis=0)
        mod = mm(jax.nn.silu(cvecs), ada_w, (l,)) + ada_b[l]
        ml = [p[:, None, :] for p in jnp.split(mod[:B], 6, axis=-1)]
        mc = jnp.split(mod[B], 6, axis=-1)

        hl = rms_norm(xl, mix_pre_g[l]) * (1.0 + ml[1]) + ml[0]
        hc = rms_norm(xc, mix_pre_g[l]) * (1.0 + mc[1]) + mc[0]
        hl2, hc2 = hl.reshape(B * L, D_MODEL).astype(BF16), hc.reshape(B * Lc, D_MODEL).astype(BF16)
        w_a, w_s5 = w_in[l, :, IN_MAIN:IN_MAIN + 2 * GLA_GATE_RANK], w_in[l, :, IN_MAIN + 2 * GLA_GATE_RANK:]
        zl, zc = mm(hl2, w_in, (l,), IN_MAIN), mm(hc2, w_in, (l,), IN_MAIN)
        zl3, zc3 = zl.reshape(B, L, IN_MAIN), zc.reshape(B, Lc, IN_MAIN)

        na_l = na_latent(zl, zc, na_bias_table(na_rpb[l], L // GRID_W), B, L, Lc)
        hy_args = (hy_conv_w[l], hy_conv_b[l], hy_w1[l], hy_b1[l], hy_w2[l], hy_b2[l], hy_w3[l], hy_freq[l], hy_bias[l])
        hy_l = hyena_mixer(zl3[..., IN_OFF[1]:IN_OFF[2]], *hy_args, B, L).astype(BF16)
        gla_l, gla_c = gla_mixer(zl, zc, mm(hl2, w_a), mm(hc2, w_a), gla_w2[l], gla_b2[l], gla_norm_g[l], B, L, Lc)
        s5_ops = s5_operators(s5_a_re[l], s5_a_im[l], s5_log_dt[l], s5_b_re[l], s5_b_im[l], s5_c_re[l], s5_c_im[l])
        s5_yc, s5_yl = s5_core(mm(hc2, w_s5), mm(hl2, w_s5), s5_ops, s5_d[l], B, L, Lc)
        s5_l = s5_glu(s5_yl, s5_glu_w, s5_glu_b, l)
        merged = gate_merge(hl2, (na_l, hy_l, gla_l.astype(BF16), s5_l), w_gate, b_gate, w_branch, l)
        yl = mm(merged, w_out, (l,)).reshape(B, L, D_MODEL)
        xl = xl + ml[2] * rms_norm(yl, mix_post_g[l])
        if need_ctx:
            na_c = na_context(zc, B, Lc).astype(BF16)
            hy_c = hyena_mixer(zc3[..., IN_OFF[1]:IN_OFF[2]], *hy_args, B, Lc).astype(BF16)
            s5_c = s5_glu(s5_yc, s5_glu_w, s5_glu_b, l)
            merged = gate_merge(hc2, (na_c, hy_c, gla_c.astype(BF16), s5_c), w_gate, b_gate, w_branch, l)
            yc = mm(merged, w_out, (l,)).reshape(B, Lc, D_MODEL)
            xc = xc + mc[2] * rms_norm(yc, mix_post_g[l])

        hl = rms_norm(xl, ffn_pre_g[l]) * (1.0 + ml[4]) + ml[3]
        xl = xl + ml[5] * rms_norm(moe_ec(hl, router_w, ex_w_gu, ex_w_down, l), ffn_post_g[l])
        if need_ctx:
            hc = rms_norm(xc, ffn_pre_g[l]) * (1.0 + mc[4]) + mc[3]
            xc = xc + mc[5] * rms_norm(moe_ec(hc, router_w, ex_w_gu, ex_w_down, l), ffn_post_g[l])
    return xl
```
